```python
import math
import jax, jax.numpy as jnp
from jax import lax
import numpy as np

D_MODEL = 1024
BATCH = 8
SEQ = 2048
DEPTH = 1

CHUNK = 64
D_RNN = 1280
H_RNN = 16
RNN_BLOCK = D_RNN // H_RNN
CONV_A = 4
LRU_C = 8.0
H_ATT = 16
HEAD_DIM = 64
D_ATT = H_ATT * HEAD_DIM
Q_BLOCK = 128
D_FF = 3 * D_MODEL
CONV_F = 3
N_BRANCH = 2
D_IN = 2 * D_RNN + 3 * D_ATT + H_ATT + N_BRANCH * D_MODEL
RMS_EPS = 1e-6

kernel_name = "hybrid_rglru_forgetting_attn_convffn_block"


def rms_norm(x, gain):
    xf = x.astype(jnp.float32)
    y = xf * lax.rsqrt(jnp.mean(xf * xf, axis=-1, keepdims=True) + RMS_EPS)
    return (y * gain.astype(jnp.float32)).astype(x.dtype)


def causal_dwconv(x, w, b):
    k_w, c = w.shape
    y = lax.conv_general_dilated(
        x, w[:, None, :].astype(x.dtype), window_strides=(1,), padding=[(k_w - 1, 0)],
        dimension_numbers=("NWC", "WIO", "NWC"), feature_group_count=c)
    return y + b.astype(x.dtype)


def rg_lru(xc, w_a, b_a, w_x, b_x, lam):
    bsz, s, _ = xc.shape
    xb = xc.reshape(bsz, s, H_RNN, RNN_BLOCK)
    r = jax.nn.sigmoid(jnp.einsum("bshi,hij->bshj", xb, w_a).reshape(bsz, s, D_RNN) + b_a)
    i = jax.nn.sigmoid(jnp.einsum("bshi,hij->bshj", xb, w_x).reshape(bsz, s, D_RNN) + b_x)
    log_a = -LRU_C * r.astype(jnp.float32) * jax.nn.softplus(-lam.astype(jnp.float32))
    a = jnp.exp(log_a)
    b_term = jnp.sqrt(-jnp.expm1(2.0 * log_a)) * (i * xc).astype(jnp.float32)

    def combine(left, right):
        a1, h1 = left
        a2, h2 = right
        return a1 * a2, a2 * h1 + h2

    _, h = lax.associative_scan(combine, (a, b_term), axis=1)
    return h.astype(xc.dtype)


def forgetting_attention(q, k, v, f_logit):
    bsz, s, _ = q.shape
    q = q.reshape(bsz, s, H_ATT, HEAD_DIM).transpose(0, 2, 1, 3)
    k = k.reshape(bsz, s, H_ATT, HEAD_DIM).transpose(0, 2, 1, 3)
    v = v.reshape(bsz, s, H_ATT, HEAD_DIM).transpose(0, 2, 1, 3)
    cum = jnp.cumsum(jax.nn.log_sigmoid(f_logit.astype(jnp.float32)), axis=1).transpose(0, 2, 1)
    scale = 1.0 / math.sqrt(HEAD_DIM)
    outs = []
    for blk in range(s // Q_BLOCK):
        qs, qe = blk * Q_BLOCK, (blk + 1) * Q_BLOCK
        sc = jnp.einsum("bhqd,bhkd->bhqk", q[:, :, qs:qe], k[:, :, :qe]).astype(jnp.float32) * scale
        sc = sc + cum[:, :, qs:qe, None] - cum[:, :, None, :qe]
        allowed = jnp.arange(qe)[None, :] <= jnp.arange(qs, qe)[:, None]
        sc = jnp.where(allowed, sc, -jnp.inf)
        p = jax.nn.softmax(sc, axis=-1).astype(v.dtype)
        outs.append(jnp.einsum("bhqk,bhkd->bhqd", p, v[:, :, :qe]))
    o = jnp.concatenate(outs, axis=2)
    return o.transpose(0, 2, 1, 3).reshape(bsz, s, D_ATT)


def setup_inputs(seed: int = 0) -> dict:
    key = jax.random.key(seed)
    ks = jax.random.split(key, 24)
    f32 = jnp.float32
    nrm = lambda k, shape, fan_in: jax.random.normal(k, shape, f32) * (fan_in ** -0.5)
    gain = lambda k: 1.0 + 0.05 * jax.random.normal(k, (DEPTH, D_MODEL), f32)
    small = lambda k, shape: 0.01 * jax.random.normal(k, shape, f32)
    u = jax.random.uniform(ks[9], (DEPTH, D_RNN), f32, 0.9, 0.999)
    a0 = u ** (1.0 / LRU_C)
    lru_lambda = jnp.log(a0) - jnp.log1p(-a0)
    return {
        "x": jax.random.normal(ks[0], (BATCH, SEQ, D_MODEL), f32),
        "mix_norm_pre": gain(ks[1]),
        "mix_norm_post": gain(ks[2]),
        "w_in": nrm(ks[3], (DEPTH, D_MODEL, D_IN), D_MODEL),
        "conv_a_w": nrm(ks[4], (DEPTH, CONV_A, D_RNN), CONV_A),
        "conv_a_b": small(ks[5], (DEPTH, D_RNN)),
        "w_rg_a": nrm(ks[6], (DEPTH, H_RNN, RNN_BLOCK, RNN_BLOCK), RNN_BLOCK),
        "b_rg_a": small(ks[7], (DEPTH, D_RNN)),
        "w_rg_x": nrm(ks[8], (DEPTH, H_RNN, RNN_BLOCK, RNN_BLOCK), RNN_BLOCK),
        "b_rg_x": small(ks[10], (DEPTH, D_RNN)),
        "lru_lambda": lru_lambda,
        "b_forget": jax.random.uniform(ks[11], (DEPTH, H_ATT), f32, 1.0, 4.0),
        "b_merge": small(ks[12], (DEPTH, N_BRANCH, D_MODEL)),
        "w_proj_a": nrm(ks[13], (DEPTH, D_RNN, D_MODEL), D_RNN),
        "w_proj_b": nrm(ks[14], (DEPTH, D_ATT, D_MODEL), D_ATT),
        "w_out": nrm(ks[15], (DEPTH, D_MODEL, D_MODEL), D_MODEL),
        "ffn_norm_pre": gain(ks[16]),
        "ffn_norm_post": gain(ks[17]),
        "w_up": nrm(ks[18], (DEPTH, D_MODEL, 2 * D_FF), D_MODEL),
        "conv_f_w": nrm(ks[19], (DEPTH, CONV_F, 2 * D_FF), CONV_F),
        "conv_f_b": small(ks[20], (DEPTH, 2 * D_FF)),
        "w_down": nrm(ks[21], (DEPTH, D_FF, D_MODEL), D_FF),
    }


def reference(x, mix_norm_pre, mix_norm_post, w_in, conv_a_w, conv_a_b, w_rg_a, b_rg_a,
              w_rg_x, b_rg_x, lru_lambda, b_forget, b_merge, w_proj_a, w_proj_b, w_out,
              ffn_norm_pre, ffn_norm_post, w_up, conv_f_w, conv_f_b, w_down):
    offs = np.cumsum([D_RNN, D_RNN, D_ATT, D_ATT, D_ATT, H_ATT, D_MODEL]).tolist()
    for layer in range(DEPTH):
        h = rms_norm(x, mix_norm_pre[layer])
        proj = jnp.einsum("bsd,de->bse", h, w_in[layer])
        xa, ga, q, k, v, f_logit, g_a, g_b = jnp.split(proj, offs, axis=-1)
        xc = causal_dwconv(xa, conv_a_w[layer], conv_a_b[layer])
        hr = rg_lru(xc, w_rg_a[layer], b_rg_a[layer], w_rg_x[layer], b_rg_x[layer], lru_lambda[layer])
        y_a = jax.nn.gelu(ga) * hr
        y_b = forgetting_attention(q, k, v, f_logit + b_forget[layer])
        merged = (jax.nn.sigmoid(g_a + b_merge[layer, 0]) * jnp.einsum("bse,ed->bsd", y_a, w_proj_a[layer])
                  + jax.nn.sigmoid(g_b + b_merge[layer, 1]) * jnp.einsum("bse,ed->bsd", y_b, w_proj_b[layer]))
        mix_out = jnp.einsum("bsd,de->bse", merged, w_out[layer])
        x = x + rms_norm(mix_out, mix_norm_post[layer])
        h = rms_norm(x, ffn_norm_pre[layer])
        up = jnp.einsum("bsd,df->bsf", h, w_up[layer])
        up = causal_dwconv(up, conv_f_w[layer], conv_f_b[layer])
        gate, val = jnp.split(up, 2, axis=-1)
        ffn_out = jnp.einsum("bsf,fd->bsd", jax.nn.gelu(gate) * val, w_down[layer])
        x = x + rms_norm(ffn_out, ffn_norm_post[layer])
    return x
```

```python
import functools
import math

import jax
import jax.numpy as jnp
from jax import lax
from jax.experimental import pallas as pl
from jax.experimental.pallas import tpu as pltpu

D_MODEL = 1024
D_RNN = 1280
H_RNN = 16
RNN_BLOCK = D_RNN // H_RNN
CONV_A = 4
LRU_C = 8.0
H_ATT = 16
HEAD_DIM = 64
D_ATT = H_ATT * HEAD_DIM
D_FF = 3 * D_MODEL
CONV_F = 3
RMS_EPS = 1e-6

LANES = 128
SUBLANES = 8
MXU_COLS = 256
VMEM_LIMIT = 56 * 1024 * 1024

F32 = jnp.float32
BF16 = jnp.bfloat16

_OFF_XA = 0
_OFF_GA = _OFF_XA + D_RNN
_OFF_Q = _OFF_GA + D_RNN
_OFF_K = _OFF_Q + D_ATT
_OFF_V = _OFF_K + D_ATT
_OFF_G = _OFF_V + D_ATT
_OFF_F = _OFF_G + 2 * D_MODEL
_W_CAT = _OFF_F + LANES

GATE_TILE = MXU_COLS
GATE_WIN = 2 * MXU_COLS
N_GATE_TILES = D_RNN // GATE_TILE


def _gate_window_start(j):
    lo = (j * GATE_TILE // RNN_BLOCK) * RNN_BLOCK
    start = min((lo // LANES) * LANES, D_RNN - GATE_WIN)
    hi = -(-((j + 1) * GATE_TILE) // RNN_BLOCK) * RNN_BLOCK
    assert start <= lo and hi <= start + GATE_WIN
    return start


GATE_STARTS = tuple(_gate_window_start(j) for j in range(N_GATE_TILES))


def _rms(x, gain):
    y = x * lax.rsqrt(jnp.mean(x * x, axis=-1, keepdims=True) + RMS_EPS)
    return y * gain


def _const_spec(shape):
    nd = len(shape)
    return pl.BlockSpec(shape, lambda *_: (0,) * nd, pipeline_mode=pl.Buffered(1))


def _in_proj_kernel(x_ref, g_ref, w_ref, xa_ref, ga_ref, q_ref, k_ref, v_ref, gab_ref, f_ref):
    h = _rms(x_ref[...], g_ref[...]).astype(BF16)

    def mm(lo, hi):
        return jnp.dot(h, w_ref[:, lo:hi], preferred_element_type=F32)

    xa_ref[...] = mm(_OFF_XA, _OFF_GA)
    ga_ref[...] = mm(_OFF_GA, _OFF_Q)
    q_ref[...] = (mm(_OFF_Q, _OFF_K) * (1.0 / math.sqrt(HEAD_DIM))).astype(BF16)
    k_ref[...] = mm(_OFF_K, _OFF_V).astype(BF16)
    v_ref[...] = mm(_OFF_V, _OFF_G).astype(BF16)
    gab_ref[...] = mm(_OFF_G, _OFF_F)
    f_ref[...] = mm(_OFF_F, _W_CAT)


def _in_proj(x2, gain, w_cat, tm):
    t = x2.shape[0]
    row = lambda n: pl.BlockSpec((tm, n), lambda i: (i, 0))
    return pl.pallas_call(
        _in_proj_kernel,
        grid=(t // tm,),
        in_specs=[row(D_MODEL), _const_spec((1, D_MODEL)), _const_spec((D_MODEL, _W_CAT))],
        out_specs=[row(D_RNN), row(D_RNN), row(D_ATT), row(D_ATT), row(D_ATT),
                   row(2 * D_MODEL), row(LANES)],
        out_shape=[jax.ShapeDtypeStruct((t, D_RNN), F32), jax.ShapeDtypeStruct((t, D_RNN), F32),
                   jax.ShapeDtypeStruct((t, D_ATT), BF16), jax.ShapeDtypeStruct((t, D_ATT), BF16),
                   jax.ShapeDtypeStruct((t, D_ATT), BF16),
                   jax.ShapeDtypeStruct((t, 2 * D_MODEL), F32),
                   jax.ShapeDtypeStruct((t, LANES), F32)],
        compiler_params=pltpu.CompilerParams(dimension_semantics=("arbitrary",),
                                             vmem_limit_bytes=VMEM_LIMIT),
        name="in_proj",
    )(x2, gain, w_cat)


def _rglru_kernel(xa_ref, ga_ref, cw_ref, cb_ref, wg_ref, ba_ref, bx_ref, lam_ref, wp_ref,
                  pa_ref, xpad, a_s, b_s, h_s, p_s, y_s, hprev, *, tc):
    seg = tc // SUBLANES
    t_idx = pl.program_id(1)

    @pl.when(t_idx == 0)
    def _():
        xpad[0:SUBLANES, :] = jnp.zeros((SUBLANES, D_RNN), F32)
        hprev[...] = jnp.zeros_like(hprev)

    xpad[SUBLANES:SUBLANES + tc, :] = xa_ref[0]
    xc = cb_ref[...] + cw_ref[CONV_A - 1:CONV_A, :] * xpad[SUBLANES:SUBLANES + tc, :]
    for kk in range(CONV_A - 1):
        sh = CONV_A - 1 - kk
        xc = xc + cw_ref[kk:kk + 1, :] * xpad[SUBLANES - sh:SUBLANES - sh + tc, :]
    xpad[0:SUBLANES, :] = xpad[tc:tc + SUBLANES, :]

    xcb = xc.astype(BF16)
    sp = jax.nn.softplus(-lam_ref[...])
    lane_chunks = GATE_TILE // LANES
    for j in range(N_GATE_TILES):
        ks = GATE_STARTS[j]
        cols = slice(j * GATE_TILE, (j + 1) * GATE_TILE)
        g = jnp.dot(xcb[:, ks:ks + GATE_WIN], wg_ref[j], preferred_element_type=F32)
        r = jax.nn.sigmoid(g[:, :GATE_TILE] + ba_ref[:, cols])
        i = jax.nn.sigmoid(g[:, GATE_TILE:] + bx_ref[:, cols])
        log_a = -LRU_C * r * sp[:, cols]
        a = jnp.exp(log_a)
        b = jnp.sqrt(-jnp.tanh(log_a) * (a * a + 1.0)) * (i * xc[:, cols])
        for c in range(lane_chunks):
            a_s[j * lane_chunks + c] = a[:, c * LANES:(c + 1) * LANES]
            b_s[j * lane_chunks + c] = b[:, c * LANES:(c + 1) * LANES]

    n_lane = D_RNN // LANES
    h8 = [jnp.zeros((SUBLANES, LANES), F32) for _ in range(n_lane)]
    p8 = [jnp.ones((SUBLANES, LANES), F32) for _ in range(n_lane)]
    for j in range(seg):
        rows = pl.ds(j, SUBLANES, stride=seg)
        for c in range(n_lane):
            a_j = a_s[c, rows, :]
            h8[c] = a_j * h8[c] + b_s[c, rows, :]
            p8[c] = p8[c] * a_j
            h_s[c, rows, :] = h8[c]
            p_s[c, rows, :] = p8[c]

    for c in range(n_lane):
        cols = slice(c * LANES, (c + 1) * LANES)
        start = hprev[:, cols]
        for s in range(SUBLANES):
            rows = slice(s * seg, (s + 1) * seg)
            hfin = h_s[c, rows, :] + p_s[c, rows, :] * start
            y_s[rows, cols] = (jax.nn.gelu(ga_ref[0, rows, cols]) * hfin).astype(BF16)
            start = h8[c][s:s + 1, :] + p8[c][s:s + 1, :] * start
        hprev[:, cols] = start

    pa_ref[0] = jnp.dot(y_s[...], wp_ref[...], preferred_element_type=F32)


def _rglru(xa3, ga3, cw, cb, wg, ba, bx, lam, wp, tc):
    b, s, _ = xa3.shape
    blk = lambda n: pl.BlockSpec((1, tc, n), lambda bi, ti: (bi, ti, 0))
    return pl.pallas_call(
        functools.partial(_rglru_kernel, tc=tc),
        grid=(b, s // tc),
        in_specs=[blk(D_RNN), blk(D_RNN), _const_spec((CONV_A, D_RNN)), _const_spec((1, D_RNN)),
                  _const_spec((N_GATE_TILES, GATE_WIN, 2 * GATE_TILE)),
                  _const_spec((1, D_RNN)), _const_spec((1, D_RNN)), _const_spec((1, D_RNN)),
                  _const_spec((D_RNN, D_MODEL))],
        out_specs=blk(D_MODEL),
        out_shape=jax.ShapeDtypeStruct((b, s, D_MODEL), F32),
        scratch_shapes=[pltpu.VMEM((tc + SUBLANES, D_RNN), F32),
                        pltpu.VMEM((D_RNN // LANES, tc, LANES), F32),
                        pltpu.VMEM((D_RNN // LANES, tc, LANES), F32),
                        pltpu.VMEM((D_RNN // LANES, tc, LANES), F32),
                        pltpu.VMEM((D_RNN // LANES, tc, LANES), F32),
                        pltpu.VMEM((tc, D_RNN), BF16), pltpu.VMEM((1, D_RNN), F32)],
        compiler_params=pltpu.CompilerParams(dimension_semantics=("arbitrary", "arbitrary"),
                                             vmem_limit_bytes=VMEM_LIMIT),
        name="rglru",
    )(xa3, ga3, cw, cb, wg, ba, bx, lam, wp)


def _fox_attn_kernel(q_ref, k_ref, v_ref, f_ref, bf_ref, o_ref, cum, cum_t, *, s, tq):
    pair = pl.program_id(1)

    @pl.when(pair == 0)
    def _():
        c = jax.nn.log_sigmoid(f_ref[0] + bf_ref[...])
        row = lax.broadcasted_iota(jnp.int32, (s, LANES), 0)
        d = 1
        while d < s:
            c = c + jnp.where(row >= d, pltpu.roll(c, d, axis=0), 0.0)
            d *= 2
        cum[...] = c
        cum_t[...] = c.T

    lane = lax.broadcasted_iota(jnp.int32, (1, LANES), 1)
    q = q_ref[0]
    outs = []
    for hh in range(2):
        head = 2 * pair + hh
        in_head = (lane >= hh * HEAD_DIM) & (lane < (hh + 1) * HEAD_DIM)
        qm = jnp.where(in_head, q, jnp.zeros_like(q))
        cq = jnp.sum(jnp.where(lane == head, cum[...], 0.0), axis=-1, keepdims=True)
        ck = cum_t[pl.ds(head, 1), :]
        blocks = []
        for i in range(s // tq):
            qs, qe = i * tq, (i + 1) * tq
            sc = lax.dot_general(qm[qs:qe], k_ref[0, 0:qe, :], (((1,), (1,)), ((), ())),
                                 preferred_element_type=F32)
            sc = sc + cq[qs:qe] - ck[:, 0:qe]
            rowi = lax.broadcasted_iota(jnp.int32, (tq, qe), 0) + qs
            coli = lax.broadcasted_iota(jnp.int32, (tq, qe), 1)
            sc = jnp.where(coli <= rowi, sc, -jnp.inf)
            m = jnp.max(sc, axis=-1, keepdims=True)
            p = jnp.exp(sc - m)
            l = jnp.sum(p, axis=-1, keepdims=True)
            o = jnp.dot(p.astype(BF16), v_ref[0, 0:qe, :], preferred_element_type=F32)
            blocks.append(o / l)
        outs.append(jnp.concatenate(blocks, axis=0))
    o_ref[0] = jnp.where(lane < HEAD_DIM, outs[0], outs[1]).astype(BF16)


def _fox_attn(q3, k3, v3, f3, bf, tq):
    b, s, _ = q3.shape
    blk = pl.BlockSpec((1, s, LANES), lambda bi, pi: (bi, 0, pi))
    return pl.pallas_call(
        functools.partial(_fox_attn_kernel, s=s, tq=tq),
        grid=(b, H_ATT // 2),
        in_specs=[blk, blk, blk, pl.BlockSpec((1, s, LANES), lambda bi, pi: (bi, 0, 0)),
                  _const_spec((1, LANES))],
        out_specs=blk,
        out_shape=jax.ShapeDtypeStruct((b, s, D_ATT), BF16),
        scratch_shapes=[pltpu.VMEM((s, LANES), F32), pltpu.VMEM((LANES, s), F32)],
        compiler_params=pltpu.CompilerParams(dimension_semantics=("arbitrary", "arbitrary"),
                                             vmem_limit_bytes=VMEM_LIMIT),
        name="fox_attn",
    )(q3, k3, v3, f3, bf)


FF_CHUNK = 512


def _merge_ffn_kernel(x_ref, pa_ref, yb_ref, gab_ref, bm_ref, wpb_ref, wout_ref, gpost_ref,
                      gpre2_ref, gpost2_ref, wup_ref, cw_ref, cb_ref, wdn_ref, o_ref, upad,
                      *, tm, tiles_per_seq):
    i = pl.program_id(0)

    @pl.when(i % tiles_per_seq == 0)
    def _():
        upad[0:SUBLANES, :] = jnp.zeros((SUBLANES, 2 * D_FF), F32)

    gates = jax.nn.sigmoid(gab_ref[...] + bm_ref[...])
    pb = jnp.dot(yb_ref[...], wpb_ref[...], preferred_element_type=F32)
    merged = gates[:, :D_MODEL] * pa_ref[...] + gates[:, D_MODEL:] * pb
    mix = jnp.dot(merged.astype(BF16), wout_ref[...], preferred_element_type=F32)
    x1 = x_ref[...] + _rms(mix, gpost_ref[...])

    h = _rms(x1, gpre2_ref[...]).astype(BF16)
    acc = jnp.zeros((tm, D_MODEL), F32)
    for c in range(D_FF // FF_CHUNK):
        halves = []
        for off in (0, D_FF):
            cols = slice(off + c * FF_CHUNK, off + (c + 1) * FF_CHUNK)
            upad[SUBLANES:SUBLANES + tm, cols] = jnp.dot(h, wup_ref[:, cols],
                                                         preferred_element_type=F32)
            u = cb_ref[:, cols] + cw_ref[CONV_F - 1:CONV_F, cols] * upad[SUBLANES:SUBLANES + tm, cols]
            for kk in range(CONV_F - 1):
                sh = CONV_F - 1 - kk
                u = u + cw_ref[kk:kk + 1, cols] * upad[SUBLANES - sh:SUBLANES - sh + tm, cols]
            upad[0:SUBLANES, cols] = upad[tm:tm + SUBLANES, cols]
            halves.append(u)
        act = (jax.nn.gelu(halves[0]) * halves[1]).astype(BF16)
        acc = acc + jnp.dot(act, wdn_ref[c * FF_CHUNK:(c + 1) * FF_CHUNK, :],
                            preferred_element_type=F32)
    o_ref[...] = x1 + _rms(acc, gpost2_ref[...])


def _merge_ffn(x2, pa2, yb2, gab2, bm, wpb, wout, gpost, gpre2, gpost2, wup, cw, cb, wdn,
               tm, seq):
    t = x2.shape[0]
    row = lambda n: pl.BlockSpec((tm, n), lambda i: (i, 0))
    return pl.pallas_call(
        functools.partial(_merge_ffn_kernel, tm=tm, tiles_per_seq=seq // tm),
        grid=(t // tm,),
        in_specs=[row(D_MODEL), row(D_MODEL), row(D_ATT), row(2 * D_MODEL),
                  _const_spec((1, 2 * D_MODEL)), _const_spec((D_ATT, D_MODEL)),
                  _const_spec((D_MODEL, D_MODEL)), _const_spec((1, D_MODEL)),
                  _const_spec((1, D_MODEL)), _const_spec((1, D_MODEL)),
                  _const_spec((D_MODEL, 2 * D_FF)), _const_spec((CONV_F, 2 * D_FF)),
                  _const_spec((1, 2 * D_FF)), _const_spec((D_FF, D_MODEL))],
        out_specs=row(D_MODEL),
        out_shape=jax.ShapeDtypeStruct((t, D_MODEL), F32),
        scratch_shapes=[pltpu.VMEM((tm + SUBLANES, 2 * D_FF), F32)],
        compiler_params=pltpu.CompilerParams(dimension_semantics=("arbitrary",),
                                             vmem_limit_bytes=VMEM_LIMIT),
        name="merge_ffn",
    )(x2, pa2, yb2, gab2, bm, wpb, wout, gpost, gpre2, gpost2, wup, cw, cb, wdn)


def _block_diag(w):
    h, n, _ = w.shape
    eye = jnp.eye(h, dtype=w.dtype)
    return jnp.einsum("hij,hg->higj", w, eye).reshape(h * n, h * n)


def _gate_windows(w_a, w_x):
    da, dx = _block_diag(w_a), _block_diag(w_x)
    tiles = []
    for j, ks in enumerate(GATE_STARTS):
        cols = slice(j * GATE_TILE, (j + 1) * GATE_TILE)
        tiles.append(jnp.concatenate([da[ks:ks + GATE_WIN, cols], dx[ks:ks + GATE_WIN, cols]],
                                     axis=1))
    return jnp.stack(tiles).astype(BF16)


def _layer(x, mix_norm_pre, mix_norm_post, w_in, conv_a_w, conv_a_b, w_rg_a, b_rg_a, w_rg_x,
           b_rg_x, lru_lambda, b_forget, b_merge, w_proj_a, w_proj_b, w_out, ffn_norm_pre,
           ffn_norm_post, w_up, conv_f_w, conv_f_b, w_down):
    bsz, seq, _ = x.shape
    t = bsz * seq
    row = lambda a: a.reshape(1, -1)

    o_f = 2 * D_RNN + 3 * D_ATT
    w_cat = jnp.concatenate(
        [w_in[:, :o_f], w_in[:, o_f + H_ATT:], w_in[:, o_f:o_f + H_ATT],
         jnp.zeros((D_MODEL, LANES - H_ATT), w_in.dtype)], axis=1).astype(BF16)
    bf = jnp.concatenate([b_forget, jnp.zeros((LANES - H_ATT,), b_forget.dtype)]).reshape(1, LANES)
    wg = _gate_windows(w_rg_a, w_rg_x)

    x2 = x.reshape(t, D_MODEL)
    xa, ga, q, k, v, gab, f = _in_proj(x2, row(mix_norm_pre), w_cat, tm=256)

    r3 = lambda a: a.reshape(bsz, seq, a.shape[-1])
    pa = _rglru(r3(xa), r3(ga), conv_a_w, row(conv_a_b), wg, row(b_rg_a), row(b_rg_x),
                row(lru_lambda), w_proj_a.astype(BF16), tc=256)
    yb = _fox_attn(r3(q), r3(k), r3(v), r3(f), bf, tq=256)

    out = _merge_ffn(x2, pa.reshape(t, D_MODEL), yb.reshape(t, D_ATT), gab, row(b_merge),
                     w_proj_b.astype(BF16), w_out.astype(BF16), row(mix_norm_post),
                     row(ffn_norm_pre), row(ffn_norm_post), w_up.astype(BF16), conv_f_w,
                     row(conv_f_b), w_down.astype(BF16), tm=256, seq=seq)
    return out.reshape(bsz, seq, D_MODEL)


def kernel(x, mix_norm_pre, mix_norm_post, w_in, conv_a_w, conv_a_b, w_rg_a, b_rg_a, w_rg_x, b_rg_x, lru_lambda, b_forget, b_merge, w_proj_a, w_proj_b, w_out, ffn_norm_pre, ffn_norm_post, w_up, conv_f_w, conv_f_b, w_down):
    depth = w_in.shape[0]
    for layer in range(depth):
        x = _layer(x, mix_norm_pre[layer], mix_norm_post[layer], w_in[layer], conv_a_w[layer],
                   conv_a_b[layer], w_rg_a[layer], b_rg_a[layer], w_rg_x[layer], b_rg_x[layer],
                   lru_lambda[layer], b_forget[layer], b_merge[layer], w_proj_a[layer],
                   w_proj_b[layer], w_out[layer], ffn_norm_pre[layer], ffn_norm_post[layer],
                   w_up[layer], conv_f_w[layer], conv_f_b[layer], w_down[layer])
    return x
```

```python
import functools
import math

import jax
import jax.numpy as jnp
from jax import lax
from jax.experimental import pallas as pl
from jax.experimental.pallas import tpu as pltpu

D_MODEL = 1024
D_RNN = 1280
H_RNN = 16
RNN_BLOCK = D_RNN // H_RNN
CONV_A = 4
LRU_C = 8.0
H_ATT = 16
HEAD_DIM = 64
D_ATT = H_ATT * HEAD_DIM
D_FF = 3 * D_MODEL
CONV_F = 3
RMS_EPS = 1e-6

LANES = 128
SUBLANES = 8
MXU_COLS = 256
VMEM_LIMIT = 56 * 1024 * 1024

LOG2E = math.log2(math.e)

F32 = jnp.float32
BF16 = jnp.bfloat16

_OFF_XA = 0
_OFF_GA = _OFF_XA + D_RNN
_OFF_Q = _OFF_GA + D_RNN
_OFF_K = _OFF_Q + D_ATT
_OFF_V = _OFF_K + D_ATT
_OFF_G = _OFF_V + D_ATT
_OFF_F = _OFF_G + 2 * D_MODEL
_W_CAT = _OFF_F + LANES

GATE_TILE = MXU_COLS
GATE_WIN = 2 * MXU_COLS
N_GATE_TILES = D_RNN // GATE_TILE


def _gate_window_start(j):
    lo = (j * GATE_TILE // RNN_BLOCK) * RNN_BLOCK
    start = min((lo // LANES) * LANES, D_RNN - GATE_WIN)
    hi = -(-((j + 1) * GATE_TILE) // RNN_BLOCK) * RNN_BLOCK
    assert start <= lo and hi <= start + GATE_WIN
    return start


GATE_STARTS = tuple(_gate_window_start(j) for j in range(N_GATE_TILES))


def _rms(x, gain):
    y = x * lax.rsqrt(jnp.mean(x * x, axis=-1, keepdims=True) + RMS_EPS)
    return y * gain


def _const_spec(shape):
    nd = len(shape)
    return pl.BlockSpec(shape, lambda *_: (0,) * nd, pipeline_mode=pl.Buffered(1))


def _in_proj_kernel(x_ref, g_ref, w_ref, xa_ref, ga_ref, q_ref, k_ref, v_ref, gab_ref, f_ref):
    h = _rms(x_ref[...], g_ref[...]).astype(BF16)

    def mm(lo, hi):
        return jnp.dot(h, w_ref[:, lo:hi], preferred_element_type=F32)

    xa_ref[...] = mm(_OFF_XA, _OFF_GA)
    ga_ref[...] = mm(_OFF_GA, _OFF_Q)
    q_ref[...] = (mm(_OFF_Q, _OFF_K) * (LOG2E / math.sqrt(HEAD_DIM))).astype(BF16)
    k_ref[...] = mm(_OFF_K, _OFF_V).astype(BF16)
    v_ref[...] = mm(_OFF_V, _OFF_G).astype(BF16)
    gab_ref[...] = mm(_OFF_G, _OFF_F)
    f_ref[...] = mm(_OFF_F, _W_CAT)


def _in_proj(x2, gain, w_cat, tm):
    t = x2.shape[0]
    row = lambda n: pl.BlockSpec((tm, n), lambda i: (i, 0))
    return pl.pallas_call(
        _in_proj_kernel,
        grid=(t // tm,),
        in_specs=[row(D_MODEL), _const_spec((1, D_MODEL)), _const_spec((D_MODEL, _W_CAT))],
        out_specs=[row(D_RNN), row(D_RNN), row(D_ATT), row(D_ATT), row(D_ATT),
                   row(2 * D_MODEL), row(LANES)],
        out_shape=[jax.ShapeDtypeStruct((t, D_RNN), F32), jax.ShapeDtypeStruct((t, D_RNN), F32),
                   jax.ShapeDtypeStruct((t, D_ATT), BF16), jax.ShapeDtypeStruct((t, D_ATT), BF16),
                   jax.ShapeDtypeStruct((t, D_ATT), BF16),
                   jax.ShapeDtypeStruct((t, 2 * D_MODEL), F32),
                   jax.ShapeDtypeStruct((t, LANES), F32)],
        compiler_params=pltpu.CompilerParams(dimension_semantics=("arbitrary",),
                                             vmem_limit_bytes=VMEM_LIMIT),
        name="in_proj",
    )(x2, gain, w_cat)


def _scan_pitch(tc):
    seg = tc // SUBLANES
    return seg + SUBLANES if (seg // SUBLANES) % 2 == 0 else seg


def _rglru_kernel(xa_ref, ga_ref, cw_ref, cb_ref, wg_ref, ba_ref, bx_ref, lam_ref, wp_ref,
                  pa_ref, xpad, a_s, b_s, h_s, p_s, y_s, hprev, *, tc):
    seg = tc // SUBLANES
    pitch = _scan_pitch(tc)
    t_idx = pl.program_id(1)

    @pl.when(t_idx == 0)
    def _():
        xpad[0:SUBLANES, :] = jnp.zeros((SUBLANES, D_RNN), F32)
        hprev[...] = jnp.zeros_like(hprev)

    xpad[SUBLANES:SUBLANES + tc, :] = xa_ref[0]
    xc = cb_ref[...] + cw_ref[CONV_A - 1:CONV_A, :] * xpad[SUBLANES:SUBLANES + tc, :]
    for kk in range(CONV_A - 1):
        sh = CONV_A - 1 - kk
        xc = xc + cw_ref[kk:kk + 1, :] * xpad[SUBLANES - sh:SUBLANES - sh + tc, :]
    xpad[0:SUBLANES, :] = xpad[tc:tc + SUBLANES, :]

    xcb = xc.astype(BF16)
    sp = jax.nn.softplus(-lam_ref[...])
    lane_chunks = GATE_TILE // LANES
    for j in range(N_GATE_TILES):
        ks = GATE_STARTS[j]
        cols = slice(j * GATE_TILE, (j + 1) * GATE_TILE)
        g = jnp.dot(xcb[:, ks:ks + GATE_WIN], wg_ref[j], preferred_element_type=F32)
        r = jax.nn.sigmoid(g[:, :GATE_TILE] + ba_ref[:, cols])
        i = jax.nn.sigmoid(g[:, GATE_TILE:] + bx_ref[:, cols])
        log_a = -LRU_C * r * sp[:, cols]
        a = jnp.exp(log_a)
        b = jnp.sqrt(-jnp.tanh(log_a) * (a * a + 1.0)) * (i * xc[:, cols])
        for c in range(lane_chunks):
            for s in range(SUBLANES):
                dst = slice(s * pitch, s * pitch + seg)
                a_s[j * lane_chunks + c, dst, :] = a[s * seg:(s + 1) * seg, c * LANES:(c + 1) * LANES]
                b_s[j * lane_chunks + c, dst, :] = b[s * seg:(s + 1) * seg, c * LANES:(c + 1) * LANES]

    n_lane = D_RNN // LANES
    h8 = [jnp.zeros((SUBLANES, LANES), F32) for _ in range(n_lane)]
    p8 = [jnp.ones((SUBLANES, LANES), F32) for _ in range(n_lane)]
    for j in range(seg):
        rows = pl.ds(j, SUBLANES, stride=pitch)
        for c in range(n_lane):
            a_j = a_s[c, rows, :]
            h8[c] = a_j * h8[c] + b_s[c, rows, :]
            p8[c] = p8[c] * a_j
            h_s[c, rows, :] = h8[c]
            p_s[c, rows, :] = p8[c]

    for c in range(n_lane):
        cols = slice(c * LANES, (c + 1) * LANES)
        start = hprev[:, cols]
        for s in range(SUBLANES):
            rows = slice(s * seg, (s + 1) * seg)
            src = slice(s * pitch, s * pitch + seg)
            hfin = h_s[c, src, :] + p_s[c, src, :] * start
            y_s[rows, cols] = (jax.nn.gelu(ga_ref[0, rows, cols]) * hfin).astype(BF16)
            start = h8[c][s:s + 1, :] + p8[c][s:s + 1, :] * start
        hprev[:, cols] = start

    pa_ref[0] = jnp.dot(y_s[...], wp_ref[...], preferred_element_type=F32)


def _rglru(xa3, ga3, cw, cb, wg, ba, bx, lam, wp, tc):
    b, s, _ = xa3.shape
    blk = lambda n: pl.BlockSpec((1, tc, n), lambda bi, ti: (bi, ti, 0))
    scan_shape = (D_RNN // LANES, SUBLANES * _scan_pitch(tc), LANES)
    return pl.pallas_call(
        functools.partial(_rglru_kernel, tc=tc),
        grid=(b, s // tc),
        in_specs=[blk(D_RNN), blk(D_RNN), _const_spec((CONV_A, D_RNN)), _const_spec((1, D_RNN)),
                  _const_spec((N_GATE_TILES, GATE_WIN, 2 * GATE_TILE)),
                  _const_spec((1, D_RNN)), _const_spec((1, D_RNN)), _const_spec((1, D_RNN)),
                  _const_spec((D_RNN, D_MODEL))],
        out_specs=blk(D_MODEL),
        out_shape=jax.ShapeDtypeStruct((b, s, D_MODEL), F32),
        scratch_shapes=[pltpu.VMEM((tc + SUBLANES, D_RNN), F32),
                        pltpu.VMEM(scan_shape, F32), pltpu.VMEM(scan_shape, F32),
                        pltpu.VMEM(scan_shape, F32), pltpu.VMEM(scan_shape, F32),
                        pltpu.VMEM((tc, D_RNN), BF16), pltpu.VMEM((1, D_RNN), F32)],
        compiler_params=pltpu.CompilerParams(dimension_semantics=("arbitrary", "arbitrary"),
                                             vmem_limit_bytes=VMEM_LIMIT),
        name="rglru",
    )(xa3, ga3, cw, cb, wg, ba, bx, lam, wp)


def _fox_attn_kernel(q_ref, k_ref, v_ref, f_ref, bf_ref, o_ref, cum, cum_t, qm_s, va_s, cq_s,
                     ck_s, sc_s, p_s, m_s, off_s, o0_s, *, s, tq):
    pair = pl.program_id(1)

    @pl.when(pair == 0)
    def _():
        c = jax.nn.log_sigmoid(f_ref[0] + bf_ref[...])
        row = lax.broadcasted_iota(jnp.int32, (s, LANES), 0)
        d = 1
        while d < s:
            c = c + jnp.where(row >= d, pltpu.roll(c, d, axis=0), 0.0)
            d *= 2
        c = c * LOG2E
        cum[...] = c
        cum_t[...] = c.T

    lane = lax.broadcasted_iota(jnp.int32, (1, LANES), 1)
    causal = (lax.broadcasted_iota(jnp.int32, (tq, tq), 1)
              <= lax.broadcasted_iota(jnp.int32, (tq, tq), 0))
    halves = tq // LANES
    n_q = s // tq

    q = q_ref[0]
    v = v_ref[0]
    for hh in range(2):
        head = 2 * pair + hh
        in_head = (lane >= hh * HEAD_DIM) & (lane < (hh + 1) * HEAD_DIM)
        qm_s[hh] = jnp.where(in_head, q, jnp.zeros_like(q))
        va_s[hh] = jnp.where(in_head, v, jnp.ones_like(v))
        cq_s[hh] = jnp.sum(jnp.where(lane == head, cum[...], 0.0), axis=-1, keepdims=True)
        ck_s[hh] = cum_t[pl.ds(head, 1), :]

    tasks = [(hh, i) for hh in range(2) for i in range(n_q)]

    def score_tile(n, j):
        hh, i = tasks[n]
        slot = n % 2
        ks = slice(j * tq, (j + 1) * tq)
        sc = lax.dot_general(qm_s[hh, i * tq:(i + 1) * tq, :], k_ref[0, ks, :],
                             (((1,), (1,)), ((), ())), preferred_element_type=F32)
        sc = sc - ck_s[hh, :, ks]
        if j == i:
            sc = jnp.where(causal, sc, -jnp.inf)
        sc_s[slot, :, ks] = sc
        part = sc[:, 0:LANES]
        for c in range(1, halves):
            part = jnp.maximum(part, sc[:, c * LANES:(c + 1) * LANES])
        m_s[slot] = part if j == 0 else jnp.maximum(m_s[slot], part)

    def finish_scores(n):
        hh, i = tasks[n]
        slot = n % 2
        m = jnp.max(m_s[slot], axis=-1, keepdims=True)
        cq_i = cq_s[hh, i * tq:(i + 1) * tq, :]
        off_s[slot] = jnp.broadcast_to((m + cq_i) - cq_i, (tq, LANES))

    def exp_tile(n, j):
        slot = n % 2
        for c in range(halves):
            cs = slice(j * tq + c * LANES, j * tq + (c + 1) * LANES)
            p_s[slot, :, cs] = jnp.exp2(sc_s[slot, :, cs] - off_s[slot]).astype(BF16)

    def weighted_values(n):
        hh, i = tasks[n]
        slot = n % 2
        qs, qe = i * tq, (i + 1) * tq
        o = jnp.dot(p_s[slot, :, 0:qe], va_s[hh, 0:qe, :], preferred_element_type=F32)
        sum_lane = (1 - hh) * HEAD_DIM
        o = o / o[:, sum_lane:sum_lane + 1]
        if hh == 0:
            o0_s[qs:qe, :] = o
        else:
            o_ref[0, qs:qe, :] = jnp.where(lane < HEAD_DIM, o0_s[qs:qe, :], o).astype(BF16)

    for j in range(tasks[0][1] + 1):
        score_tile(0, j)
    finish_scores(0)
    for n in range(len(tasks)):
        n_exp = tasks[n][1] + 1
        n_score = tasks[n + 1][1] + 1 if n + 1 < len(tasks) else 0
        for j in range(max(n_exp, n_score)):
            if j < n_score:
                score_tile(n + 1, j)
            if j < n_exp:
                exp_tile(n, j)
        if n_score:
            finish_scores(n + 1)
        weighted_values(n)


def _fox_attn(q3, k3, v3, f3, bf, tq):
    b, s, _ = q3.shape
    blk = pl.BlockSpec((1, s, LANES), lambda bi, pi: (bi, 0, pi))
    return pl.pallas_call(
        functools.partial(_fox_attn_kernel, s=s, tq=tq),
        grid=(b, H_ATT // 2),
        in_specs=[blk, blk, blk, pl.BlockSpec((1, s, LANES), lambda bi, pi: (bi, 0, 0)),
                  _const_spec((1, LANES))],
        out_specs=blk,
        out_shape=jax.ShapeDtypeStruct((b, s, D_ATT), BF16),
        scratch_shapes=[pltpu.VMEM((s, LANES), F32), pltpu.VMEM((LANES, s), F32),
                        pltpu.VMEM((2, s, LANES), BF16), pltpu.VMEM((2, s, LANES), BF16),
                        pltpu.VMEM((2, s, 1), F32), pltpu.VMEM((2, 1, s), F32),
                        pltpu.VMEM((2, tq, s), F32), pltpu.VMEM((2, tq, s), BF16),
                        pltpu.VMEM((2, tq, LANES), F32), pltpu.VMEM((2, tq, LANES), F32),
                        pltpu.VMEM((s, LANES), F32)],
        compiler_params=pltpu.CompilerParams(dimension_semantics=("arbitrary", "arbitrary"),
                                             vmem_limit_bytes=VMEM_LIMIT),
        name="fox_attn",
    )(q3, k3, v3, f3, bf)


FF_CHUNK = 512


def _merge_ffn_kernel(x_ref, pa_ref, yb_ref, gab_ref, bm_ref, wpb_ref, wout_ref, gpost_ref,
                      gpre2_ref, gpost2_ref, wup_ref, cw_ref, cb_ref, wdn_ref, o_ref, upad,
                      *, tm, tiles_per_seq):
    i = pl.program_id(0)

    @pl.when(i % tiles_per_seq == 0)
    def _():
        upad[0:SUBLANES, :] = jnp.zeros((SUBLANES, 2 * D_FF), F32)

    gates = jax.nn.sigmoid(gab_ref[...] + bm_ref[...])
    pb = jnp.dot(yb_ref[...], wpb_ref[...], preferred_element_type=F32)
    merged = gates[:, :D_MODEL] * pa_ref[...] + gates[:, D_MODEL:] * pb
    mix = jnp.dot(merged.astype(BF16), wout_ref[...], preferred_element_type=F32)
    x1 = x_ref[...] + _rms(mix, gpost_ref[...])

    h = _rms(x1, gpre2_ref[...]).astype(BF16)
    acc = jnp.zeros((tm, D_MODEL), F32)
    for c in range(D_FF // FF_CHUNK):
        halves = []
        for off in (0, D_FF):
            cols = slice(off + c * FF_CHUNK, off + (c + 1) * FF_CHUNK)
            upad[SUBLANES:SUBLANES + tm, cols] = jnp.dot(h, wup_ref[:, cols],
                                                         preferred_element_type=F32)
            u = cb_ref[:, cols] + cw_ref[CONV_F - 1:CONV_F, cols] * upad[SUBLANES:SUBLANES + tm, cols]
            for kk in range(CONV_F - 1):
                sh = CONV_F - 1 - kk
                u = u + cw_ref[kk:kk + 1, cols] * upad[SUBLANES - sh:SUBLANES - sh + tm, cols]
            upad[0:SUBLANES, cols] = upad[tm:tm + SUBLANES, cols]
            halves.append(u)
        act = (jax.nn.gelu(halves[0]) * halves[1]).astype(BF16)
        acc = acc + jnp.dot(act, wdn_ref[c * FF_CHUNK:(c + 1) * FF_CHUNK, :],
                            preferred_element_type=F32)
    o_ref[...] = x1 + _rms(acc, gpost2_ref[...])


def _merge_ffn(x2, pa2, yb2, gab2, bm, wpb, wout, gpost, gpre2, gpost2, wup, cw, cb, wdn,
               tm, seq):
    t = x2.shape[0]
    row = lambda n: pl.BlockSpec((tm, n), lambda i: (i, 0))
    return pl.pallas_call(
        functools.partial(_merge_ffn_kernel, tm=tm, tiles_per_seq=seq // tm),
        grid=(t // tm,),
        in_specs=[row(D_MODEL), row(D_MODEL), row(D_ATT), row(2 * D_MODEL),
                  _const_spec((1, 2 * D_MODEL)), _const_spec((D_ATT, D_MODEL)),
                  _const_spec((D_MODEL, D_MODEL)), _const_spec((1, D_MODEL)),
                  _const_spec((1, D_MODEL)), _const_spec((1, D_MODEL)),
                  _const_spec((D_MODEL, 2 * D_FF)), _const_spec((CONV_F, 2 * D_FF)),
                  _const_spec((1, 2 * D_FF)), _const_spec((D_FF, D_MODEL))],
        out_specs=row(D_MODEL),
        out_shape=jax.ShapeDtypeStruct((t, D_MODEL), F32),
        scratch_shapes=[pltpu.VMEM((tm + SUBLANES, 2 * D_FF), F32)],
        compiler_params=pltpu.CompilerParams(dimension_semantics=("arbitrary",),
                                             vmem_limit_bytes=VMEM_LIMIT),
        name="merge_ffn",
    )(x2, pa2, yb2, gab2, bm, wpb, wout, gpost, gpre2, gpost2, wup, cw, cb, wdn)


def _block_diag(w):
    h, n, _ = w.shape
    eye = jnp.eye(h, dtype=w.dtype)
    return jnp.einsum("hij,hg->higj", w, eye).reshape(h * n, h * n)


def _gate_windows(w_a, w_x):
    da, dx = _block_diag(w_a), _block_diag(w_x)
    tiles = []
    for j, ks in enumerate(GATE_STARTS):
        cols = slice(j * GATE_TILE, (j + 1) * GATE_TILE)
        tiles.append(jnp.concatenate([da[ks:ks + GATE_WIN, cols], dx[ks:ks + GATE_WIN, cols]],
                                     axis=1))
    return jnp.stack(tiles).astype(BF16)


def _layer(x, mix_norm_pre, mix_norm_post, w_in, conv_a_w, conv_a_b, w_rg_a, b_rg_a, w_rg_x,
           b_rg_x, lru_lambda, b_forget, b_merge, w_proj_a, w_proj_b, w_out, ffn_norm_pre,
           ffn_norm_post, w_up, conv_f_w, conv_f_b, w_down):
    bsz, seq, _ = x.shape
    t = bsz * seq
    row = lambda a: a.reshape(1, -1)

    o_f = 2 * D_RNN + 3 * D_ATT
    w_cat = jnp.concatenate(
        [w_in[:, :o_f], w_in[:, o_f + H_ATT:], w_in[:, o_f:o_f + H_ATT],
         jnp.zeros((D_MODEL, LANES - H_ATT), w_in.dtype)], axis=1).astype(BF16)
    bf = jnp.concatenate([b_forget, jnp.zeros((LANES - H_ATT,), b_forget.dtype)]).reshape(1, LANES)
    wg = _gate_windows(w_rg_a, w_rg_x)

    x2 = x.reshape(t, D_MODEL)
    xa, ga, q, k, v, gab, f = _in_proj(x2, row(mix_norm_pre), w_cat, tm=256)

    r3 = lambda a: a.reshape(bsz, seq, a.shape[-1])
    pa = _rglru(r3(xa), r3(ga), conv_a_w, row(conv_a_b), wg, row(b_rg_a), row(b_rg_x),
                row(lru_lambda), w_proj_a.astype(BF16), tc=256)
    yb = _fox_attn(r3(q), r3(k), r3(v), r3(f), bf, tq=256)

    out = _merge_ffn(x2, pa.reshape(t, D_MODEL), yb.reshape(t, D_ATT), gab, row(b_merge),
                     w_proj_b.astype(BF16), w_out.astype(BF16), row(mix_norm_post),
                     row(ffn_norm_pre), row(ffn_norm_post), w_up.astype(BF16), conv_f_w,
                     row(conv_f_b), w_down.astype(BF16), tm=256, seq=seq)
    return out.reshape(bsz, seq, D_MODEL)


def kernel(x, mix_norm_pre, mix_norm_post, w_in, conv_a_w, conv_a_b, w_rg_a, b_rg_a, w_rg_x, b_rg_x, lru_lambda, b_forget, b_merge, w_proj_a, w_proj_b, w_out, ffn_norm_pre, ffn_norm_post, w_up, conv_f_w, conv_f_b, w_down):
    depth = w_in.shape[0]
    for layer in range(depth):
        x = _layer(x, mix_norm_pre[layer], mix_norm_post[layer], w_in[layer], conv_a_w[layer],
                   conv_a_b[layer], w_rg_a[layer], b_rg_a[layer], w_rg_x[layer], b_rg_x[layer],
                   lru_lambda[layer], b_forget[layer], b_merge[layer], w_proj_a[layer],
                   w_proj_b[layer], w_out[layer], ffn_norm_pre[layer], ffn_norm_post[layer],
                   w_up[layer], conv_f_w[layer], conv_f_b[layer], w_down[layer])
    return x
```

```python
import functools
import math

import jax
import jax.numpy as jnp
from jax import lax
from jax.experimental import pallas as pl
from jax.experimental.pallas import tpu as pltpu

D_MODEL = 1024
D_RNN = 1280
H_RNN = 16
RNN_BLOCK = D_RNN // H_RNN
CONV_A = 4
LRU_C = 8.0
H_ATT = 16
HEAD_DIM = 64
D_ATT = H_ATT * HEAD_DIM
D_FF = 3 * D_MODEL
CONV_F = 3
RMS_EPS = 1e-6

LANES = 128
SUBLANES = 8
MXU_COLS = 256
VMEM_LIMIT = 56 * 1024 * 1024

LOG2E = math.log2(math.e)

F32 = jnp.float32
BF16 = jnp.bfloat16

_OFF_XA = 0
_OFF_GA = _OFF_XA + D_RNN
_OFF_Q = _OFF_GA + D_RNN
_OFF_K = _OFF_Q + D_ATT
_OFF_V = _OFF_K + D_ATT
_OFF_G = _OFF_V + D_ATT
_OFF_F = _OFF_G + 2 * D_MODEL
_W_CAT = _OFF_F + LANES

GATE_TILE = MXU_COLS
GATE_WIN = 2 * MXU_COLS
N_GATE_TILES = D_RNN // GATE_TILE


def _gate_window_start(j):
    lo = (j * GATE_TILE // RNN_BLOCK) * RNN_BLOCK
    start = min((lo // LANES) * LANES, D_RNN - GATE_WIN)
    hi = -(-((j + 1) * GATE_TILE) // RNN_BLOCK) * RNN_BLOCK
    assert start <= lo and hi <= start + GATE_WIN
    return start


GATE_STARTS = tuple(_gate_window_start(j) for j in range(N_GATE_TILES))


def _rms(x, gain):
    y = x * lax.rsqrt(jnp.mean(x * x, axis=-1, keepdims=True) + RMS_EPS)
    return y * gain


def _padded(n):
    return n + LANES if (n // 32) % 8 == 0 else n


def _bf16_weight(w):
    k, n = w.shape
    return jnp.pad(w.astype(BF16), ((0, 0), (0, _padded(n) - n)))


def _const_spec(shape):
    nd = len(shape)
    return pl.BlockSpec(shape, lambda *_: (0,) * nd, pipeline_mode=pl.Buffered(1))


def _in_proj_kernel(x_ref, g_ref, w_ref, xa_ref, ga_ref, q_ref, k_ref, v_ref, gab_ref, f_ref):
    h = _rms(x_ref[...], g_ref[...]).astype(BF16)

    def mm(lo, hi):
        return jnp.dot(h, w_ref[:, lo:hi], preferred_element_type=F32)

    xa_ref[...] = mm(_OFF_XA, _OFF_GA)
    ga_ref[...] = mm(_OFF_GA, _OFF_Q)
    q_ref[...] = (mm(_OFF_Q, _OFF_K) * (LOG2E / math.sqrt(HEAD_DIM))).astype(BF16)
    k_ref[...] = mm(_OFF_K, _OFF_V).astype(BF16)
    v_ref[...] = mm(_OFF_V, _OFF_G).astype(BF16)
    gab_ref[...] = mm(_OFF_G, _OFF_F)
    f_ref[...] = mm(_OFF_F, _W_CAT)


def _in_proj(x2, gain, w_cat, tm):
    t = x2.shape[0]
    row = lambda n: pl.BlockSpec((tm, n), lambda i: (i, 0))
    return pl.pallas_call(
        _in_proj_kernel,
        grid=(t // tm,),
        in_specs=[row(D_MODEL), _const_spec((1, D_MODEL)), _const_spec((D_MODEL, _W_CAT))],
        out_specs=[row(D_RNN), row(D_RNN), row(D_ATT), row(D_ATT), row(D_ATT),
                   row(2 * D_MODEL), row(LANES)],
        out_shape=[jax.ShapeDtypeStruct((t, D_RNN), F32), jax.ShapeDtypeStruct((t, D_RNN), F32),
                   jax.ShapeDtypeStruct((t, D_ATT), BF16), jax.ShapeDtypeStruct((t, D_ATT), BF16),
                   jax.ShapeDtypeStruct((t, D_ATT), BF16),
                   jax.ShapeDtypeStruct((t, 2 * D_MODEL), F32),
                   jax.ShapeDtypeStruct((t, LANES), F32)],
        compiler_params=pltpu.CompilerParams(dimension_semantics=("arbitrary",),
                                             vmem_limit_bytes=VMEM_LIMIT),
        name="in_proj",
    )(x2, gain, w_cat)


def _scan_pitch(tc):
    seg = tc // SUBLANES
    return seg + SUBLANES if (seg // SUBLANES) % 2 == 0 else seg


def _rglru_kernel(xa_ref, ga_ref, cw_ref, cb_ref, wg_ref, ba_ref, bx_ref, lam_ref, wp_ref,
                  pa_ref, xpad, a_s, b_s, h_s, p_s, y_s, hprev, *, tc):
    seg = tc // SUBLANES
    pitch = _scan_pitch(tc)
    t_idx = pl.program_id(1)

    @pl.when(t_idx == 0)
    def _():
        xpad[0:SUBLANES, :] = jnp.zeros((SUBLANES, D_RNN), F32)
        hprev[...] = jnp.zeros_like(hprev)

    xpad[SUBLANES:SUBLANES + tc, :] = xa_ref[0]
    xc = cb_ref[...] + cw_ref[CONV_A - 1:CONV_A, :] * xpad[SUBLANES:SUBLANES + tc, :]
    for kk in range(CONV_A - 1):
        sh = CONV_A - 1 - kk
        xc = xc + cw_ref[kk:kk + 1, :] * xpad[SUBLANES - sh:SUBLANES - sh + tc, :]
    xpad[0:SUBLANES, :] = xpad[tc:tc + SUBLANES, :]

    xcb = xc.astype(BF16)
    sp = jax.nn.softplus(-lam_ref[...])
    lane_chunks = GATE_TILE // LANES
    for j in range(N_GATE_TILES):
        ks = GATE_STARTS[j]
        cols = slice(j * GATE_TILE, (j + 1) * GATE_TILE)
        g = jnp.dot(xcb[:, ks:ks + GATE_WIN], wg_ref[j], preferred_element_type=F32)
        r = jax.nn.sigmoid(g[:, :GATE_TILE] + ba_ref[:, cols])
        i = jax.nn.sigmoid(g[:, GATE_TILE:] + bx_ref[:, cols])
        log_a = -LRU_C * r * sp[:, cols]
        a = jnp.exp(log_a)
        b = jnp.sqrt(-jnp.tanh(log_a) * (a * a + 1.0)) * (i * xc[:, cols])
        for c in range(lane_chunks):
            for s in range(SUBLANES):
                dst = slice(s * pitch, s * pitch + seg)
                a_s[j * lane_chunks + c, dst, :] = a[s * seg:(s + 1) * seg, c * LANES:(c + 1) * LANES]
                b_s[j * lane_chunks + c, dst, :] = b[s * seg:(s + 1) * seg, c * LANES:(c + 1) * LANES]

    n_lane = D_RNN // LANES
    h8 = [jnp.zeros((SUBLANES, LANES), F32) for _ in range(n_lane)]
    p8 = [jnp.ones((SUBLANES, LANES), F32) for _ in range(n_lane)]
    for j in range(seg):
        rows = pl.ds(j, SUBLANES, stride=pitch)
        for c in range(n_lane):
            a_j = a_s[c, rows, :]
            h8[c] = a_j * h8[c] + b_s[c, rows, :]
            p8[c] = p8[c] * a_j
            h_s[c, rows, :] = h8[c]
            p_s[c, rows, :] = p8[c]

    for c in range(n_lane):
        cols = slice(c * LANES, (c + 1) * LANES)
        start = hprev[:, cols]
        for s in range(SUBLANES):
            rows = slice(s * seg, (s + 1) * seg)
            src = slice(s * pitch, s * pitch + seg)
            hfin = h_s[c, src, :] + p_s[c, src, :] * start
            y_s[rows, cols] = (jax.nn.gelu(ga_ref[0, rows, cols]) * hfin).astype(BF16)
            start = h8[c][s:s + 1, :] + p8[c][s:s + 1, :] * start
        hprev[:, cols] = start

    pa_ref[0] = jnp.dot(y_s[...], wp_ref[...], preferred_element_type=F32)


def _rglru(xa3, ga3, cw, cb, wg, ba, bx, lam, wp, tc):
    b, s, _ = xa3.shape
    blk = lambda n: pl.BlockSpec((1, tc, n), lambda bi, ti: (bi, ti, 0))
    scan_shape = (D_RNN // LANES, SUBLANES * _scan_pitch(tc), LANES)
    return pl.pallas_call(
        functools.partial(_rglru_kernel, tc=tc),
        grid=(b, s // tc),
        in_specs=[blk(D_RNN), blk(D_RNN), _const_spec((CONV_A, D_RNN)), _const_spec((1, D_RNN)),
                  _const_spec((N_GATE_TILES, GATE_WIN, 2 * GATE_TILE)),
                  _const_spec((1, D_RNN)), _const_spec((1, D_RNN)), _const_spec((1, D_RNN)),
                  _const_spec((D_RNN, D_MODEL))],
        out_specs=blk(D_MODEL),
        out_shape=jax.ShapeDtypeStruct((b, s, D_MODEL), F32),
        scratch_shapes=[pltpu.VMEM((tc + SUBLANES, D_RNN), F32),
                        pltpu.VMEM(scan_shape, F32), pltpu.VMEM(scan_shape, F32),
                        pltpu.VMEM(scan_shape, F32), pltpu.VMEM(scan_shape, F32),
                        pltpu.VMEM((tc, D_RNN), BF16), pltpu.VMEM((1, D_RNN), F32)],
        compiler_params=pltpu.CompilerParams(dimension_semantics=("arbitrary", "arbitrary"),
                                             vmem_limit_bytes=VMEM_LIMIT),
        name="rglru",
    )(xa3, ga3, cw, cb, wg, ba, bx, lam, wp)


def _fox_attn_kernel(q_ref, k_ref, v_ref, f_ref, bf_ref, o_ref, cum, cum_t, qm_s, va_s, cq_s,
                     ck_s, sc_s, p_s, m_s, off_s, o0_s, *, s, tq):
    pair = pl.program_id(1)

    @pl.when(pair == 0)
    def _():
        c = jax.nn.log_sigmoid(f_ref[0] + bf_ref[...])
        row = lax.broadcasted_iota(jnp.int32, (s, LANES), 0)
        d = 1
        while d < s:
            c = c + jnp.where(row >= d, pltpu.roll(c, d, axis=0), 0.0)
            d *= 2
        c = c * LOG2E
        cum[...] = c
        cum_t[...] = c.T

    lane = lax.broadcasted_iota(jnp.int32, (1, LANES), 1)
    causal = (lax.broadcasted_iota(jnp.int32, (tq, tq), 1)
              <= lax.broadcasted_iota(jnp.int32, (tq, tq), 0))
    halves = tq // LANES
    n_q = s // tq

    q = q_ref[0]
    v = v_ref[0]
    for hh in range(2):
        head = 2 * pair + hh
        in_head = (lane >= hh * HEAD_DIM) & (lane < (hh + 1) * HEAD_DIM)
        qm_s[hh] = jnp.where(in_head, q, jnp.zeros_like(q))
        va_s[hh] = jnp.where(in_head, v, jnp.ones_like(v))
        cq_s[hh] = jnp.sum(jnp.where(lane == head, cum[...], 0.0), axis=-1, keepdims=True)
        ck_s[hh] = cum_t[pl.ds(head, 1), :]

    tasks = [(hh, i) for hh in range(2) for i in range(n_q)]

    def score_tile(n, j):
        hh, i = tasks[n]
        slot = n % 2
        ks = slice(j * tq, (j + 1) * tq)
        sc = lax.dot_general(qm_s[hh, i * tq:(i + 1) * tq, :], k_ref[0, ks, :],
                             (((1,), (1,)), ((), ())), preferred_element_type=F32)
        sc = sc - ck_s[hh, :, ks]
        if j == i:
            sc = jnp.where(causal, sc, -jnp.inf)
        sc_s[slot, :, ks] = sc
        part = sc[:, 0:LANES]
        for c in range(1, halves):
            part = jnp.maximum(part, sc[:, c * LANES:(c + 1) * LANES])
        m_s[slot] = part if j == 0 else jnp.maximum(m_s[slot], part)

    def finish_scores(n):
        hh, i = tasks[n]
        slot = n % 2
        m = jnp.max(m_s[slot], axis=-1, keepdims=True)
        cq_i = cq_s[hh, i * tq:(i + 1) * tq, :]
        off_s[slot] = jnp.broadcast_to((m + cq_i) - cq_i, (tq, LANES))

    def exp_tile(n, j):
        slot = n % 2
        for c in range(halves):
            cs = slice(j * tq + c * LANES, j * tq + (c + 1) * LANES)
            p_s[slot, :, cs] = jnp.exp2(sc_s[slot, :, cs] - off_s[slot]).astype(BF16)

    def weighted_values(n):
        hh, i = tasks[n]
        slot = n % 2
        qs, qe = i * tq, (i + 1) * tq
        o = jnp.dot(p_s[slot, :, 0:qe], va_s[hh, 0:qe, :], preferred_element_type=F32)
        sum_lane = (1 - hh) * HEAD_DIM
        o = o / o[:, sum_lane:sum_lane + 1]
        if hh == 0:
            o0_s[qs:qe, :] = o
        else:
            o_ref[0, qs:qe, :] = jnp.where(lane < HEAD_DIM, o0_s[qs:qe, :], o).astype(BF16)

    for j in range(tasks[0][1] + 1):
        score_tile(0, j)
    finish_scores(0)
    for n in range(len(tasks)):
        n_exp = tasks[n][1] + 1
        n_score = tasks[n + 1][1] + 1 if n + 1 < len(tasks) else 0
        for j in range(max(n_exp, n_score)):
            if j < n_score:
                score_tile(n + 1, j)
            if j < n_exp:
                exp_tile(n, j)
        if n_score:
            finish_scores(n + 1)
        weighted_values(n)


def _fox_attn(q3, k3, v3, f3, bf, tq):
    b, s, _ = q3.shape
    blk = pl.BlockSpec((1, s, LANES), lambda bi, pi: (bi, 0, pi))
    return pl.pallas_call(
        functools.partial(_fox_attn_kernel, s=s, tq=tq),
        grid=(b, H_ATT // 2),
        in_specs=[blk, blk, blk, pl.BlockSpec((1, s, LANES), lambda bi, pi: (bi, 0, 0)),
                  _const_spec((1, LANES))],
        out_specs=blk,
        out_shape=jax.ShapeDtypeStruct((b, s, D_ATT), BF16),
        scratch_shapes=[pltpu.VMEM((s, LANES), F32), pltpu.VMEM((LANES, s), F32),
                        pltpu.VMEM((2, s, LANES), BF16), pltpu.VMEM((2, s, LANES), BF16),
                        pltpu.VMEM((2, s, 1), F32), pltpu.VMEM((2, 1, s), F32),
                        pltpu.VMEM((2, tq, s), F32), pltpu.VMEM((2, tq, s), BF16),
                        pltpu.VMEM((2, tq, LANES), F32), pltpu.VMEM((2, tq, LANES), F32),
                        pltpu.VMEM((s, LANES), F32)],
        compiler_params=pltpu.CompilerParams(dimension_semantics=("arbitrary", "arbitrary"),
                                             vmem_limit_bytes=VMEM_LIMIT),
        name="fox_attn",
    )(q3, k3, v3, f3, bf)


FF_CHUNK = 512


def _merge_ffn_kernel(x_ref, pa_ref, yb_ref, gab_ref, bm_ref, wpb_ref, wout_ref, gpost_ref,
                      gpre2_ref, gpost2_ref, perm_ref, wup_ref, cw_ref, cb_ref, wdn_ref, o_ref,
                      tail, unperm, *, tm, tiles_per_seq):
    i = pl.program_id(0)
    seg = tm // SUBLANES
    pitch = _scan_pitch(tm)

    @pl.when(i % tiles_per_seq == 0)
    def _():
        tail[...] = jnp.zeros_like(tail)

    gates = jax.nn.sigmoid(gab_ref[...] + bm_ref[...])
    pb = jnp.dot(yb_ref[...], wpb_ref[:, :D_MODEL], preferred_element_type=F32)
    merged = gates[:, :D_MODEL] * pa_ref[...] + gates[:, D_MODEL:] * pb
    mix = jnp.dot(merged.astype(BF16), wout_ref[:, :D_MODEL], preferred_element_type=F32)
    x1 = x_ref[...] + _rms(mix, gpost_ref[...])

    h = _rms(x1, gpre2_ref[...]).astype(BF16)
    hp = jnp.dot(perm_ref[...], h, preferred_element_type=F32).astype(BF16)
    first = lax.broadcasted_iota(jnp.int32, (SUBLANES, FF_CHUNK), 0) == 0
    acc = jnp.zeros((tm, D_MODEL), F32)
    for c in range(D_FF // FF_CHUNK):
        halves = []
        for off in (0, D_FF):
            cols = slice(off + c * FF_CHUNK, off + (c + 1) * FF_CHUNK)
            u = jnp.dot(hp, wup_ref[:, cols], preferred_element_type=F32)
            prev = [jnp.where(first, pltpu.roll(tail[g * SUBLANES:(g + 1) * SUBLANES, cols], 1, axis=0),
                              pltpu.roll(u[tm - (2 - g) * SUBLANES:tm - (1 - g) * SUBLANES, :], 1, axis=0))
                    for g in range(CONV_F - 1)]
            tail[:, cols] = u[tm - (CONV_F - 1) * SUBLANES:, :]
            u1 = jnp.concatenate([prev[1], u[:tm - SUBLANES, :]], axis=0)
            u2 = jnp.concatenate([prev[0], prev[1], u[:tm - 2 * SUBLANES, :]], axis=0)
            halves.append(cb_ref[:, cols] + cw_ref[2:3, cols] * u + cw_ref[1:2, cols] * u1
                          + cw_ref[0:1, cols] * u2)
        act = (jax.nn.gelu(halves[0]) * halves[1]).astype(BF16)
        acc = acc + jnp.dot(act, wdn_ref[c * FF_CHUNK:(c + 1) * FF_CHUNK, :D_MODEL],
                            preferred_element_type=F32)

    for j in range(seg):
        for c in range(D_MODEL // LANES):
            unperm[c, pl.ds(j, SUBLANES, stride=pitch), :] = (
                acc[j * SUBLANES:(j + 1) * SUBLANES, c * LANES:(c + 1) * LANES])
    ffn = jnp.concatenate(
        [jnp.concatenate([unperm[c, s * pitch:s * pitch + seg, :] for s in range(SUBLANES)], axis=0)
         for c in range(D_MODEL // LANES)], axis=1)
    o_ref[...] = x1 + _rms(ffn, gpost2_ref[...])


def _merge_ffn(x2, pa2, yb2, gab2, bm, wpb, wout, gpost, gpre2, gpost2, wup, cw, cb, wdn,
               tm, seq):
    t = x2.shape[0]
    row = lambda n: pl.BlockSpec((tm, n), lambda i: (i, 0))
    r = jnp.arange(tm)
    perm = (r[None, :] == ((r % SUBLANES) * (tm // SUBLANES) + r // SUBLANES)[:, None]).astype(BF16)
    return pl.pallas_call(
        functools.partial(_merge_ffn_kernel, tm=tm, tiles_per_seq=seq // tm),
        grid=(t // tm,),
        in_specs=[row(D_MODEL), row(D_MODEL), row(D_ATT), row(2 * D_MODEL),
                  _const_spec((1, 2 * D_MODEL)), _const_spec((D_ATT, _padded(D_MODEL))),
                  _const_spec((D_MODEL, _padded(D_MODEL))), _const_spec((1, D_MODEL)),
                  _const_spec((1, D_MODEL)), _const_spec((1, D_MODEL)), _const_spec((tm, tm)),
                  _const_spec((D_MODEL, _padded(2 * D_FF))), _const_spec((CONV_F, 2 * D_FF)),
                  _const_spec((1, 2 * D_FF)), _const_spec((D_FF, _padded(D_MODEL)))],
        out_specs=row(D_MODEL),
        out_shape=jax.ShapeDtypeStruct((t, D_MODEL), F32),
        scratch_shapes=[pltpu.VMEM(((CONV_F - 1) * SUBLANES, 2 * D_FF), F32),
                        pltpu.VMEM((D_MODEL // LANES, SUBLANES * _scan_pitch(tm), LANES), F32)],
        compiler_params=pltpu.CompilerParams(dimension_semantics=("arbitrary",),
                                             vmem_limit_bytes=VMEM_LIMIT),
        name="merge_ffn",
    )(x2, pa2, yb2, gab2, bm, wpb, wout, gpost, gpre2, gpost2, perm, wup, cw, cb, wdn)


def _block_diag(w):
    h, n, _ = w.shape
    eye = jnp.eye(h, dtype=w.dtype)
    return jnp.einsum("hij,hg->higj", w, eye).reshape(h * n, h * n)


def _gate_windows(w_a, w_x):
    da, dx = _block_diag(w_a), _block_diag(w_x)
    tiles = []
    for j, ks in enumerate(GATE_STARTS):
        cols = slice(j * GATE_TILE, (j + 1) * GATE_TILE)
        tiles.append(jnp.concatenate([da[ks:ks + GATE_WIN, cols], dx[ks:ks + GATE_WIN, cols]],
                                     axis=1))
    return jnp.stack(tiles).astype(BF16)


def _layer(x, mix_norm_pre, mix_norm_post, w_in, conv_a_w, conv_a_b, w_rg_a, b_rg_a, w_rg_x,
           b_rg_x, lru_lambda, b_forget, b_merge, w_proj_a, w_proj_b, w_out, ffn_norm_pre,
           ffn_norm_post, w_up, conv_f_w, conv_f_b, w_down):
    bsz, seq, _ = x.shape
    t = bsz * seq
    row = lambda a: a.reshape(1, -1)

    o_f = 2 * D_RNN + 3 * D_ATT
    w_cat = jnp.concatenate(
        [w_in[:, :o_f], w_in[:, o_f + H_ATT:], w_in[:, o_f:o_f + H_ATT],
         jnp.zeros((D_MODEL, LANES - H_ATT), w_in.dtype)], axis=1).astype(BF16)
    bf = jnp.concatenate([b_forget, jnp.zeros((LANES - H_ATT,), b_forget.dtype)]).reshape(1, LANES)
    wg = _gate_windows(w_rg_a, w_rg_x)

    x2 = x.reshape(t, D_MODEL)
    xa, ga, q, k, v, gab, f = _in_proj(x2, row(mix_norm_pre), w_cat, tm=256)

    r3 = lambda a: a.reshape(bsz, seq, a.shape[-1])
    pa = _rglru(r3(xa), r3(ga), conv_a_w, row(conv_a_b), wg, row(b_rg_a), row(b_rg_x),
                row(lru_lambda), w_proj_a.astype(BF16), tc=256)
    yb = _fox_attn(r3(q), r3(k), r3(v), r3(f), bf, tq=256)

    out = _merge_ffn(x2, pa.reshape(t, D_MODEL), yb.reshape(t, D_ATT), gab, row(b_merge),
                     _bf16_weight(w_proj_b), _bf16_weight(w_out), row(mix_norm_post),
                     row(ffn_norm_pre), row(ffn_norm_post), _bf16_weight(w_up), conv_f_w,
                     row(conv_f_b), _bf16_weight(w_down), tm=256, seq=seq)
    return out.reshape(bsz, seq, D_MODEL)


def kernel(x, mix_norm_pre, mix_norm_post, w_in, conv_a_w, conv_a_b, w_rg_a, b_rg_a, w_rg_x, b_rg_x, lru_lambda, b_forget, b_merge, w_proj_a, w_proj_b, w_out, ffn_norm_pre, ffn_norm_post, w_up, conv_f_w, conv_f_b, w_down):
    depth = w_in.shape[0]
    for layer in range(depth):
        x = _layer(x, mix_norm_pre[layer], mix_norm_post[layer], w_in[layer], conv_a_w[layer],
                   conv_a_b[layer], w_rg_a[layer], b_rg_a[layer], w_rg_x[layer], b_rg_x[layer],
                   lru_lambda[layer], b_forget[layer], b_merge[layer], w_proj_a[layer],
                   w_proj_b[layer], w_out[layer], ffn_norm_pre[layer], ffn_norm_post[layer],
                   w_up[layer], conv_f_w[layer], conv_f_b[layer], w_down[layer])
    return x
```

```python
import functools
import math

import jax
import jax.numpy as jnp
from jax import lax
from jax.experimental import pallas as pl
from jax.experimental.pallas import tpu as pltpu

D_MODEL = 1024
D_RNN = 1280
H_RNN = 16
RNN_BLOCK = D_RNN // H_RNN
CONV_A = 4
LRU_C = 8.0
H_ATT = 16
HEAD_DIM = 64
D_ATT = H_ATT * HEAD_DIM
D_FF = 3 * D_MODEL
CONV_F = 3
RMS_EPS = 1e-6

LANES = 128
SUBLANES = 8
MXU_COLS = 256
VMEM_LIMIT = 56 * 1024 * 1024

LOG2E = math.log2(math.e)

F32 = jnp.float32
BF16 = jnp.bfloat16

_OFF_XA = 0
_OFF_GA = _OFF_XA + D_RNN
_OFF_Q = _OFF_GA + D_RNN
_OFF_K = _OFF_Q + D_ATT
_OFF_V = _OFF_K + D_ATT
_OFF_F = _OFF_V + D_ATT
_OFF_G = _OFF_F + H_ATT

GATE_TILE = MXU_COLS
GATE_WIN = 2 * MXU_COLS
N_GATE_TILES = D_RNN // GATE_TILE


def _gate_window_start(j):
    lo = (j * GATE_TILE // RNN_BLOCK) * RNN_BLOCK
    start = min((lo // LANES) * LANES, D_RNN - GATE_WIN)
    hi = -(-((j + 1) * GATE_TILE) // RNN_BLOCK) * RNN_BLOCK
    assert start <= lo and hi <= start + GATE_WIN
    return start


GATE_STARTS = tuple(_gate_window_start(j) for j in range(N_GATE_TILES))


def _rms(x, gain):
    y = x * lax.rsqrt(jnp.mean(x * x, axis=-1, keepdims=True) + RMS_EPS)
    return y * gain


def _padded(n):
    return n + LANES if (n // 32) % 8 == 0 else n


def _bf16_weight(w):
    k, n = w.shape
    return jnp.pad(w.astype(BF16), ((0, 0), (0, _padded(n) - n)))


def _const_spec(shape):
    nd = len(shape)
    return pl.BlockSpec(shape, lambda *_: (0,) * nd, pipeline_mode=pl.Buffered(1))


def _in_proj_kernel(x_ref, g_ref, w_ref, wg_ref, wf_ref, xa_ref, ga_ref, q_ref, k_ref, v_ref,
                    gab_ref, f_ref):
    h = _rms(x_ref[...], g_ref[...]).astype(BF16)

    def mm(lo, hi):
        return jnp.dot(h, w_ref[:, lo:hi], preferred_element_type=F32)

    xa_ref[...] = mm(_OFF_XA, _OFF_GA)
    ga_ref[...] = mm(_OFF_GA, _OFF_Q)
    q_ref[...] = (mm(_OFF_Q, _OFF_K) * (LOG2E / math.sqrt(HEAD_DIM))).astype(BF16)
    k_ref[...] = mm(_OFF_K, _OFF_V).astype(BF16)
    v_ref[...] = mm(_OFF_V, _OFF_F).astype(BF16)
    gab_ref[...] = jnp.dot(h, wg_ref[:, :2 * D_MODEL], preferred_element_type=F32)
    f_ref[...] = jnp.dot(h, wf_ref[...], preferred_element_type=F32)


def _in_proj(x2, gain, w_main, w_gate, w_f, tm):
    t = x2.shape[0]
    row = lambda n: pl.BlockSpec((tm, n), lambda i: (i, 0))
    return pl.pallas_call(
        _in_proj_kernel,
        grid=(t // tm,),
        in_specs=[row(D_MODEL), _const_spec((1, D_MODEL)), _const_spec(w_main.shape),
                  _const_spec(w_gate.shape), _const_spec(w_f.shape)],
        out_specs=[row(D_RNN), row(D_RNN), row(D_ATT), row(D_ATT), row(D_ATT),
                   row(2 * D_MODEL), row(LANES)],
        out_shape=[jax.ShapeDtypeStruct((t, D_RNN), F32), jax.ShapeDtypeStruct((t, D_RNN), F32),
                   jax.ShapeDtypeStruct((t, D_ATT), BF16), jax.ShapeDtypeStruct((t, D_ATT), BF16),
                   jax.ShapeDtypeStruct((t, D_ATT), BF16),
                   jax.ShapeDtypeStruct((t, 2 * D_MODEL), F32),
                   jax.ShapeDtypeStruct((t, LANES), F32)],
        compiler_params=pltpu.CompilerParams(dimension_semantics=("arbitrary",),
                                             vmem_limit_bytes=VMEM_LIMIT),
        name="in_proj",
    )(x2, gain, w_main, w_gate, w_f)


def _scan_pitch(tc):
    seg = tc // SUBLANES
    return seg + SUBLANES if (seg // SUBLANES) % 2 == 0 else seg


def _rglru_kernel(xa_ref, ga_ref, cw_ref, cb_ref, wg_ref, ba_ref, bx_ref, lam_ref, wp_ref,
                  pa_ref, xpad, a_s, b_s, h_s, p_s, y_s, hprev, *, tc):
    seg = tc // SUBLANES
    pitch = _scan_pitch(tc)
    t_idx = pl.program_id(1)

    @pl.when(t_idx == 0)
    def _():
        xpad[0:SUBLANES, :] = jnp.zeros((SUBLANES, D_RNN), F32)
        hprev[...] = jnp.zeros_like(hprev)

    xpad[SUBLANES:SUBLANES + tc, :] = xa_ref[0]
    xc = cb_ref[...] + cw_ref[CONV_A - 1:CONV_A, :] * xpad[SUBLANES:SUBLANES + tc, :]
    for kk in range(CONV_A - 1):
        sh = CONV_A - 1 - kk
        xc = xc + cw_ref[kk:kk + 1, :] * xpad[SUBLANES - sh:SUBLANES - sh + tc, :]
    xpad[0:SUBLANES, :] = xpad[tc:tc + SUBLANES, :]

    xcb = xc.astype(BF16)
    sp = jax.nn.softplus(-lam_ref[...])
    lane_chunks = GATE_TILE // LANES
    for j in range(N_GATE_TILES):
        ks = GATE_STARTS[j]
        cols = slice(j * GATE_TILE, (j + 1) * GATE_TILE)
        g = jnp.dot(xcb[:, ks:ks + GATE_WIN], wg_ref[j, :, :2 * GATE_TILE],
                    preferred_element_type=F32)
        r = jax.nn.sigmoid(g[:, :GATE_TILE] + ba_ref[:, cols])
        i = jax.nn.sigmoid(g[:, GATE_TILE:] + bx_ref[:, cols])
        log_a = -LRU_C * r * sp[:, cols]
        a = jnp.exp(log_a)
        b = jnp.sqrt(-jnp.tanh(log_a) * (a * a + 1.0)) * (i * xc[:, cols])
        for c in range(lane_chunks):
            for s in range(SUBLANES):
                dst = slice(s * pitch, s * pitch + seg)
                a_s[j * lane_chunks + c, dst, :] = a[s * seg:(s + 1) * seg, c * LANES:(c + 1) * LANES]
                b_s[j * lane_chunks + c, dst, :] = b[s * seg:(s + 1) * seg, c * LANES:(c + 1) * LANES]

    n_lane = D_RNN // LANES
    h8 = [jnp.zeros((SUBLANES, LANES), F32) for _ in range(n_lane)]
    p8 = [jnp.ones((SUBLANES, LANES), F32) for _ in range(n_lane)]
    for j in range(seg):
        rows = pl.ds(j, SUBLANES, stride=pitch)
        for c in range(n_lane):
            a_j = a_s[c, rows, :]
            h8[c] = a_j * h8[c] + b_s[c, rows, :]
            p8[c] = p8[c] * a_j
            h_s[c, rows, :] = h8[c]
            p_s[c, rows, :] = p8[c]

    for c in range(n_lane):
        cols = slice(c * LANES, (c + 1) * LANES)
        start = hprev[:, cols]
        for s in range(SUBLANES):
            rows = slice(s * seg, (s + 1) * seg)
            src = slice(s * pitch, s * pitch + seg)
            hfin = h_s[c, src, :] + p_s[c, src, :] * start
            y_s[rows, cols] = (jax.nn.gelu(ga_ref[0, rows, cols]) * hfin).astype(BF16)
            start = h8[c][s:s + 1, :] + p8[c][s:s + 1, :] * start
        hprev[:, cols] = start

    pa_ref[0] = jnp.dot(y_s[...], wp_ref[:, :D_MODEL], preferred_element_type=F32)


def _rglru(xa3, ga3, cw, cb, wg, ba, bx, lam, wp, tc):
    b, s, _ = xa3.shape
    blk = lambda n: pl.BlockSpec((1, tc, n), lambda bi, ti: (bi, ti, 0))
    scan_shape = (D_RNN // LANES, SUBLANES * _scan_pitch(tc), LANES)
    return pl.pallas_call(
        functools.partial(_rglru_kernel, tc=tc),
        grid=(b, s // tc),
        in_specs=[blk(D_RNN), blk(D_RNN), _const_spec((CONV_A, D_RNN)), _const_spec((1, D_RNN)),
                  _const_spec((N_GATE_TILES, GATE_WIN, _padded(2 * GATE_TILE))),
                  _const_spec((1, D_RNN)), _const_spec((1, D_RNN)), _const_spec((1, D_RNN)),
                  _const_spec((D_RNN, _padded(D_MODEL)))],
        out_specs=blk(D_MODEL),
        out_shape=jax.ShapeDtypeStruct((b, s, D_MODEL), F32),
        scratch_shapes=[pltpu.VMEM((tc + SUBLANES, D_RNN), F32),
                        pltpu.VMEM(scan_shape, F32), pltpu.VMEM(scan_shape, F32),
                        pltpu.VMEM(scan_shape, F32), pltpu.VMEM(scan_shape, F32),
                        pltpu.VMEM((tc, D_RNN), BF16), pltpu.VMEM((1, D_RNN), F32)],
        compiler_params=pltpu.CompilerParams(dimension_semantics=("arbitrary", "arbitrary"),
                                             vmem_limit_bytes=VMEM_LIMIT),
        name="rglru",
    )(xa3, ga3, cw, cb, wg, ba, bx, lam, wp)


def _fox_attn_kernel(q_ref, k_ref, v_ref, f_ref, bf_ref, o_ref, cum, cum_t, qm_s, va_s, cq_s,
                     ck_s, sc_s, p_s, m_s, off_s, o0_s, *, s, tq):
    pair = pl.program_id(1)

    @pl.when(pair == 0)
    def _():
        c = jax.nn.log_sigmoid(f_ref[0] + bf_ref[...])
        row = lax.broadcasted_iota(jnp.int32, (s, LANES), 0)
        d = 1
        while d < s:
            c = c + jnp.where(row >= d, pltpu.roll(c, d, axis=0), 0.0)
            d *= 2
        c = c * LOG2E
        cum[...] = c
        cum_t[...] = c.T

    lane = lax.broadcasted_iota(jnp.int32, (1, LANES), 1)
    causal = (lax.broadcasted_iota(jnp.int32, (tq, tq), 1)
              <= lax.broadcasted_iota(jnp.int32, (tq, tq), 0))
    halves = tq // LANES
    n_q = s // tq

    q = q_ref[0]
    v = v_ref[0]
    for hh in range(2):
        head = 2 * pair + hh
        in_head = (lane >= hh * HEAD_DIM) & (lane < (hh + 1) * HEAD_DIM)
        qm_s[hh] = jnp.where(in_head, q, jnp.zeros_like(q))
        va_s[hh] = jnp.where(in_head, v, jnp.ones_like(v))
        cq_s[hh] = jnp.sum(jnp.where(lane == head, cum[...], 0.0), axis=-1, keepdims=True)
        ck_s[hh] = cum_t[pl.ds(head, 1), :]

    tasks = [(hh, i) for hh in range(2) for i in range(n_q)]

    def score_tile(n, j):
        hh, i = tasks[n]
        slot = n % 2
        ks = slice(j * tq, (j + 1) * tq)
        sc = lax.dot_general(qm_s[hh, i * tq:(i + 1) * tq, :], k_ref[0, ks, :],
                             (((1,), (1,)), ((), ())), preferred_element_type=F32)
        sc = sc - ck_s[hh, :, ks]
        if j == i:
            sc = jnp.where(causal, sc, -jnp.inf)
        sc_s[slot, :, ks] = sc
        part = sc[:, 0:LANES]
        for c in range(1, halves):
            part = jnp.maximum(part, sc[:, c * LANES:(c + 1) * LANES])
        m_s[slot] = part if j == 0 else jnp.maximum(m_s[slot], part)

    def finish_scores(n):
        hh, i = tasks[n]
        slot = n % 2
        m = jnp.max(m_s[slot], axis=-1, keepdims=True)
        cq_i = cq_s[hh, i * tq:(i + 1) * tq, :]
        off_s[slot] = jnp.broadcast_to((m + cq_i) - cq_i, (tq, LANES))

    def exp_tile(n, j):
        slot = n % 2
        for c in range(halves):
            cs = slice(j * tq + c * LANES, j * tq + (c + 1) * LANES)
            p_s[slot, :, cs] = jnp.exp2(sc_s[slot, :, cs] - off_s[slot]).astype(BF16)

    def weighted_values(n):
        hh, i = tasks[n]
        slot = n % 2
        qs, qe = i * tq, (i + 1) * tq
        o = jnp.dot(p_s[slot, :, 0:qe], va_s[hh, 0:qe, :], preferred_element_type=F32)
        sum_lane = (1 - hh) * HEAD_DIM
        o = o / o[:, sum_lane:sum_lane + 1]
        if hh == 0:
            o0_s[qs:qe, :] = o
        else:
            o_ref[0, qs:qe, :] = jnp.where(lane < HEAD_DIM, o0_s[qs:qe, :], o).astype(BF16)

    for j in range(tasks[0][1] + 1):
        score_tile(0, j)
    finish_scores(0)
    for n in range(len(tasks)):
        n_exp = tasks[n][1] + 1
        n_score = tasks[n + 1][1] + 1 if n + 1 < len(tasks) else 0
        for j in range(max(n_exp, n_score)):
            if j < n_score:
                score_tile(n + 1, j)
            if j < n_exp:
                exp_tile(n, j)
        if n_score:
            finish_scores(n + 1)
        weighted_values(n)


def _fox_attn(q3, k3, v3, f3, bf, tq):
    b, s, _ = q3.shape
    blk = pl.BlockSpec((1, s, LANES), lambda bi, pi: (bi, 0, pi))
    return pl.pallas_call(
        functools.partial(_fox_attn_kernel, s=s, tq=tq),
        grid=(b, H_ATT // 2),
        in_specs=[blk, blk, blk, pl.BlockSpec((1, s, LANES), lambda bi, pi: (bi, 0, 0)),
                  _const_spec((1, LANES))],
        out_specs=blk,
        out_shape=jax.ShapeDtypeStruct((b, s, D_ATT), BF16),
        scratch_shapes=[pltpu.VMEM((s, LANES), F32), pltpu.VMEM((LANES, s), F32),
                        pltpu.VMEM((2, s, LANES), BF16), pltpu.VMEM((2, s, LANES), BF16),
                        pltpu.VMEM((2, s, 1), F32), pltpu.VMEM((2, 1, s), F32),
                        pltpu.VMEM((2, tq, s + LANES), F32), pltpu.VMEM((2, tq, s + LANES), BF16),
                        pltpu.VMEM((2, tq, LANES), F32), pltpu.VMEM((2, tq, LANES), F32),
                        pltpu.VMEM((s, LANES), F32)],
        compiler_params=pltpu.CompilerParams(dimension_semantics=("arbitrary", "arbitrary"),
                                             vmem_limit_bytes=VMEM_LIMIT),
        name="fox_attn",
    )(q3, k3, v3, f3, bf)


FF_CHUNK = 512


def _merge_ffn_kernel(x_ref, pa_ref, yb_ref, gab_ref, bm_ref, wpb_ref, wout_ref, gpost_ref,
                      gpre2_ref, gpost2_ref, perm_ref, wup_ref, cw_ref, cb_ref, wdn_ref, o_ref,
                      tail, unperm, x1_s, hp_s, *, tm, tiles_per_seq):
    i = pl.program_id(0)
    seg = tm // SUBLANES
    pitch = _scan_pitch(tm)

    @pl.when(i == 0)
    def _():
        x1_s[...] = jnp.zeros_like(x1_s)
        hp_s[...] = jnp.zeros_like(hp_s)
        tail[...] = jnp.zeros_like(tail)

    @pl.when((i + tiles_per_seq - 1) % tiles_per_seq == 0)
    def _():
        tail[...] = jnp.zeros_like(tail)

    x1_prev = x1_s[...]
    hp = hp_s[...]

    def merge_stage(k, st):
        if k == 0:
            st["gates"] = jax.nn.sigmoid(gab_ref[...] + bm_ref[...])
        elif k == 1:
            st["pb"] = jnp.dot(yb_ref[...], wpb_ref[:, :D_MODEL], preferred_element_type=F32)
        elif k == 2:
            g = st.pop("gates")
            st["merged"] = (g[:, :D_MODEL] * pa_ref[...] + g[:, D_MODEL:] * st.pop("pb")).astype(BF16)
        elif k == 3:
            st["mix"] = jnp.dot(st.pop("merged"), wout_ref[:, :D_MODEL], preferred_element_type=F32)
        elif k == 4:
            x1_s[...] = x_ref[...] + _rms(st.pop("mix"), gpost_ref[...])
        elif k == 5:
            h = _rms(x1_s[...], gpre2_ref[...]).astype(BF16)
            hp_s[...] = jnp.dot(perm_ref[...], h, preferred_element_type=F32).astype(BF16)

    first = lax.broadcasted_iota(jnp.int32, (SUBLANES, FF_CHUNK), 0) == 0

    def up_proj(c):
        return [jnp.dot(hp, wup_ref[:, off + c * FF_CHUNK:off + (c + 1) * FF_CHUNK],
                        preferred_element_type=F32) for off in (0, D_FF)]

    def conv(u, cols):
        prev = [jnp.where(first, pltpu.roll(tail[g * SUBLANES:(g + 1) * SUBLANES, cols], 1, axis=0),
                          pltpu.roll(u[tm - (2 - g) * SUBLANES:tm - (1 - g) * SUBLANES, :], 1, axis=0))
                for g in range(CONV_F - 1)]
        tail[:, cols] = u[tm - (CONV_F - 1) * SUBLANES:, :]
        u1 = jnp.concatenate([prev[1], u[:tm - SUBLANES, :]], axis=0)
        u2 = jnp.concatenate([prev[0], prev[1], u[:tm - 2 * SUBLANES, :]], axis=0)
        return (cb_ref[:, cols] + cw_ref[2:3, cols] * u + cw_ref[1:2, cols] * u1
                + cw_ref[0:1, cols] * u2)

    n_chunks = D_FF // FF_CHUNK
    stages = {}
    acc = jnp.zeros((tm, D_MODEL), F32)
    u_next = up_proj(0)
    for c in range(n_chunks):
        u_gate, u_val = u_next
        if c + 1 < n_chunks:
            u_next = up_proj(c + 1)
        merge_stage(c, stages)
        gate = conv(u_gate, slice(c * FF_CHUNK, (c + 1) * FF_CHUNK))
        val = conv(u_val, slice(D_FF + c * FF_CHUNK, D_FF + (c + 1) * FF_CHUNK))
        act = (jax.nn.gelu(gate) * val).astype(BF16)
        acc = acc + jnp.dot(act, wdn_ref[c * FF_CHUNK:(c + 1) * FF_CHUNK, :D_MODEL],
                            preferred_element_type=F32)

    for j in range(seg):
        for c in range(D_MODEL // LANES):
            unperm[c, pl.ds(j, SUBLANES, stride=pitch), :] = (
                acc[j * SUBLANES:(j + 1) * SUBLANES, c * LANES:(c + 1) * LANES])
    ffn = jnp.concatenate(
        [jnp.concatenate([unperm[c, s * pitch:s * pitch + seg, :] for s in range(SUBLANES)], axis=0)
         for c in range(D_MODEL // LANES)], axis=1)
    o_ref[...] = x1_prev + _rms(ffn, gpost2_ref[...])


def _merge_ffn(x2, pa2, yb2, gab2, bm, wpb, wout, gpost, gpre2, gpost2, wup, cw, cb, wdn,
               tm, seq):
    t = x2.shape[0]
    n_tiles = t // tm
    row = lambda n: pl.BlockSpec((tm, n), lambda i: (jnp.minimum(i, n_tiles - 1), 0))
    r = jnp.arange(tm)
    perm = (r[None, :] == ((r % SUBLANES) * (tm // SUBLANES) + r // SUBLANES)[:, None]).astype(BF16)
    return pl.pallas_call(
        functools.partial(_merge_ffn_kernel, tm=tm, tiles_per_seq=seq // tm),
        grid=(n_tiles + 1,),
        in_specs=[row(D_MODEL), row(D_MODEL), row(D_ATT), row(2 * D_MODEL),
                  _const_spec((1, 2 * D_MODEL)), _const_spec((D_ATT, _padded(D_MODEL))),
                  _const_spec((D_MODEL, _padded(D_MODEL))), _const_spec((1, D_MODEL)),
                  _const_spec((1, D_MODEL)), _const_spec((1, D_MODEL)), _const_spec((tm, tm)),
                  _const_spec((D_MODEL, _padded(2 * D_FF))), _const_spec((CONV_F, 2 * D_FF)),
                  _const_spec((1, 2 * D_FF)), _const_spec((D_FF, _padded(D_MODEL)))],
        out_specs=pl.BlockSpec((tm, D_MODEL), lambda i: (jnp.maximum(i - 1, 0), 0)),
        out_shape=jax.ShapeDtypeStruct((t, D_MODEL), F32),
        scratch_shapes=[pltpu.VMEM(((CONV_F - 1) * SUBLANES, 2 * D_FF), F32),
                        pltpu.VMEM((D_MODEL // LANES, SUBLANES * _scan_pitch(tm), LANES), F32),
                        pltpu.VMEM((tm, D_MODEL), F32), pltpu.VMEM((tm, D_MODEL), BF16)],
        compiler_params=pltpu.CompilerParams(dimension_semantics=("arbitrary",),
                                             vmem_limit_bytes=VMEM_LIMIT),
        name="merge_ffn",
    )(x2, pa2, yb2, gab2, bm, wpb, wout, gpost, gpre2, gpost2, perm, wup, cw, cb, wdn)


def _block_diag(w):
    h, n, _ = w.shape
    eye = jnp.eye(h, dtype=w.dtype)
    return jnp.einsum("hij,hg->higj", w, eye).reshape(h * n, h * n)


def _gate_windows(w_a, w_x):
    da, dx = _block_diag(w_a), _block_diag(w_x)
    tiles = []
    for j, ks in enumerate(GATE_STARTS):
        cols = slice(j * GATE_TILE, (j + 1) * GATE_TILE)
        tiles.append(jnp.concatenate([da[ks:ks + GATE_WIN, cols], dx[ks:ks + GATE_WIN, cols]],
                                     axis=1))
    return jnp.stack([_bf16_weight(t) for t in tiles])


def _layer(x, mix_norm_pre, mix_norm_post, w_in, conv_a_w, conv_a_b, w_rg_a, b_rg_a, w_rg_x,
           b_rg_x, lru_lambda, b_forget, b_merge, w_proj_a, w_proj_b, w_out, ffn_norm_pre,
           ffn_norm_post, w_up, conv_f_w, conv_f_b, w_down):
    bsz, seq, _ = x.shape
    t = bsz * seq
    row = lambda a: a.reshape(1, -1)

    w_main = _bf16_weight(w_in[:, :_OFF_F])
    w_gate = _bf16_weight(w_in[:, _OFF_G:])
    w_f = jnp.pad(w_in[:, _OFF_F:_OFF_G].astype(BF16), ((0, 0), (0, LANES - H_ATT)))
    bf = jnp.concatenate([b_forget, jnp.zeros((LANES - H_ATT,), b_forget.dtype)]).reshape(1, LANES)
    wg = _gate_windows(w_rg_a, w_rg_x)

    x2 = x.reshape(t, D_MODEL)
    xa, ga, q, k, v, gab, f = _in_proj(x2, row(mix_norm_pre), w_main, w_gate, w_f, tm=256)

    r3 = lambda a: a.reshape(bsz, seq, a.shape[-1])
    pa = _rglru(r3(xa), r3(ga), conv_a_w, row(conv_a_b), wg, row(b_rg_a), row(b_rg_x),
                row(lru_lambda), _bf16_weight(w_proj_a), tc=256)
    yb = _fox_attn(r3(q), r3(k), r3(v), r3(f), bf, tq=256)

    out = _merge_ffn(x2, pa.reshape(t, D_MODEL), yb.reshape(t, D_ATT), gab, row(b_merge),
                     _bf16_weight(w_proj_b), _bf16_weight(w_out), row(mix_norm_post),
                     row(ffn_norm_pre), row(ffn_norm_post), _bf16_weight(w_up), conv_f_w,
                     row(conv_f_b), _bf16_weight(w_down), tm=256, seq=seq)
    return out.reshape(bsz, seq, D_MODEL)


def kernel(x, mix_norm_pre, mix_norm_post, w_in, conv_a_w, conv_a_b, w_rg_a, b_rg_a, w_rg_x, b_rg_x, lru_lambda, b_forget, b_merge, w_proj_a, w_proj_b, w_out, ffn_norm_pre, ffn_norm_post, w_up, conv_f_w, conv_f_b, w_down):
    depth = w_in.shape[0]
    for layer in range(depth):
        x = _layer(x, mix_norm_pre[layer], mix_norm_post[layer], w_in[layer], conv_a_w[layer],
                   conv_a_b[layer], w_rg_a[layer], b_rg_a[layer], w_rg_x[layer], b_rg_x[layer],
                   lru_lambda[layer], b_forget[layer], b_merge[layer], w_proj_a[layer],
                   w_proj_b[layer], w_out[layer], ffn_norm_pre[layer], ffn_norm_post[layer],
                   w_up[layer], conv_f_w[layer], conv_f_b[layer], w_down[layer])
    return x
```

```python
import functools
import math

import jax
import jax.numpy as jnp
from jax import lax
from jax.experimental import pallas as pl
from jax.experimental.pallas import tpu as pltpu

D_MODEL = 1024
D_RNN = 1280
H_RNN = 16
RNN_BLOCK = D_RNN // H_RNN
CONV_A = 4
LRU_C = 8.0
H_ATT = 16
HEAD_DIM = 64
D_ATT = H_ATT * HEAD_DIM
D_FF = 3 * D_MODEL
CONV_F = 3
RMS_EPS = 1e-6

LANES = 128
SUBLANES = 8
MXU_COLS = 256
VMEM_LIMIT = 56 * 1024 * 1024

IN_PROJ_ROWS = 512
RGLRU_ROWS = 512
ATTN_Q_ROWS = 256
FFN_ROWS = 256

LOG2E = math.log2(math.e)

F32 = jnp.float32
BF16 = jnp.bfloat16

_OFF_XA = 0
_OFF_GA = _OFF_XA + D_RNN
_OFF_Q = _OFF_GA + D_RNN
_OFF_K = _OFF_Q + D_ATT
_OFF_V = _OFF_K + D_ATT
_OFF_F = _OFF_V + D_ATT
_OFF_G = _OFF_F + H_ATT

GATE_TILE = MXU_COLS
GATE_WIN = 2 * MXU_COLS
N_GATE_TILES = D_RNN // GATE_TILE


def _gate_window_start(j):
    lo = (j * GATE_TILE // RNN_BLOCK) * RNN_BLOCK
    start = min((lo // LANES) * LANES, D_RNN - GATE_WIN)
    hi = -(-((j + 1) * GATE_TILE) // RNN_BLOCK) * RNN_BLOCK
    assert start <= lo and hi <= start + GATE_WIN
    return start


GATE_STARTS = tuple(_gate_window_start(j) for j in range(N_GATE_TILES))


def _rms(x, gain):
    y = x * lax.rsqrt(jnp.mean(x * x, axis=-1, keepdims=True) + RMS_EPS)
    return y * gain


def _padded(n):
    return n + LANES if (n // 32) % 8 == 0 else n


def _bf16_weight(w):
    k, n = w.shape
    return jnp.pad(w, ((0, 0), (0, _padded(n) - n))).astype(BF16)


def _const_spec(shape):
    nd = len(shape)
    return pl.BlockSpec(shape, lambda *_: (0,) * nd, pipeline_mode=pl.Buffered(1))


def _in_proj_kernel(x_ref, g_ref, w_ref, wg_ref, wf_ref, xa_ref, ga_ref, q_ref, k_ref, v_ref,
                    gab_ref, f_ref):
    h = _rms(x_ref[...], g_ref[...]).astype(BF16)

    def mm(lo, hi):
        return jnp.dot(h, w_ref[:, lo:hi], preferred_element_type=F32)

    xa_ref[...] = mm(_OFF_XA, _OFF_GA)
    ga_ref[...] = mm(_OFF_GA, _OFF_Q)
    q_ref[...] = (mm(_OFF_Q, _OFF_K) * (LOG2E / math.sqrt(HEAD_DIM))).astype(BF16)
    k_ref[...] = mm(_OFF_K, _OFF_V).astype(BF16)
    v_ref[...] = mm(_OFF_V, _OFF_F).astype(BF16)
    gab_ref[...] = jnp.dot(h, wg_ref[:, :2 * D_MODEL], preferred_element_type=F32)
    f_ref[...] = jnp.dot(h, wf_ref[...], preferred_element_type=F32)


def _in_proj(x2, gain, w_main, w_gate, w_f, tm):
    t = x2.shape[0]
    row = lambda n: pl.BlockSpec((tm, n), lambda i: (i, 0))
    return pl.pallas_call(
        _in_proj_kernel,
        grid=(t // tm,),
        in_specs=[row(D_MODEL), _const_spec((1, D_MODEL)), _const_spec(w_main.shape),
                  _const_spec(w_gate.shape), _const_spec(w_f.shape)],
        out_specs=[row(D_RNN), row(D_RNN), row(D_ATT), row(D_ATT), row(D_ATT),
                   row(2 * D_MODEL), row(LANES)],
        out_shape=[jax.ShapeDtypeStruct((t, D_RNN), F32), jax.ShapeDtypeStruct((t, D_RNN), F32),
                   jax.ShapeDtypeStruct((t, D_ATT), BF16), jax.ShapeDtypeStruct((t, D_ATT), BF16),
                   jax.ShapeDtypeStruct((t, D_ATT), BF16),
                   jax.ShapeDtypeStruct((t, 2 * D_MODEL), F32),
                   jax.ShapeDtypeStruct((t, LANES), F32)],
        compiler_params=pltpu.CompilerParams(dimension_semantics=("arbitrary",),
                                             vmem_limit_bytes=VMEM_LIMIT),
        name="in_proj",
    )(x2, gain, w_main, w_gate, w_f)


def _scan_pitch(tc):
    seg = tc // SUBLANES
    return seg + SUBLANES if (seg // SUBLANES) % 2 == 0 else seg


def _rglru_kernel(xa_ref, ga_ref, cw_ref, cb_ref, wg_ref, ba_ref, bx_ref, lam_ref, wp_ref,
                  pa_ref, xpad, a_s, b_s, h_s, p_s, y_s, hprev, *, tc):
    seg = tc // SUBLANES
    pitch = _scan_pitch(tc)
    t_idx = pl.program_id(1)

    @pl.when(t_idx == 0)
    def _():
        xpad[0:SUBLANES, :] = jnp.zeros((SUBLANES, D_RNN), F32)
        hprev[...] = jnp.zeros_like(hprev)

    xpad[SUBLANES:SUBLANES + tc, :] = xa_ref[0]
    xc = cb_ref[...] + cw_ref[CONV_A - 1:CONV_A, :] * xpad[SUBLANES:SUBLANES + tc, :]
    for kk in range(CONV_A - 1):
        sh = CONV_A - 1 - kk
        xc = xc + cw_ref[kk:kk + 1, :] * xpad[SUBLANES - sh:SUBLANES - sh + tc, :]
    xpad[0:SUBLANES, :] = xpad[tc:tc + SUBLANES, :]

    xcb = xc.astype(BF16)
    sp = jax.nn.softplus(-lam_ref[...])
    lane_chunks = GATE_TILE // LANES
    for j in range(N_GATE_TILES):
        ks = GATE_STARTS[j]
        cols = slice(j * GATE_TILE, (j + 1) * GATE_TILE)
        g = jnp.dot(xcb[:, ks:ks + GATE_WIN], wg_ref[j, :, :2 * GATE_TILE],
                    preferred_element_type=F32)
        r = jax.nn.sigmoid(g[:, :GATE_TILE] + ba_ref[:, cols])
        i = jax.nn.sigmoid(g[:, GATE_TILE:] + bx_ref[:, cols])
        log_a = -LRU_C * r * sp[:, cols]
        a = jnp.exp(log_a)
        b = jnp.sqrt(-jnp.tanh(log_a) * (a * a + 1.0)) * (i * xc[:, cols])
        for c in range(lane_chunks):
            for s in range(SUBLANES):
                dst = slice(s * pitch, s * pitch + seg)
                a_s[j * lane_chunks + c, dst, :] = a[s * seg:(s + 1) * seg, c * LANES:(c + 1) * LANES]
                b_s[j * lane_chunks + c, dst, :] = b[s * seg:(s + 1) * seg, c * LANES:(c + 1) * LANES]

    n_lane = D_RNN // LANES
    h8 = [jnp.zeros((SUBLANES, LANES), F32) for _ in range(n_lane)]
    p8 = [jnp.ones((SUBLANES, LANES), F32) for _ in range(n_lane)]
    for j in range(seg):
        rows = pl.ds(j, SUBLANES, stride=pitch)
        for c in range(n_lane):
            a_j = a_s[c, rows, :]
            h8[c] = a_j * h8[c] + b_s[c, rows, :]
            p8[c] = p8[c] * a_j
            h_s[c, rows, :] = h8[c]
            p_s[c, rows, :] = p8[c]

    for c in range(n_lane):
        cols = slice(c * LANES, (c + 1) * LANES)
        start = hprev[:, cols]
        for s in range(SUBLANES):
            rows = slice(s * seg, (s + 1) * seg)
            src = slice(s * pitch, s * pitch + seg)
            hfin = h_s[c, src, :] + p_s[c, src, :] * start
            y_s[rows, cols] = (jax.nn.gelu(ga_ref[0, rows, cols]) * hfin).astype(BF16)
            start = h8[c][s:s + 1, :] + p8[c][s:s + 1, :] * start
        hprev[:, cols] = start

    pa_ref[0] = jnp.dot(y_s[...], wp_ref[:, :D_MODEL], preferred_element_type=F32)


def _rglru(xa3, ga3, cw, cb, wg, ba, bx, lam, wp, tc):
    b, s, _ = xa3.shape
    blk = lambda n: pl.BlockSpec((1, tc, n), lambda bi, ti: (bi, ti, 0))
    scan_shape = (D_RNN // LANES, SUBLANES * _scan_pitch(tc), LANES)
    return pl.pallas_call(
        functools.partial(_rglru_kernel, tc=tc),
        grid=(b, s // tc),
        in_specs=[blk(D_RNN), blk(D_RNN), _const_spec((CONV_A, D_RNN)), _const_spec((1, D_RNN)),
                  _const_spec((N_GATE_TILES, GATE_WIN, _padded(2 * GATE_TILE))),
                  _const_spec((1, D_RNN)), _const_spec((1, D_RNN)), _const_spec((1, D_RNN)),
                  _const_spec((D_RNN, _padded(D_MODEL)))],
        out_specs=blk(D_MODEL),
        out_shape=jax.ShapeDtypeStruct((b, s, D_MODEL), F32),
        scratch_shapes=[pltpu.VMEM((tc + SUBLANES, D_RNN), F32),
                        pltpu.VMEM(scan_shape, F32), pltpu.VMEM(scan_shape, F32),
                        pltpu.VMEM(scan_shape, F32), pltpu.VMEM(scan_shape, F32),
                        pltpu.VMEM((tc, D_RNN), BF16), pltpu.VMEM((1, D_RNN), F32)],
        compiler_params=pltpu.CompilerParams(dimension_semantics=("arbitrary", "arbitrary"),
                                             vmem_limit_bytes=VMEM_LIMIT),
        name="rglru",
    )(xa3, ga3, cw, cb, wg, ba, bx, lam, wp)


def _fox_attn_kernel(q_ref, k_ref, v_ref, f_ref, bf_ref, o_ref, cum, cum_t, qm_s, va_s, cq_s,
                     ck_s, sc_s, p_s, m_s, off_s, o0_s, *, s, tq):
    pair = pl.program_id(1)

    @pl.when(pair == 0)
    def _():
        c = jax.nn.log_sigmoid(f_ref[0] + bf_ref[...])
        row = lax.broadcasted_iota(jnp.int32, (s, LANES), 0)
        d = 1
        while d < s:
            c = c + jnp.where(row >= d, pltpu.roll(c, d, axis=0), 0.0)
            d *= 2
        c = c * LOG2E
        cum[...] = c
        cum_t[...] = c.T

    lane = lax.broadcasted_iota(jnp.int32, (1, LANES), 1)
    causal = (lax.broadcasted_iota(jnp.int32, (tq, tq), 1)
              <= lax.broadcasted_iota(jnp.int32, (tq, tq), 0))
    halves = tq // LANES
    n_q = s // tq

    q = q_ref[0]
    v = v_ref[0]
    for hh in range(2):
        head = 2 * pair + hh
        in_head = (lane >= hh * HEAD_DIM) & (lane < (hh + 1) * HEAD_DIM)
        qm_s[hh] = jnp.where(in_head, q, jnp.zeros_like(q))
        va_s[hh] = jnp.where(in_head, v, jnp.ones_like(v))
        cq_s[hh] = jnp.sum(jnp.where(lane == head, cum[...], 0.0), axis=-1, keepdims=True)
        ck_s[hh] = cum_t[pl.ds(head, 1), :]

    tasks = [(hh, i) for hh in range(2) for i in range(n_q)]

    def score_tile(n, j):
        hh, i = tasks[n]
        slot = n % 2
        ks = slice(j * tq, (j + 1) * tq)
        sc = lax.dot_general(qm_s[hh, i * tq:(i + 1) * tq, :], k_ref[0, ks, :],
                             (((1,), (1,)), ((), ())), preferred_element_type=F32)
        sc = sc - ck_s[hh, :, ks]
        if j == i:
            sc = jnp.where(causal, sc, -jnp.inf)
        sc_s[slot, :, ks] = sc
        part = sc[:, 0:LANES]
        for c in range(1, halves):
            part = jnp.maximum(part, sc[:, c * LANES:(c + 1) * LANES])
        m_s[slot] = part if j == 0 else jnp.maximum(m_s[slot], part)

    def finish_scores(n):
        hh, i = tasks[n]
        slot = n % 2
        m = jnp.max(m_s[slot], axis=-1, keepdims=True)
        cq_i = cq_s[hh, i * tq:(i + 1) * tq, :]
        off_s[slot] = jnp.broadcast_to((m + cq_i) - cq_i, (tq, LANES))

    def exp_tile(n, j):
        slot = n % 2
        for c in range(halves):
            cs = slice(j * tq + c * LANES, j * tq + (c + 1) * LANES)
            p_s[slot, :, cs] = jnp.exp2(sc_s[slot, :, cs] - off_s[slot]).astype(BF16)

    def weighted_values(n):
        hh, i = tasks[n]
        slot = n % 2
        qs, qe = i * tq, (i + 1) * tq
        o = jnp.dot(p_s[slot, :, 0:qe], va_s[hh, 0:qe, :], preferred_element_type=F32)
        sum_lane = (1 - hh) * HEAD_DIM
        o = o / o[:, sum_lane:sum_lane + 1]
        if hh == 0:
            o0_s[qs:qe, :] = o
        else:
            o_ref[0, qs:qe, :] = jnp.where(lane < HEAD_DIM, o0_s[qs:qe, :], o).astype(BF16)

    for j in range(tasks[0][1] + 1):
        score_tile(0, j)
    finish_scores(0)
    for n in range(len(tasks)):
        n_exp = tasks[n][1] + 1
        n_score = tasks[n + 1][1] + 1 if n + 1 < len(tasks) else 0
        for j in range(max(n_exp, n_score)):
            if j < n_score:
                score_tile(n + 1, j)
            if j < n_exp:
                exp_tile(n, j)
        if n_score:
            finish_scores(n + 1)
        weighted_values(n)


def _fox_attn(q3, k3, v3, f3, bf, tq):
    b, s, _ = q3.shape
    blk = pl.BlockSpec((1, s, LANES), lambda bi, pi: (bi, 0, pi))
    return pl.pallas_call(
        functools.partial(_fox_attn_kernel, s=s, tq=tq),
        grid=(b, H_ATT // 2),
        in_specs=[blk, blk, blk, pl.BlockSpec((1, s, LANES), lambda bi, pi: (bi, 0, 0)),
                  _const_spec((1, LANES))],
        out_specs=blk,
        out_shape=jax.ShapeDtypeStruct((b, s, D_ATT), BF16),
        scratch_shapes=[pltpu.VMEM((s, LANES), F32), pltpu.VMEM((LANES, s), F32),
                        pltpu.VMEM((2, s, LANES), BF16), pltpu.VMEM((2, s, LANES), BF16),
                        pltpu.VMEM((2, s, 1), F32), pltpu.VMEM((2, 1, s), F32),
                        pltpu.VMEM((2, tq, s + LANES), F32), pltpu.VMEM((2, tq, s + LANES), BF16),
                        pltpu.VMEM((2, tq, LANES), F32), pltpu.VMEM((2, tq, LANES), F32),
                        pltpu.VMEM((s, LANES), F32)],
        compiler_params=pltpu.CompilerParams(dimension_semantics=("arbitrary", "arbitrary"),
                                             vmem_limit_bytes=VMEM_LIMIT),
        name="fox_attn",
    )(q3, k3, v3, f3, bf)


FF_CHUNK = 512


def _merge_ffn_kernel(x_ref, pa_ref, yb_ref, gab_ref, bm_ref, wpb_ref, wout_ref, gpost_ref,
                      gpre2_ref, gpost2_ref, perm_ref, wup_ref, cw_ref, cb_ref, wdn_ref, o_ref,
                      tail, unperm, x1_s, hp_s, *, tm, tiles_per_seq):
    i = pl.program_id(0)
    seg = tm // SUBLANES
    pitch = _scan_pitch(tm)

    @pl.when(i == 0)
    def _():
        x1_s[...] = jnp.zeros_like(x1_s)
        hp_s[...] = jnp.zeros_like(hp_s)
        tail[...] = jnp.zeros_like(tail)

    @pl.when((i + tiles_per_seq - 1) % tiles_per_seq == 0)
    def _():
        tail[...] = jnp.zeros_like(tail)

    x1_prev = x1_s[...]
    hp = hp_s[...]

    def merge_stage(k, st):
        if k == 0:
            st["gates"] = jax.nn.sigmoid(gab_ref[...] + bm_ref[...])
        elif k == 1:
            st["pb"] = jnp.dot(yb_ref[...], wpb_ref[:, :D_MODEL], preferred_element_type=F32)
        elif k == 2:
            g = st.pop("gates")
            st["merged"] = (g[:, :D_MODEL] * pa_ref[...] + g[:, D_MODEL:] * st.pop("pb")).astype(BF16)
        elif k == 3:
            st["mix"] = jnp.dot(st.pop("merged"), wout_ref[:, :D_MODEL], preferred_element_type=F32)
        elif k == 4:
            x1_s[...] = x_ref[...] + _rms(st.pop("mix"), gpost_ref[...])
        elif k == 5:
            h = _rms(x1_s[...], gpre2_ref[...]).astype(BF16)
            hp_s[...] = jnp.dot(perm_ref[...], h, preferred_element_type=F32).astype(BF16)

    first = lax.broadcasted_iota(jnp.int32, (SUBLANES, FF_CHUNK), 0) == 0

    def up_proj(c):
        return [jnp.dot(hp, wup_ref[:, off + c * FF_CHUNK:off + (c + 1) * FF_CHUNK],
                        preferred_element_type=F32) for off in (0, D_FF)]

    def conv(u, cols):
        prev = [jnp.where(first, pltpu.roll(tail[g * SUBLANES:(g + 1) * SUBLANES, cols], 1, axis=0),
                          pltpu.roll(u[tm - (2 - g) * SUBLANES:tm - (1 - g) * SUBLANES, :], 1, axis=0))
                for g in range(CONV_F - 1)]
        tail[:, cols] = u[tm - (CONV_F - 1) * SUBLANES:, :]
        u1 = jnp.concatenate([prev[1], u[:tm - SUBLANES, :]], axis=0)
        u2 = jnp.concatenate([prev[0], prev[1], u[:tm - 2 * SUBLANES, :]], axis=0)
        return (cb_ref[:, cols] + cw_ref[2:3, cols] * u + cw_ref[1:2, cols] * u1
                + cw_ref[0:1, cols] * u2)

    n_chunks = D_FF // FF_CHUNK
    stages = {}
    acc = jnp.zeros((tm, D_MODEL), F32)
    u_next = up_proj(0)
    for c in range(n_chunks):
        u_gate, u_val = u_next
        if c + 1 < n_chunks:
            u_next = up_proj(c + 1)
        merge_stage(c, stages)
        gate = conv(u_gate, slice(c * FF_CHUNK, (c + 1) * FF_CHUNK))
        val = conv(u_val, slice(D_FF + c * FF_CHUNK, D_FF + (c + 1) * FF_CHUNK))
        act = (jax.nn.gelu(gate) * val).astype(BF16)
        acc = acc + jnp.dot(act, wdn_ref[c * FF_CHUNK:(c + 1) * FF_CHUNK, :D_MODEL],
                            preferred_element_type=F32)

    for j in range(seg):
        for c in range(D_MODEL // LANES):
            unperm[c, pl.ds(j, SUBLANES, stride=pitch), :] = (
                acc[j * SUBLANES:(j + 1) * SUBLANES, c * LANES:(c + 1) * LANES])
    ffn = jnp.concatenate(
        [jnp.concatenate([unperm[c, s * pitch:s * pitch + seg, :] for s in range(SUBLANES)], axis=0)
         for c in range(D_MODEL // LANES)], axis=1)
    o_ref[...] = x1_prev + _rms(ffn, gpost2_ref[...])


def _merge_ffn(x2, pa2, yb2, gab2, bm, wpb, wout, gpost, gpre2, gpost2, wup, cw, cb, wdn,
               tm, seq):
    t = x2.shape[0]
    n_tiles = t // tm
    row = lambda n: pl.BlockSpec((tm, n), lambda i: (jnp.minimum(i, n_tiles - 1), 0))
    r = jnp.arange(tm)
    perm = (r[None, :] == ((r % SUBLANES) * (tm // SUBLANES) + r // SUBLANES)[:, None]).astype(BF16)
    return pl.pallas_call(
        functools.partial(_merge_ffn_kernel, tm=tm, tiles_per_seq=seq // tm),
        grid=(n_tiles + 1,),
        in_specs=[row(D_MODEL), row(D_MODEL), row(D_ATT), row(2 * D_MODEL),
                  _const_spec((1, 2 * D_MODEL)), _const_spec((D_ATT, _padded(D_MODEL))),
                  _const_spec((D_MODEL, _padded(D_MODEL))), _const_spec((1, D_MODEL)),
                  _const_spec((1, D_MODEL)), _const_spec((1, D_MODEL)), _const_spec((tm, tm)),
                  _const_spec((D_MODEL, _padded(2 * D_FF))), _const_spec((CONV_F, 2 * D_FF)),
                  _const_spec((1, 2 * D_FF)), _const_spec((D_FF, _padded(D_MODEL)))],
        out_specs=pl.BlockSpec((tm, D_MODEL), lambda i: (jnp.maximum(i - 1, 0), 0)),
        out_shape=jax.ShapeDtypeStruct((t, D_MODEL), F32),
        scratch_shapes=[pltpu.VMEM(((CONV_F - 1) * SUBLANES, 2 * D_FF), F32),
                        pltpu.VMEM((D_MODEL // LANES, SUBLANES * _scan_pitch(tm), LANES), F32),
                        pltpu.VMEM((tm, D_MODEL), F32), pltpu.VMEM((tm, D_MODEL), BF16)],
        compiler_params=pltpu.CompilerParams(dimension_semantics=("arbitrary",),
                                             vmem_limit_bytes=VMEM_LIMIT),
        name="merge_ffn",
    )(x2, pa2, yb2, gab2, bm, wpb, wout, gpost, gpre2, gpost2, perm, wup, cw, cb, wdn)


def _block_diag(w):
    h, n, _ = w.shape
    col = jnp.arange(h * n)
    tile = (col[None, :] % n == jnp.arange(n)[:, None]).astype(w.dtype)
    rep = jnp.dot(w.reshape(h * n, n), tile, precision=lax.Precision.HIGHEST)
    return jnp.where(col[:, None] // n == col[None, :] // n, rep, 0.0)


def _gate_windows(w_a, w_x):
    da, dx = _block_diag(w_a), _block_diag(w_x)
    tiles = []
    for j, ks in enumerate(GATE_STARTS):
        cols = slice(j * GATE_TILE, (j + 1) * GATE_TILE)
        tiles.append(jnp.concatenate([da[ks:ks + GATE_WIN, cols], dx[ks:ks + GATE_WIN, cols]],
                                     axis=1))
    return jnp.stack([_bf16_weight(t) for t in tiles])


def _layer(x, mix_norm_pre, mix_norm_post, w_in, conv_a_w, conv_a_b, w_rg_a, b_rg_a, w_rg_x,
           b_rg_x, lru_lambda, b_forget, b_merge, w_proj_a, w_proj_b, w_out, ffn_norm_pre,
           ffn_norm_post, w_up, conv_f_w, conv_f_b, w_down):
    bsz, seq, _ = x.shape
    t = bsz * seq
    row = lambda a: a.reshape(1, -1)

    w_main = _bf16_weight(w_in[:, :_OFF_F])
    w_gate = _bf16_weight(w_in[:, _OFF_G:])
    w_f = jnp.pad(w_in[:, _OFF_F:_OFF_G].astype(BF16), ((0, 0), (0, LANES - H_ATT)))
    bf = jnp.concatenate([b_forget, jnp.zeros((LANES - H_ATT,), b_forget.dtype)]).reshape(1, LANES)
    wg = _gate_windows(w_rg_a, w_rg_x)

    x2 = x.reshape(t, D_MODEL)
    xa, ga, q, k, v, gab, f = _in_proj(x2, row(mix_norm_pre), w_main, w_gate, w_f,
                                       tm=IN_PROJ_ROWS)

    r3 = lambda a: a.reshape(bsz, seq, a.shape[-1])
    pa = _rglru(r3(xa), r3(ga), conv_a_w, row(conv_a_b), wg, row(b_rg_a), row(b_rg_x),
                row(lru_lambda), _bf16_weight(w_proj_a), tc=RGLRU_ROWS)
    yb = _fox_attn(r3(q), r3(k), r3(v), r3(f), bf, tq=ATTN_Q_ROWS)

    out = _merge_ffn(x2, pa.reshape(t, D_MODEL), yb.reshape(t, D_ATT), gab, row(b_merge),
                     _bf16_weight(w_proj_b), _bf16_weight(w_out), row(mix_norm_post),
                     row(ffn_norm_pre), row(ffn_norm_post), _bf16_weight(w_up), conv_f_w,
                     row(conv_f_b), _bf16_weight(w_down), tm=FFN_ROWS, seq=seq)
    return out.reshape(bsz, seq, D_MODEL)


def kernel(x, mix_norm_pre, mix_norm_post, w_in, conv_a_w, conv_a_b, w_rg_a, b_rg_a, w_rg_x, b_rg_x, lru_lambda, b_forget, b_merge, w_proj_a, w_proj_b, w_out, ffn_norm_pre, ffn_norm_post, w_up, conv_f_w, conv_f_b, w_down):
    depth = w_in.shape[0]
    for layer in range(depth):
        x = _layer(x, mix_norm_pre[layer], mix_norm_post[layer], w_in[layer], conv_a_w[layer],
                   conv_a_b[layer], w_rg_a[layer], b_rg_a[layer], w_rg_x[layer], b_rg_x[layer],
                   lru_lambda[layer], b_forget[layer], b_merge[layer], w_proj_a[layer],
                   w_proj_b[layer], w_out[layer], ffn_norm_pre[layer], ffn_norm_post[layer],
                   w_up[layer], conv_f_w[layer], conv_f_b[layer], w_down[layer])
    return x
```

```python
import functools
import math

import jax
import jax.numpy as jnp
from jax import lax
from jax.experimental import pallas as pl
from jax.experimental.pallas import tpu as pltpu

D_MODEL = 1024
D_RNN = 1280
H_RNN = 16
RNN_BLOCK = D_RNN // H_RNN
CONV_A = 4
LRU_C = 8.0
H_ATT = 16
HEAD_DIM = 64
D_ATT = H_ATT * HEAD_DIM
D_FF = 3 * D_MODEL
CONV_F = 3
RMS_EPS = 1e-6

LANES = 128
SUBLANES = 8
MXU_COLS = 256
VMEM_LIMIT = 56 * 1024 * 1024

IN_PROJ_ROWS = 512
RGLRU_ROWS = 512
ATTN_Q_ROWS = 256
FFN_ROWS = 256

LOG2E = math.log2(math.e)

F32 = jnp.float32
BF16 = jnp.bfloat16

_OFF_XA = 0
_OFF_GA = _OFF_XA + D_RNN
_OFF_Q = _OFF_GA + D_RNN
_OFF_K = _OFF_Q + D_ATT
_OFF_V = _OFF_K + D_ATT
_OFF_F = _OFF_V + D_ATT
_OFF_G = _OFF_F + H_ATT

GATE_TILE = MXU_COLS
GATE_WIN = 2 * MXU_COLS
N_GATE_TILES = D_RNN // GATE_TILE


def _gate_window_start(j):
    lo = (j * GATE_TILE // RNN_BLOCK) * RNN_BLOCK
    start = min((lo // LANES) * LANES, D_RNN - GATE_WIN)
    hi = -(-((j + 1) * GATE_TILE) // RNN_BLOCK) * RNN_BLOCK
    assert start <= lo and hi <= start + GATE_WIN
    return start


GATE_STARTS = tuple(_gate_window_start(j) for j in range(N_GATE_TILES))


def _rms(x, gain):
    y = x * lax.rsqrt(jnp.mean(x * x, axis=-1, keepdims=True) + RMS_EPS)
    return y * gain


def _padded(n):
    return n + LANES if (n // 32) % 8 == 0 else n


def _bf16_weight(w):
    k, n = w.shape
    return jnp.pad(w, ((0, 0), (0, _padded(n) - n))).astype(BF16)


def _const_spec(shape):
    nd = len(shape)
    return pl.BlockSpec(shape, lambda *_: (0,) * nd, pipeline_mode=pl.Buffered(1))


def _in_proj_kernel(x_ref, g_ref, w_ref, wg_ref, wf_ref, xa_ref, ga_ref, q_ref, k_ref, v_ref,
                    gab_ref, f_ref):
    h = _rms(x_ref[...], g_ref[...]).astype(BF16)

    def mm(lo, hi):
        return jnp.dot(h, w_ref[:, lo:hi], preferred_element_type=F32)

    xa_ref[...] = mm(_OFF_XA, _OFF_GA)
    ga_ref[...] = mm(_OFF_GA, _OFF_Q)
    q_ref[...] = (mm(_OFF_Q, _OFF_K) * (LOG2E / math.sqrt(HEAD_DIM))).astype(BF16)
    k_ref[...] = mm(_OFF_K, _OFF_V).astype(BF16)
    v_ref[...] = mm(_OFF_V, _OFF_F).astype(BF16)
    gab_ref[...] = jnp.dot(h, wg_ref[:, :2 * D_MODEL], preferred_element_type=F32)
    f_ref[...] = jnp.dot(h, wf_ref[...], preferred_element_type=F32)


def _in_proj(x2, gain, w_main, w_gate, w_f, tm):
    t = x2.shape[0]
    row = lambda n: pl.BlockSpec((tm, n), lambda i: (i, 0))
    return pl.pallas_call(
        _in_proj_kernel,
        grid=(t // tm,),
        in_specs=[row(D_MODEL), _const_spec((1, D_MODEL)), _const_spec(w_main.shape),
                  _const_spec(w_gate.shape), _const_spec(w_f.shape)],
        out_specs=[row(D_RNN), row(D_RNN), row(D_ATT), row(D_ATT), row(D_ATT),
                   row(2 * D_MODEL), row(LANES)],
        out_shape=[jax.ShapeDtypeStruct((t, D_RNN), F32), jax.ShapeDtypeStruct((t, D_RNN), F32),
                   jax.ShapeDtypeStruct((t, D_ATT), BF16), jax.ShapeDtypeStruct((t, D_ATT), BF16),
                   jax.ShapeDtypeStruct((t, D_ATT), BF16),
                   jax.ShapeDtypeStruct((t, 2 * D_MODEL), F32),
                   jax.ShapeDtypeStruct((t, LANES), F32)],
        compiler_params=pltpu.CompilerParams(dimension_semantics=("arbitrary",),
                                             vmem_limit_bytes=VMEM_LIMIT),
        name="in_proj",
    )(x2, gain, w_main, w_gate, w_f)


def _scan_pitch(tc):
    seg = tc // SUBLANES
    return seg + SUBLANES if (seg // SUBLANES) % 2 == 0 else seg


def _rglru_kernel(xa_ref, ga_ref, cw_ref, cb_ref, wg_ref, ba_ref, bx_ref, lam_ref, wp_ref,
                  pa_ref, xpad, a_s, b_s, h_s, p_s, y_s, hprev, *, tc):
    seg = tc // SUBLANES
    pitch = _scan_pitch(tc)
    t_idx = pl.program_id(1)

    @pl.when(t_idx == 0)
    def _():
        xpad[0:SUBLANES, :] = jnp.zeros((SUBLANES, D_RNN), F32)
        hprev[...] = jnp.zeros_like(hprev)

    xpad[SUBLANES:SUBLANES + tc, :] = xa_ref[0]
    xc = cb_ref[...] + cw_ref[CONV_A - 1:CONV_A, :] * xpad[SUBLANES:SUBLANES + tc, :]
    for kk in range(CONV_A - 1):
        sh = CONV_A - 1 - kk
        xc = xc + cw_ref[kk:kk + 1, :] * xpad[SUBLANES - sh:SUBLANES - sh + tc, :]
    xpad[0:SUBLANES, :] = xpad[tc:tc + SUBLANES, :]

    xcb = xc.astype(BF16)
    sp = jax.nn.softplus(-lam_ref[...])
    lane_chunks = GATE_TILE // LANES
    for j in range(N_GATE_TILES):
        ks = GATE_STARTS[j]
        cols = slice(j * GATE_TILE, (j + 1) * GATE_TILE)
        g = jnp.dot(xcb[:, ks:ks + GATE_WIN], wg_ref[j, :, :2 * GATE_TILE],
                    preferred_element_type=F32)
        r = jax.nn.sigmoid(g[:, :GATE_TILE] + ba_ref[:, cols])
        i = jax.nn.sigmoid(g[:, GATE_TILE:] + bx_ref[:, cols])
        log_a = -LRU_C * r * sp[:, cols]
        a = jnp.exp(log_a)
        b = jnp.sqrt(-jnp.tanh(log_a) * (a * a + 1.0)) * (i * xc[:, cols])
        for c in range(lane_chunks):
            for s in range(SUBLANES):
                dst = slice(s * pitch, s * pitch + seg)
                a_s[j * lane_chunks + c, dst, :] = a[s * seg:(s + 1) * seg, c * LANES:(c + 1) * LANES]
                b_s[j * lane_chunks + c, dst, :] = b[s * seg:(s + 1) * seg, c * LANES:(c + 1) * LANES]

    n_lane = D_RNN // LANES
    h8 = [jnp.zeros((SUBLANES, LANES), F32) for _ in range(n_lane)]
    p8 = [jnp.ones((SUBLANES, LANES), F32) for _ in range(n_lane)]
    for j in range(seg):
        rows = pl.ds(j, SUBLANES, stride=pitch)
        for c in range(n_lane):
            a_j = a_s[c, rows, :]
            h8[c] = a_j * h8[c] + b_s[c, rows, :]
            p8[c] = p8[c] * a_j
            h_s[c, rows, :] = h8[c]
            p_s[c, rows, :] = p8[c]

    for c in range(n_lane):
        cols = slice(c * LANES, (c + 1) * LANES)
        start = hprev[:, cols]
        for s in range(SUBLANES):
            rows = slice(s * seg, (s + 1) * seg)
            src = slice(s * pitch, s * pitch + seg)
            hfin = h_s[c, src, :] + p_s[c, src, :] * start
            y_s[rows, cols] = (jax.nn.gelu(ga_ref[0, rows, cols]) * hfin).astype(BF16)
            start = h8[c][s:s + 1, :] + p8[c][s:s + 1, :] * start
        hprev[:, cols] = start

    pa_ref[0] = jnp.dot(y_s[...], wp_ref[:, :D_MODEL], preferred_element_type=F32)


def _rglru(xa3, ga3, cw, cb, wg, ba, bx, lam, wp, tc):
    b, s, _ = xa3.shape
    blk = lambda n: pl.BlockSpec((1, tc, n), lambda bi, ti: (bi, ti, 0))
    scan_shape = (D_RNN // LANES, SUBLANES * _scan_pitch(tc), LANES)
    return pl.pallas_call(
        functools.partial(_rglru_kernel, tc=tc),
        grid=(b, s // tc),
        in_specs=[blk(D_RNN), blk(D_RNN), _const_spec((CONV_A, D_RNN)), _const_spec((1, D_RNN)),
                  _const_spec((N_GATE_TILES, GATE_WIN, _padded(2 * GATE_TILE))),
                  _const_spec((1, D_RNN)), _const_spec((1, D_RNN)), _const_spec((1, D_RNN)),
                  _const_spec((D_RNN, _padded(D_MODEL)))],
        out_specs=blk(D_MODEL),
        out_shape=jax.ShapeDtypeStruct((b, s, D_MODEL), F32),
        scratch_shapes=[pltpu.VMEM((tc + SUBLANES, D_RNN), F32),
                        pltpu.VMEM(scan_shape, F32), pltpu.VMEM(scan_shape, F32),
                        pltpu.VMEM(scan_shape, F32), pltpu.VMEM(scan_shape, F32),
                        pltpu.VMEM((tc, D_RNN), BF16), pltpu.VMEM((1, D_RNN), F32)],
        compiler_params=pltpu.CompilerParams(dimension_semantics=("arbitrary", "arbitrary"),
                                             vmem_limit_bytes=VMEM_LIMIT),
        name="rglru",
    )(xa3, ga3, cw, cb, wg, ba, bx, lam, wp)


def _fox_attn_kernel(q_ref, k_ref, v_ref, f_ref, bf_ref, o_ref, cum, cum_t, km_s, vt_s, ck_s,
                     cq_s, sc_s, p_s, m_s, off_s, o0_s, *, s, tq):
    pair = pl.program_id(1)

    @pl.when(pair == 0)
    def _():
        c = jax.nn.log_sigmoid(f_ref[0] + bf_ref[...])
        row = lax.broadcasted_iota(jnp.int32, (s, LANES), 0)
        d = 1
        while d < s:
            c = c + jnp.where(row >= d, pltpu.roll(c, d, axis=0), 0.0)
            d *= 2
        c = c * LOG2E
        cum[...] = c
        cum_t[...] = c.T

    lane = lax.broadcasted_iota(jnp.int32, (1, LANES), 1)
    causal = (lax.broadcasted_iota(jnp.int32, (tq, tq), 0)
              <= lax.broadcasted_iota(jnp.int32, (tq, tq), 1))
    halves = tq // LANES
    n_q = s // tq

    k = k_ref[0]
    v = v_ref[0]
    for hh in range(2):
        head = 2 * pair + hh
        in_head = (lane >= hh * HEAD_DIM) & (lane < (hh + 1) * HEAD_DIM)
        km_s[hh] = jnp.where(in_head, k, jnp.zeros_like(k))
        vt_s[hh] = jnp.where(in_head, v, jnp.ones_like(v)).astype(F32).T.astype(BF16)
        ck = jnp.sum(jnp.where(lane == head, cum[...], 0.0), axis=-1, keepdims=True)
        ck_s[hh] = jnp.broadcast_to(ck, (s, LANES))
        cq_s[hh] = cum_t[pl.ds(head, 1), :]

    tasks = [(hh, i) for hh in range(2) for i in range(n_q)]

    def score_tile(n, j):
        hh, i = tasks[n]
        slot = n % 2
        ks = slice(j * tq, (j + 1) * tq)
        sc = lax.dot_general(km_s[hh, ks, :], q_ref[0, i * tq:(i + 1) * tq, :],
                             (((1,), (1,)), ((), ())), preferred_element_type=F32)
        sc = jnp.concatenate([sc[:, c * LANES:(c + 1) * LANES] - ck_s[hh, ks, :]
                              for c in range(halves)], axis=1)
        if j == i:
            sc = jnp.where(causal, sc, -jnp.inf)
        sc_s[slot, ks, :] = sc
        part = sc[0:SUBLANES, :]
        for r in range(1, tq // SUBLANES):
            part = jnp.maximum(part, sc[r * SUBLANES:(r + 1) * SUBLANES, :])
        m_s[slot] = part if j == 0 else jnp.maximum(m_s[slot], part)

    def finish_scores(n):
        hh, i = tasks[n]
        slot = n % 2
        m = jnp.max(m_s[slot], axis=0, keepdims=True)
        cq_i = cq_s[hh, :, i * tq:(i + 1) * tq]
        off_s[slot] = (m + cq_i) - cq_i

    def exp_tile(n, j):
        slot = n % 2
        ks = slice(j * tq, (j + 1) * tq)
        p_s[slot, ks, :] = jnp.exp2(sc_s[slot, ks, :] - off_s[slot]).astype(BF16)

    def weighted_values(n):
        hh, i = tasks[n]
        slot = n % 2
        qs, qe = i * tq, (i + 1) * tq
        o = jnp.dot(vt_s[hh, :, 0:qe], p_s[slot, 0:qe, :], preferred_element_type=F32)
        sum_row = (1 - hh) * HEAD_DIM
        o = o / o[sum_row:sum_row + 1, :]
        if hh == 0:
            o0_s[i] = o
        else:
            row = lax.broadcasted_iota(jnp.int32, (LANES, 1), 0)
            o_ref[0, qs:qe, :] = jnp.where(row < HEAD_DIM, o0_s[i], o).T.astype(BF16)

    for j in range(tasks[0][1] + 1):
        score_tile(0, j)
    finish_scores(0)
    for n in range(len(tasks)):
        n_exp = tasks[n][1] + 1
        n_score = tasks[n + 1][1] + 1 if n + 1 < len(tasks) else 0
        for j in range(max(n_exp, n_score)):
            if j < n_score:
                score_tile(n + 1, j)
            if j < n_exp:
                exp_tile(n, j)
        if n_score:
            finish_scores(n + 1)
        weighted_values(n)


def _fox_attn(q3, k3, v3, f3, bf, tq):
    b, s, _ = q3.shape
    blk = pl.BlockSpec((1, s, LANES), lambda bi, pi: (bi, 0, pi))
    return pl.pallas_call(
        functools.partial(_fox_attn_kernel, s=s, tq=tq),
        grid=(b, H_ATT // 2),
        in_specs=[blk, blk, blk, pl.BlockSpec((1, s, LANES), lambda bi, pi: (bi, 0, 0)),
                  _const_spec((1, LANES))],
        out_specs=blk,
        out_shape=jax.ShapeDtypeStruct((b, s, D_ATT), BF16),
        scratch_shapes=[pltpu.VMEM((s, LANES), F32), pltpu.VMEM((LANES, s), F32),
                        pltpu.VMEM((2, s, LANES), BF16), pltpu.VMEM((2, LANES, s), BF16),
                        pltpu.VMEM((2, s, LANES), F32), pltpu.VMEM((2, 1, s), F32),
                        pltpu.VMEM((2, s, tq), F32), pltpu.VMEM((2, s, tq), BF16),
                        pltpu.VMEM((2, SUBLANES, tq), F32), pltpu.VMEM((2, 1, tq), F32),
                        pltpu.VMEM((s // tq, LANES, tq), F32)],
        compiler_params=pltpu.CompilerParams(dimension_semantics=("arbitrary", "arbitrary"),
                                             vmem_limit_bytes=VMEM_LIMIT),
        name="fox_attn",
    )(q3, k3, v3, f3, bf)


FF_CHUNK = 512


def _merge_ffn_kernel(x_ref, pa_ref, yb_ref, gab_ref, bm_ref, wpb_ref, wout_ref, gpost_ref,
                      gpre2_ref, gpost2_ref, perm_ref, wup_ref, cw_ref, cb_ref, wdn_ref, o_ref,
                      tail, unperm, x1_s, hp_s, *, tm, tiles_per_seq):
    i = pl.program_id(0)
    seg = tm // SUBLANES
    pitch = _scan_pitch(tm)

    @pl.when(i == 0)
    def _():
        x1_s[...] = jnp.zeros_like(x1_s)
        hp_s[...] = jnp.zeros_like(hp_s)
        tail[...] = jnp.zeros_like(tail)

    @pl.when((i + tiles_per_seq - 1) % tiles_per_seq == 0)
    def _():
        tail[...] = jnp.zeros_like(tail)

    x1_prev = x1_s[...]
    hp = hp_s[...]

    def merge_stage(k, st):
        if k == 0:
            st["gates"] = jax.nn.sigmoid(gab_ref[...] + bm_ref[...])
        elif k == 1:
            st["pb"] = jnp.dot(yb_ref[...], wpb_ref[:, :D_MODEL], preferred_element_type=F32)
        elif k == 2:
            g = st.pop("gates")
            st["merged"] = (g[:, :D_MODEL] * pa_ref[...] + g[:, D_MODEL:] * st.pop("pb")).astype(BF16)
        elif k == 3:
            st["mix"] = jnp.dot(st.pop("merged"), wout_ref[:, :D_MODEL], preferred_element_type=F32)
        elif k == 4:
            x1_s[...] = x_ref[...] + _rms(st.pop("mix"), gpost_ref[...])
        elif k == 5:
            h = _rms(x1_s[...], gpre2_ref[...]).astype(BF16)
            hp_s[...] = jnp.dot(perm_ref[...], h, preferred_element_type=F32).astype(BF16)

    first = lax.broadcasted_iota(jnp.int32, (SUBLANES, FF_CHUNK), 0) == 0

    def up_proj(c):
        return [jnp.dot(hp, wup_ref[:, off + c * FF_CHUNK:off + (c + 1) * FF_CHUNK],
                        preferred_element_type=F32) for off in (0, D_FF)]

    def conv(u, cols):
        prev = [jnp.where(first, pltpu.roll(tail[g * SUBLANES:(g + 1) * SUBLANES, cols], 1, axis=0),
                          pltpu.roll(u[tm - (2 - g) * SUBLANES:tm - (1 - g) * SUBLANES, :], 1, axis=0))
                for g in range(CONV_F - 1)]
        tail[:, cols] = u[tm - (CONV_F - 1) * SUBLANES:, :]
        u1 = jnp.concatenate([prev[1], u[:tm - SUBLANES, :]], axis=0)
        u2 = jnp.concatenate([prev[0], prev[1], u[:tm - 2 * SUBLANES, :]], axis=0)
        return (cb_ref[:, cols] + cw_ref[2:3, cols] * u + cw_ref[1:2, cols] * u1
                + cw_ref[0:1, cols] * u2)

    n_chunks = D_FF // FF_CHUNK
    stages = {}
    acc = jnp.zeros((tm, D_MODEL), F32)
    u_next = up_proj(0)
    for c in range(n_chunks):
        u_gate, u_val = u_next
        if c + 1 < n_chunks:
            u_next = up_proj(c + 1)
        merge_stage(c, stages)
        gate = conv(u_gate, slice(c * FF_CHUNK, (c + 1) * FF_CHUNK))
        val = conv(u_val, slice(D_FF + c * FF_CHUNK, D_FF + (c + 1) * FF_CHUNK))
        act = (jax.nn.gelu(gate) * val).astype(BF16)
        acc = acc + jnp.dot(act, wdn_ref[c * FF_CHUNK:(c + 1) * FF_CHUNK, :D_MODEL],
                            preferred_element_type=F32)

    for j in range(seg):
        for c in range(D_MODEL // LANES):
            unperm[c, pl.ds(j, SUBLANES, stride=pitch), :] = (
                acc[j * SUBLANES:(j + 1) * SUBLANES, c * LANES:(c + 1) * LANES])
    ffn = jnp.concatenate(
        [jnp.concatenate([unperm[c, s * pitch:s * pitch + seg, :] for s in range(SUBLANES)], axis=0)
         for c in range(D_MODEL // LANES)], axis=1)
    o_ref[...] = x1_prev + _rms(ffn, gpost2_ref[...])


def _merge_ffn(x2, pa2, yb2, gab2, bm, wpb, wout, gpost, gpre2, gpost2, wup, cw, cb, wdn,
               tm, seq):
    t = x2.shape[0]
    n_tiles = t // tm
    row = lambda n: pl.BlockSpec((tm, n), lambda i: (jnp.minimum(i, n_tiles - 1), 0))
    r = jnp.arange(tm)
    perm = (r[None, :] == ((r % SUBLANES) * (tm // SUBLANES) + r // SUBLANES)[:, None]).astype(BF16)
    return pl.pallas_call(
        functools.partial(_merge_ffn_kernel, tm=tm, tiles_per_seq=seq // tm),
        grid=(n_tiles + 1,),
        in_specs=[row(D_MODEL), row(D_MODEL), row(D_ATT), row(2 * D_MODEL),
                  _const_spec((1, 2 * D_MODEL)), _const_spec((D_ATT, _padded(D_MODEL))),
                  _const_spec((D_MODEL, _padded(D_MODEL))), _const_spec((1, D_MODEL)),
                  _const_spec((1, D_MODEL)), _const_spec((1, D_MODEL)), _const_spec((tm, tm)),
                  _const_spec((D_MODEL, _padded(2 * D_FF))), _const_spec((CONV_F, 2 * D_FF)),
                  _const_spec((1, 2 * D_FF)), _const_spec((D_FF, _padded(D_MODEL)))],
        out_specs=pl.BlockSpec((tm, D_MODEL), lambda i: (jnp.maximum(i - 1, 0), 0)),
        out_shape=jax.ShapeDtypeStruct((t, D_MODEL), F32),
        scratch_shapes=[pltpu.VMEM(((CONV_F - 1) * SUBLANES, 2 * D_FF), F32),
                        pltpu.VMEM((D_MODEL // LANES, SUBLANES * _scan_pitch(tm), LANES), F32),
                        pltpu.VMEM((tm, D_MODEL), F32), pltpu.VMEM((tm, D_MODEL), BF16)],
        compiler_params=pltpu.CompilerParams(dimension_semantics=("arbitrary",),
                                             vmem_limit_bytes=VMEM_LIMIT),
        name="merge_ffn",
    )(x2, pa2, yb2, gab2, bm, wpb, wout, gpost, gpre2, gpost2, perm, wup, cw, cb, wdn)


def _block_diag(w):
    h, n, _ = w.shape
    col = jnp.arange(h * n)
    tile = (col[None, :] % n == jnp.arange(n)[:, None]).astype(w.dtype)
    rep = jnp.dot(w.reshape(h * n, n), tile, precision=lax.Precision.HIGHEST)
    return jnp.where(col[:, None] // n == col[None, :] // n, rep, 0.0)


def _gate_windows(w_a, w_x):
    da, dx = _block_diag(w_a), _block_diag(w_x)
    tiles = []
    for j, ks in enumerate(GATE_STARTS):
        cols = slice(j * GATE_TILE, (j + 1) * GATE_TILE)
        tiles.append(jnp.concatenate([da[ks:ks + GATE_WIN, cols], dx[ks:ks + GATE_WIN, cols]],
                                     axis=1))
    return jnp.stack([_bf16_weight(t) for t in tiles])


def _layer(x, mix_norm_pre, mix_norm_post, w_in, conv_a_w, conv_a_b, w_rg_a, b_rg_a, w_rg_x,
           b_rg_x, lru_lambda, b_forget, b_merge, w_proj_a, w_proj_b, w_out, ffn_norm_pre,
           ffn_norm_post, w_up, conv_f_w, conv_f_b, w_down):
    bsz, seq, _ = x.shape
    t = bsz * seq
    row = lambda a: a.reshape(1, -1)

    w_main = _bf16_weight(w_in[:, :_OFF_F])
    w_gate = _bf16_weight(w_in[:, _OFF_G:])
    w_f = jnp.pad(w_in[:, _OFF_F:_OFF_G].astype(BF16), ((0, 0), (0, LANES - H_ATT)))
    bf = jnp.concatenate([b_forget, jnp.zeros((LANES - H_ATT,), b_forget.dtype)]).reshape(1, LANES)
    wg = _gate_windows(w_rg_a, w_rg_x)

    x2 = x.reshape(t, D_MODEL)
    xa, ga, q, k, v, gab, f = _in_proj(x2, row(mix_norm_pre), w_main, w_gate, w_f,
                                       tm=IN_PROJ_ROWS)

    r3 = lambda a: a.reshape(bsz, seq, a.shape[-1])
    pa = _rglru(r3(xa), r3(ga), conv_a_w, row(conv_a_b), wg, row(b_rg_a), row(b_rg_x),
                row(lru_lambda), _bf16_weight(w_proj_a), tc=RGLRU_ROWS)
    yb = _fox_attn(r3(q), r3(k), r3(v), r3(f), bf, tq=ATTN_Q_ROWS)

    out = _merge_ffn(x2, pa.reshape(t, D_MODEL), yb.reshape(t, D_ATT), gab, row(b_merge),
                     _bf16_weight(w_proj_b), _bf16_weight(w_out), row(mix_norm_post),
                     row(ffn_norm_pre), row(ffn_norm_post), _bf16_weight(w_up), conv_f_w,
                     row(conv_f_b), _bf16_weight(w_down), tm=FFN_ROWS, seq=seq)
    return out.reshape(bsz, seq, D_MODEL)


def kernel(x, mix_norm_pre, mix_norm_post, w_in, conv_a_w, conv_a_b, w_rg_a, b_rg_a, w_rg_x, b_rg_x, lru_lambda, b_forget, b_merge, w_proj_a, w_proj_b, w_out, ffn_norm_pre, ffn_norm_post, w_up, conv_f_w, conv_f_b, w_down):
    depth = w_in.shape[0]
    for layer in range(depth):
        x = _layer(x, mix_norm_pre[layer], mix_norm_post[layer], w_in[layer], conv_a_w[layer],
                   conv_a_b[layer], w_rg_a[layer], b_rg_a[layer], w_rg_x[layer], b_rg_x[layer],
                   lru_lambda[layer], b_forget[layer], b_merge[layer], w_proj_a[layer],
                   w_proj_b[layer], w_out[layer], ffn_norm_pre[layer], ffn_norm_post[layer],
                   w_up[layer], conv_f_w[layer], conv_f_b[layer], w_down[layer])
    return x
```

```python
import functools
import math

import jax
import jax.numpy as jnp
from jax import lax
from jax.experimental import pallas as pl
from jax.experimental.pallas import tpu as pltpu

D_MODEL = 1024
D_RNN = 1280
H_RNN = 16
RNN_BLOCK = D_RNN // H_RNN
CONV_A = 4
LRU_C = 8.0
H_ATT = 16
HEAD_DIM = 64
D_ATT = H_ATT * HEAD_DIM
D_FF = 3 * D_MODEL
CONV_F = 3
RMS_EPS = 1e-6

LANES = 128
SUBLANES = 8
MXU_COLS = 256
VMEM_LIMIT = 56 * 1024 * 1024

MIX_IN_ROWS = 256
ATTN_Q_ROWS = 256
FFN_ROWS = 256

LOG2E = math.log2(math.e)

F32 = jnp.float32
BF16 = jnp.bfloat16

_OFF_XA = 0
_OFF_GA = _OFF_XA + D_RNN
_OFF_Q = _OFF_GA + D_RNN
_OFF_K = _OFF_Q + D_ATT
_OFF_V = _OFF_K + D_ATT
_OFF_F = _OFF_V + D_ATT
_OFF_G = _OFF_F + H_ATT

GATE_TILE = MXU_COLS
GATE_WIN = 2 * MXU_COLS
N_GATE_TILES = D_RNN // GATE_TILE


def _gate_window_start(j):
    lo = (j * GATE_TILE // RNN_BLOCK) * RNN_BLOCK
    start = min((lo // LANES) * LANES, D_RNN - GATE_WIN)
    hi = -(-((j + 1) * GATE_TILE) // RNN_BLOCK) * RNN_BLOCK
    assert start <= lo and hi <= start + GATE_WIN
    return start


GATE_STARTS = tuple(_gate_window_start(j) for j in range(N_GATE_TILES))


def _rms(x, gain):
    y = x * lax.rsqrt(jnp.mean(x * x, axis=-1, keepdims=True) + RMS_EPS)
    return y * gain


def _padded(n):
    return n + LANES if (n // 32) % 8 == 0 else n


def _bf16_weight(w):
    k, n = w.shape
    return jnp.pad(w, ((0, 0), (0, _padded(n) - n))).astype(BF16)


def _const_spec(shape):
    nd = len(shape)
    return pl.BlockSpec(shape, lambda *_: (0,) * nd, pipeline_mode=pl.Buffered(1))


def _scan_pitch(tc):
    seg = tc // SUBLANES
    return seg + SUBLANES if (seg // SUBLANES) % 2 == 0 else seg


def _mix_in_kernel(x_ref, g_ref, w_ref, wgate_ref, wf_ref, cw_ref, cb_ref, wg_ref, ba_ref, bx_ref,
                   lam_ref, wp_ref, q_ref, k_ref, v_ref, gab_ref, f_ref, pa_ref,
                   xa_s, ga_in_s, xpad, ga_s, xc_s, xcb_s, a_s, b_s, h_s, p_s, y_s, hprev,
                   *, tc, tiles_per_seq):
    seg = tc // SUBLANES
    pitch = _scan_pitch(tc)
    n_lane = D_RNN // LANES
    step = pl.program_id(0)

    @pl.when(step == 0)
    def _():
        xa_s[...] = jnp.zeros_like(xa_s)
        ga_in_s[...] = jnp.zeros_like(ga_in_s)
        xpad[0:SUBLANES, :] = jnp.zeros((SUBLANES, D_RNN), F32)
        hprev[...] = jnp.zeros_like(hprev)

    @pl.when((step + tiles_per_seq - 1) % tiles_per_seq == 0)
    def _():
        xpad[0:SUBLANES, :] = jnp.zeros((SUBLANES, D_RNN), F32)
        hprev[...] = jnp.zeros_like(hprev)

    xpad[SUBLANES:SUBLANES + tc, :] = xa_s[...]
    ga_s[...] = ga_in_s[...]

    h = _rms(x_ref[...], g_ref[...]).astype(BF16)

    def mm(lo, hi):
        return jnp.dot(h, w_ref[:, lo:hi], preferred_element_type=F32)

    def proj_chunk(kind, c):
        lo, hi = c * MXU_COLS, (c + 1) * MXU_COLS
        if kind == "q":
            q_ref[:, lo:hi] = (mm(_OFF_Q + lo, _OFF_Q + hi)
                               * (LOG2E / math.sqrt(HEAD_DIM))).astype(BF16)
        elif kind == "k":
            k_ref[:, lo:hi] = mm(_OFF_K + lo, _OFF_K + hi).astype(BF16)
        elif kind == "v":
            v_ref[:, lo:hi] = mm(_OFF_V + lo, _OFF_V + hi).astype(BF16)
        elif kind == "g":
            gab_ref[:, lo:hi] = jnp.dot(h, wgate_ref[:, lo:hi], preferred_element_type=F32)
        elif kind == "xa":
            xa_s[:, lo:hi] = mm(_OFF_XA + lo, _OFF_XA + hi)
        elif kind == "ga":
            ga_in_s[:, lo:hi] = mm(_OFF_GA + lo, _OFF_GA + hi)
        else:
            f_ref[...] = jnp.dot(h, wf_ref[...], preferred_element_type=F32)

    pending = ([("q", c) for c in range(D_ATT // MXU_COLS)]
               + [("k", c) for c in range(D_ATT // MXU_COLS)]
               + [("v", c) for c in range(D_ATT // MXU_COLS)]
               + [("g", c) for c in range(2 * D_MODEL // MXU_COLS)] + [("f", 0)]
               + [("xa", c) for c in range(D_RNN // MXU_COLS)]
               + [("ga", c) for c in range(D_RNN // MXU_COLS)])

    def emit(n):
        for _ in range(min(n, len(pending))):
            proj_chunk(*pending.pop(0))

    for j in range(N_GATE_TILES):
        cols = slice(j * GATE_TILE, (j + 1) * GATE_TILE)
        xc = cb_ref[:, cols] + cw_ref[CONV_A - 1:CONV_A, cols] * xpad[SUBLANES:SUBLANES + tc, cols]
        for kk in range(CONV_A - 1):
            sh = CONV_A - 1 - kk
            xc = xc + cw_ref[kk:kk + 1, cols] * xpad[SUBLANES - sh:SUBLANES - sh + tc, cols]
        xc_s[:, cols] = xc
        xcb_s[:, cols] = xc.astype(BF16)
        emit(1)
    xpad[0:SUBLANES, :] = xpad[tc:tc + SUBLANES, :]

    sp = jax.nn.softplus(-lam_ref[...])
    lane_chunks = GATE_TILE // LANES
    for j in range(N_GATE_TILES):
        ks = GATE_STARTS[j]
        cols = slice(j * GATE_TILE, (j + 1) * GATE_TILE)
        g = jnp.dot(xcb_s[:, ks:ks + GATE_WIN], wg_ref[j, :, :2 * GATE_TILE],
                    preferred_element_type=F32)
        r = jax.nn.sigmoid(g[:, :GATE_TILE] + ba_ref[:, cols])
        i = jax.nn.sigmoid(g[:, GATE_TILE:] + bx_ref[:, cols])
        log_a = -LRU_C * r * sp[:, cols]
        a = jnp.exp(log_a)
        b = jnp.sqrt(-jnp.tanh(log_a) * (a * a + 1.0)) * (i * xc_s[:, cols])
        for c in range(lane_chunks):
            for s in range(SUBLANES):
                dst = slice(s * pitch, s * pitch + seg)
                a_s[j * lane_chunks + c, dst, :] = a[s * seg:(s + 1) * seg, c * LANES:(c + 1) * LANES]
                b_s[j * lane_chunks + c, dst, :] = b[s * seg:(s + 1) * seg, c * LANES:(c + 1) * LANES]
        emit(3)

    h8 = [jnp.zeros((SUBLANES, LANES), F32) for _ in range(n_lane)]
    p8 = [jnp.ones((SUBLANES, LANES), F32) for _ in range(n_lane)]
    for j in range(seg):
        rows = pl.ds(j, SUBLANES, stride=pitch)
        for c in range(n_lane):
            a_j = a_s[c, rows, :]
            h8[c] = a_j * h8[c] + b_s[c, rows, :]
            p8[c] = p8[c] * a_j
            h_s[c, rows, :] = h8[c]
            p_s[c, rows, :] = p8[c]
        if j % 4 == 3:
            emit(1)

    pa = jnp.zeros((tc, D_MODEL), F32)
    for c in range(n_lane):
        cols = slice(c * LANES, (c + 1) * LANES)
        start = hprev[:, cols]
        for s in range(SUBLANES):
            rows = slice(s * seg, (s + 1) * seg)
            src = slice(s * pitch, s * pitch + seg)
            hfin = h_s[c, src, :] + p_s[c, src, :] * start
            y_s[rows, cols] = (jax.nn.gelu(ga_s[rows, cols]) * hfin).astype(BF16)
            start = h8[c][s:s + 1, :] + p8[c][s:s + 1, :] * start
        hprev[:, cols] = start
        emit(1)
        if c % lane_chunks == lane_chunks - 1:
            slab = slice((c - lane_chunks + 1) * LANES, (c + 1) * LANES)
            pa = pa + jnp.dot(y_s[:, slab], wp_ref[slab, :D_MODEL], preferred_element_type=F32)
    emit(len(pending))
    pa_ref[...] = pa


def _mix_in(x2, gain, w_main, w_gate, w_f, cw, cb, wg, ba, bx, lam, wp, tc, seq):
    t = x2.shape[0]
    n_tiles = t // tc
    row = lambda n: pl.BlockSpec((tc, n), lambda i: (jnp.minimum(i, n_tiles - 1), 0))
    scan_shape = (D_RNN // LANES, SUBLANES * _scan_pitch(tc), LANES)
    consts = (gain, w_main, w_gate, w_f, cw, cb, wg, ba, bx, lam, wp)
    return pl.pallas_call(
        functools.partial(_mix_in_kernel, tc=tc, tiles_per_seq=seq // tc),
        grid=(n_tiles + 1,),
        in_specs=[row(D_MODEL)] + [_const_spec(c.shape) for c in consts],
        out_specs=[row(D_ATT), row(D_ATT), row(D_ATT), row(2 * D_MODEL), row(LANES),
                   pl.BlockSpec((tc, D_MODEL), lambda i: (jnp.maximum(i - 1, 0), 0))],
        out_shape=[jax.ShapeDtypeStruct((t, D_ATT), BF16), jax.ShapeDtypeStruct((t, D_ATT), BF16),
                   jax.ShapeDtypeStruct((t, D_ATT), BF16),
                   jax.ShapeDtypeStruct((t, 2 * D_MODEL), F32),
                   jax.ShapeDtypeStruct((t, LANES), F32),
                   jax.ShapeDtypeStruct((t, D_MODEL), F32)],
        scratch_shapes=[pltpu.VMEM((tc, D_RNN), F32), pltpu.VMEM((tc, D_RNN), F32),
                        pltpu.VMEM((tc + SUBLANES, D_RNN), F32), pltpu.VMEM((tc, D_RNN), F32),
                        pltpu.VMEM((tc, D_RNN), F32), pltpu.VMEM((tc, D_RNN), BF16),
                        pltpu.VMEM(scan_shape, F32), pltpu.VMEM(scan_shape, F32),
                        pltpu.VMEM(scan_shape, F32), pltpu.VMEM(scan_shape, F32),
                        pltpu.VMEM((tc, D_RNN), BF16), pltpu.VMEM((1, D_RNN), F32)],
        compiler_params=pltpu.CompilerParams(dimension_semantics=("arbitrary",),
                                             vmem_limit_bytes=VMEM_LIMIT),
        name="mix_in",
    )(x2, *consts)


def _fox_attn_kernel(q_ref, k_ref, v_ref, f_ref, bf_ref, o_ref, cum, cum_t, km_s, vt_s, ck_s,
                     cq_s, sc_s, p_s, m_s, off_s, o0_s, *, s, tq):
    pair = pl.program_id(1)

    @pl.when(pair == 0)
    def _():
        c = jax.nn.log_sigmoid(f_ref[0] + bf_ref[...])
        row = lax.broadcasted_iota(jnp.int32, (s, LANES), 0)
        d = 1
        while d < s:
            c = c + jnp.where(row >= d, pltpu.roll(c, d, axis=0), 0.0)
            d *= 2
        c = c * LOG2E
        cum[...] = c
        cum_t[...] = c.T

    lane = lax.broadcasted_iota(jnp.int32, (1, LANES), 1)
    causal = (lax.broadcasted_iota(jnp.int32, (tq, tq), 0)
              <= lax.broadcasted_iota(jnp.int32, (tq, tq), 1))
    halves = tq // LANES
    n_q = s // tq

    k = k_ref[0]
    v = v_ref[0]
    for hh in range(2):
        head = 2 * pair + hh
        in_head = (lane >= hh * HEAD_DIM) & (lane < (hh + 1) * HEAD_DIM)
        km_s[hh] = jnp.where(in_head, k, jnp.zeros_like(k))
        vt_s[hh] = jnp.where(in_head, v, jnp.ones_like(v)).astype(F32).T.astype(BF16)
        ck = jnp.sum(jnp.where(lane == head, cum[...], 0.0), axis=-1, keepdims=True)
        ck_s[hh] = jnp.broadcast_to(ck, (s, LANES))
        cq_s[hh] = cum_t[pl.ds(head, 1), :]

    tasks = [(hh, i) for hh in range(2) for i in range(n_q)]

    def score_tile(n, j):
        hh, i = tasks[n]
        slot = n % 2
        ks = slice(j * tq, (j + 1) * tq)
        sc = lax.dot_general(km_s[hh, ks, :], q_ref[0, i * tq:(i + 1) * tq, :],
                             (((1,), (1,)), ((), ())), preferred_element_type=F32)
        sc = jnp.concatenate([sc[:, c * LANES:(c + 1) * LANES] - ck_s[hh, ks, :]
                              for c in range(halves)], axis=1)
        if j == i:
            sc = jnp.where(causal, sc, -jnp.inf)
        sc_s[slot, ks, :] = sc
        part = sc[0:SUBLANES, :]
        for r in range(1, tq // SUBLANES):
            part = jnp.maximum(part, sc[r * SUBLANES:(r + 1) * SUBLANES, :])
        m_s[slot] = part if j == 0 else jnp.maximum(m_s[slot], part)

    def finish_scores(n):
        hh, i = tasks[n]
        slot = n % 2
        m = jnp.max(m_s[slot], axis=0, keepdims=True)
        cq_i = cq_s[hh, :, i * tq:(i + 1) * tq]
        off_s[slot] = (m + cq_i) - cq_i

    def exp_tile(n, j):
        slot = n % 2
        ks = slice(j * tq, (j + 1) * tq)
        p_s[slot, ks, :] = jnp.exp2(sc_s[slot, ks, :] - off_s[slot]).astype(BF16)

    def weighted_values(n):
        hh, i = tasks[n]
        slot = n % 2
        qs, qe = i * tq, (i + 1) * tq
        o = jnp.dot(vt_s[hh, :, 0:qe], p_s[slot, 0:qe, :], preferred_element_type=F32)
        sum_row = (1 - hh) * HEAD_DIM
        o = o / o[sum_row:sum_row + 1, :]
        if hh == 0:
            o0_s[i] = o
        else:
            row = lax.broadcasted_iota(jnp.int32, (LANES, 1), 0)
            o_ref[0, qs:qe, :] = jnp.where(row < HEAD_DIM, o0_s[i], o).T.astype(BF16)

    for j in range(tasks[0][1] + 1):
        score_tile(0, j)
    finish_scores(0)
    for n in range(len(tasks)):
        n_exp = tasks[n][1] + 1
        n_score = tasks[n + 1][1] + 1 if n + 1 < len(tasks) else 0
        for j in range(max(n_exp, n_score)):
            if j < n_score:
                score_tile(n + 1, j)
            if j < n_exp:
                exp_tile(n, j)
        if n_score:
            finish_scores(n + 1)
        weighted_values(n)


def _fox_attn(q3, k3, v3, f3, bf, tq):
    b, s, _ = q3.shape
    blk = pl.BlockSpec((1, s, LANES), lambda bi, pi: (bi, 0, pi))
    return pl.pallas_call(
        functools.partial(_fox_attn_kernel, s=s, tq=tq),
        grid=(b, H_ATT // 2),
        in_specs=[blk, blk, blk, pl.BlockSpec((1, s, LANES), lambda bi, pi: (bi, 0, 0)),
                  _const_spec((1, LANES))],
        out_specs=blk,
        out_shape=jax.ShapeDtypeStruct((b, s, D_ATT), BF16),
        scratch_shapes=[pltpu.VMEM((s, LANES), F32), pltpu.VMEM((LANES, s), F32),
                        pltpu.VMEM((2, s, LANES), BF16), pltpu.VMEM((2, LANES, s), BF16),
                        pltpu.VMEM((2, s, LANES), F32), pltpu.VMEM((2, 1, s), F32),
                        pltpu.VMEM((2, s, tq), F32), pltpu.VMEM((2, s, tq), BF16),
                        pltpu.VMEM((2, SUBLANES, tq), F32), pltpu.VMEM((2, 1, tq), F32),
                        pltpu.VMEM((s // tq, LANES, tq), F32)],
        compiler_params=pltpu.CompilerParams(dimension_semantics=("arbitrary", "arbitrary"),
                                             vmem_limit_bytes=VMEM_LIMIT),
        name="fox_attn",
    )(q3, k3, v3, f3, bf)


FF_CHUNK = 512


def _merge_ffn_kernel(x_ref, pa_ref, yb_ref, gab_ref, bm_ref, wpb_ref, wout_ref, gpost_ref,
                      gpre2_ref, gpost2_ref, perm_ref, wup_ref, cw_ref, cb_ref, wdn_ref, o_ref,
                      tail, unperm, x1_s, hp_s, *, tm, tiles_per_seq):
    i = pl.program_id(0)
    seg = tm // SUBLANES
    pitch = _scan_pitch(tm)

    @pl.when(i == 0)
    def _():
        x1_s[...] = jnp.zeros_like(x1_s)
        hp_s[...] = jnp.zeros_like(hp_s)
        tail[...] = jnp.zeros_like(tail)

    @pl.when((i + tiles_per_seq - 1) % tiles_per_seq == 0)
    def _():
        tail[...] = jnp.zeros_like(tail)

    x1_prev = x1_s[...]
    hp = hp_s[...]

    def merge_stage(k, st):
        if k == 0:
            st["gates"] = jax.nn.sigmoid(gab_ref[...] + bm_ref[...])
        elif k == 1:
            st["pb"] = jnp.dot(yb_ref[...], wpb_ref[:, :D_MODEL], preferred_element_type=F32)
        elif k == 2:
            g = st.pop("gates")
            st["merged"] = (g[:, :D_MODEL] * pa_ref[...] + g[:, D_MODEL:] * st.pop("pb")).astype(BF16)
        elif k == 3:
            st["mix"] = jnp.dot(st.pop("merged"), wout_ref[:, :D_MODEL], preferred_element_type=F32)
        elif k == 4:
            x1_s[...] = x_ref[...] + _rms(st.pop("mix"), gpost_ref[...])
        elif k == 5:
            h = _rms(x1_s[...], gpre2_ref[...]).astype(BF16)
            hp_s[...] = jnp.dot(perm_ref[...], h, preferred_element_type=F32).astype(BF16)

    first = lax.broadcasted_iota(jnp.int32, (SUBLANES, FF_CHUNK), 0) == 0

    def up_proj(c):
        return [jnp.dot(hp, wup_ref[:, off + c * FF_CHUNK:off + (c + 1) * FF_CHUNK],
                        preferred_element_type=F32) for off in (0, D_FF)]

    def conv(u, cols):
        prev = [jnp.where(first, pltpu.roll(tail[g * SUBLANES:(g + 1) * SUBLANES, cols], 1, axis=0),
                          pltpu.roll(u[tm - (2 - g) * SUBLANES:tm - (1 - g) * SUBLANES, :], 1, axis=0))
                for g in range(CONV_F - 1)]
        tail[:, cols] = u[tm - (CONV_F - 1) * SUBLANES:, :]
        u1 = jnp.concatenate([prev[1], u[:tm - SUBLANES, :]], axis=0)
        u2 = jnp.concatenate([prev[0], prev[1], u[:tm - 2 * SUBLANES, :]], axis=0)
        return (cb_ref[:, cols] + cw_ref[2:3, cols] * u + cw_ref[1:2, cols] * u1
                + cw_ref[0:1, cols] * u2)

    n_chunks = D_FF // FF_CHUNK
    stages = {}
    acc = jnp.zeros((tm, D_MODEL), F32)
    u_next = up_proj(0)
    for c in range(n_chunks):
        u_gate, u_val = u_next
        if c + 1 < n_chunks:
            u_next = up_proj(c + 1)
        merge_stage(c, stages)
        gate = conv(u_gate, slice(c * FF_CHUNK, (c + 1) * FF_CHUNK))
        val = conv(u_val, slice(D_FF + c * FF_CHUNK, D_FF + (c + 1) * FF_CHUNK))
        act = (jax.nn.gelu(gate) * val).astype(BF16)
        acc = acc + jnp.dot(act, wdn_ref[c * FF_CHUNK:(c + 1) * FF_CHUNK, :D_MODEL],
                            preferred_element_type=F32)

    for j in range(seg):
        for c in range(D_MODEL // LANES):
            unperm[c, pl.ds(j, SUBLANES, stride=pitch), :] = (
                acc[j * SUBLANES:(j + 1) * SUBLANES, c * LANES:(c + 1) * LANES])
    ffn = jnp.concatenate(
        [jnp.concatenate([unperm[c, s * pitch:s * pitch + seg, :] for s in range(SUBLANES)], axis=0)
         for c in range(D_MODEL // LANES)], axis=1)
    o_ref[...] = x1_prev + _rms(ffn, gpost2_ref[...])


def _merge_ffn(x2, pa2, yb2, gab2, bm, wpb, wout, gpost, gpre2, gpost2, wup, cw, cb, wdn,
               tm, seq):
    t = x2.shape[0]
    n_tiles = t // tm
    row = lambda n: pl.BlockSpec((tm, n), lambda i: (jnp.minimum(i, n_tiles - 1), 0))
    r = jnp.arange(tm)
    perm = (r[None, :] == ((r % SUBLANES) * (tm // SUBLANES) + r // SUBLANES)[:, None]).astype(BF16)
    return pl.pallas_call(
        functools.partial(_merge_ffn_kernel, tm=tm, tiles_per_seq=seq // tm),
        grid=(n_tiles + 1,),
        in_specs=[row(D_MODEL), row(D_MODEL), row(D_ATT), row(2 * D_MODEL),
                  _const_spec((1, 2 * D_MODEL)), _const_spec((D_ATT, _padded(D_MODEL))),
                  _const_spec((D_MODEL, _padded(D_MODEL))), _const_spec((1, D_MODEL)),
                  _const_spec((1, D_MODEL)), _const_spec((1, D_MODEL)), _const_spec((tm, tm)),
                  _const_spec((D_MODEL, _padded(2 * D_FF))), _const_spec((CONV_F, 2 * D_FF)),
                  _const_spec((1, 2 * D_FF)), _const_spec((D_FF, _padded(D_MODEL)))],
        out_specs=pl.BlockSpec((tm, D_MODEL), lambda i: (jnp.maximum(i - 1, 0), 0)),
        out_shape=jax.ShapeDtypeStruct((t, D_MODEL), F32),
        scratch_shapes=[pltpu.VMEM(((CONV_F - 1) * SUBLANES, 2 * D_FF), F32),
                        pltpu.VMEM((D_MODEL // LANES, SUBLANES * _scan_pitch(tm), LANES), F32),
                        pltpu.VMEM((tm, D_MODEL), F32), pltpu.VMEM((tm, D_MODEL), BF16)],
        compiler_params=pltpu.CompilerParams(dimension_semantics=("arbitrary",),
                                             vmem_limit_bytes=VMEM_LIMIT),
        name="merge_ffn",
    )(x2, pa2, yb2, gab2, bm, wpb, wout, gpost, gpre2, gpost2, perm, wup, cw, cb, wdn)


def _block_diag(w):
    h, n, _ = w.shape
    col = jnp.arange(h * n)
    tile = (col[None, :] % n == jnp.arange(n)[:, None]).astype(w.dtype)
    rep = jnp.dot(w.reshape(h * n, n), tile, precision=lax.Precision.HIGHEST)
    return jnp.where(col[:, None] // n == col[None, :] // n, rep, 0.0)


def _gate_windows(w_a, w_x):
    da, dx = _block_diag(w_a), _block_diag(w_x)
    tiles = []
    for j, ks in enumerate(GATE_STARTS):
        cols = slice(j * GATE_TILE, (j + 1) * GATE_TILE)
        tiles.append(jnp.concatenate([da[ks:ks + GATE_WIN, cols], dx[ks:ks + GATE_WIN, cols]],
                                     axis=1))
    return jnp.stack([_bf16_weight(t) for t in tiles])


def _layer(x, mix_norm_pre, mix_norm_post, w_in, conv_a_w, conv_a_b, w_rg_a, b_rg_a, w_rg_x,
           b_rg_x, lru_lambda, b_forget, b_merge, w_proj_a, w_proj_b, w_out, ffn_norm_pre,
           ffn_norm_post, w_up, conv_f_w, conv_f_b, w_down):
    bsz, seq, _ = x.shape
    t = bsz * seq
    row = lambda a: a.reshape(1, -1)

    w_main = _bf16_weight(w_in[:, :_OFF_F])
    w_gate = _bf16_weight(w_in[:, _OFF_G:])
    w_f = jnp.pad(w_in[:, _OFF_F:_OFF_G].astype(BF16), ((0, 0), (0, LANES - H_ATT)))
    bf = jnp.concatenate([b_forget, jnp.zeros((LANES - H_ATT,), b_forget.dtype)]).reshape(1, LANES)
    wg = _gate_windows(w_rg_a, w_rg_x)

    x2 = x.reshape(t, D_MODEL)
    q, k, v, gab, f, pa = _mix_in(x2, row(mix_norm_pre), w_main, w_gate, w_f, conv_a_w,
                                  row(conv_a_b), wg, row(b_rg_a), row(b_rg_x), row(lru_lambda),
                                  _bf16_weight(w_proj_a), tc=MIX_IN_ROWS, seq=seq)

    r3 = lambda a: a.reshape(bsz, seq, a.shape[-1])
    yb = _fox_attn(r3(q), r3(k), r3(v), r3(f), bf, tq=ATTN_Q_ROWS)

    out = _merge_ffn(x2, pa, yb.reshape(t, D_ATT), gab, row(b_merge),
                     _bf16_weight(w_proj_b), _bf16_weight(w_out), row(mix_norm_post),
                     row(ffn_norm_pre), row(ffn_norm_post), _bf16_weight(w_up), conv_f_w,
                     row(conv_f_b), _bf16_weight(w_down), tm=FFN_ROWS, seq=seq)
    return out.reshape(bsz, seq, D_MODEL)


def kernel(x, mix_norm_pre, mix_norm_post, w_in, conv_a_w, conv_a_b, w_rg_a, b_rg_a, w_rg_x, b_rg_x, lru_lambda, b_forget, b_merge, w_proj_a, w_proj_b, w_out, ffn_norm_pre, ffn_norm_post, w_up, conv_f_w, conv_f_b, w_down):
    depth = w_in.shape[0]
    for layer in range(depth):
        x = _layer(x, mix_norm_pre[layer], mix_norm_post[layer], w_in[layer], conv_a_w[layer],
                   conv_a_b[layer], w_rg_a[layer], b_rg_a[layer], w_rg_x[layer], b_rg_x[layer],
                   lru_lambda[layer], b_forget[layer], b_merge[layer], w_proj_a[layer],
                   w_proj_b[layer], w_out[layer], ffn_norm_pre[layer], ffn_norm_post[layer],
                   w_up[layer], conv_f_w[layer], conv_f_b[layer], w_down[layer])
    return x
```

```python
import functools
import math

import jax
import jax.numpy as jnp
from jax import lax
from jax.experimental import pallas as pl
from jax.experimental.pallas import tpu as pltpu

D_MODEL = 1024
D_RNN = 1280
H_RNN = 16
RNN_BLOCK = D_RNN // H_RNN
CONV_A = 4
LRU_C = 8.0
H_ATT = 16
HEAD_DIM = 64
D_ATT = H_ATT * HEAD_DIM
D_FF = 3 * D_MODEL
CONV_F = 3
RMS_EPS = 1e-6

LANES = 128
SUBLANES = 8
MXU_COLS = 256
VMEM_LIMIT = 56 * 1024 * 1024

MIX_IN_ROWS = 256
ATTN_Q_ROWS = 256
FFN_ROWS = 256

LOG2E = math.log2(math.e)

F32 = jnp.float32
BF16 = jnp.bfloat16

_OFF_XA = 0
_OFF_GA = _OFF_XA + D_RNN
_OFF_Q = _OFF_GA + D_RNN
_OFF_K = _OFF_Q + D_ATT
_OFF_V = _OFF_K + D_ATT
_OFF_F = _OFF_V + D_ATT
_OFF_G = _OFF_F + H_ATT

GATE_TILE = MXU_COLS
GATE_WIN = 2 * MXU_COLS
N_GATE_TILES = D_RNN // GATE_TILE


def _gate_window_start(j):
    lo = (j * GATE_TILE // RNN_BLOCK) * RNN_BLOCK
    start = min((lo // LANES) * LANES, D_RNN - GATE_WIN)
    hi = -(-((j + 1) * GATE_TILE) // RNN_BLOCK) * RNN_BLOCK
    assert start <= lo and hi <= start + GATE_WIN
    return start


GATE_STARTS = tuple(_gate_window_start(j) for j in range(N_GATE_TILES))


def _rms(x, gain):
    y = x * lax.rsqrt(jnp.mean(x * x, axis=-1, keepdims=True) + RMS_EPS)
    return y * gain


def _padded(n):
    return n + LANES if (n // 32) % 8 == 0 else n


def _bf16_weight(w):
    k, n = w.shape
    return jnp.pad(w, ((0, 0), (0, _padded(n) - n))).astype(BF16)


def _const_spec(shape):
    nd = len(shape)
    return pl.BlockSpec(shape, lambda *_: (0,) * nd, pipeline_mode=pl.Buffered(1))


def _scan_pitch(tc):
    seg = tc // SUBLANES
    return seg + SUBLANES if (seg // SUBLANES) % 2 == 0 else seg


def _mix_in_kernel(x_ref, g_ref, w_ref, wgate_ref, wf_ref, cw_ref, cb_ref, wg_ref, ba_ref, bx_ref,
                   lam_ref, wp_ref, q_ref, k_ref, v_ref, gab_ref, f_ref, pa_ref,
                   xa_s, ga_in_s, xpad, ga_s, xc_s, xcb_s, a_s, b_s, h_s, p_s, y_s, hprev,
                   *, tc, tiles_per_seq):
    seg = tc // SUBLANES
    pitch = _scan_pitch(tc)
    n_lane = D_RNN // LANES
    step = pl.program_id(0)

    @pl.when(step == 0)
    def _():
        xa_s[...] = jnp.zeros_like(xa_s)
        ga_in_s[...] = jnp.zeros_like(ga_in_s)
        xpad[0:SUBLANES, :] = jnp.zeros((SUBLANES, D_RNN), F32)
        hprev[...] = jnp.zeros_like(hprev)

    @pl.when((step + tiles_per_seq - 1) % tiles_per_seq == 0)
    def _():
        xpad[0:SUBLANES, :] = jnp.zeros((SUBLANES, D_RNN), F32)
        hprev[...] = jnp.zeros_like(hprev)

    xpad[SUBLANES:SUBLANES + tc, :] = xa_s[...]
    ga_s[...] = ga_in_s[...]

    h = _rms(x_ref[...], g_ref[...]).astype(BF16)

    def mm(lo, hi):
        return jnp.dot(h, w_ref[:, lo:hi], preferred_element_type=F32)

    def proj_chunk(kind, c):
        lo, hi = c * MXU_COLS, (c + 1) * MXU_COLS
        if kind == "q":
            q_ref[:, lo:hi] = (mm(_OFF_Q + lo, _OFF_Q + hi)
                               * (LOG2E / math.sqrt(HEAD_DIM))).astype(BF16)
        elif kind == "k":
            k_ref[:, lo:hi] = mm(_OFF_K + lo, _OFF_K + hi).astype(BF16)
        elif kind == "v":
            v_ref[:, lo:hi] = mm(_OFF_V + lo, _OFF_V + hi).astype(BF16)
        elif kind == "g":
            gab_ref[:, lo:hi] = jnp.dot(h, wgate_ref[:, lo:hi], preferred_element_type=F32)
        elif kind == "xa":
            xa_s[:, lo:hi] = mm(_OFF_XA + lo, _OFF_XA + hi)
        elif kind == "ga":
            ga_in_s[:, lo:hi] = mm(_OFF_GA + lo, _OFF_GA + hi)
        else:
            f_ref[...] = jnp.dot(h, wf_ref[...], preferred_element_type=F32)

    pending = ([("q", c) for c in range(D_ATT // MXU_COLS)]
               + [("k", c) for c in range(D_ATT // MXU_COLS)]
               + [("v", c) for c in range(D_ATT // MXU_COLS)]
               + [("g", c) for c in range(2 * D_MODEL // MXU_COLS)] + [("f", 0)]
               + [("xa", c) for c in range(D_RNN // MXU_COLS)]
               + [("ga", c) for c in range(D_RNN // MXU_COLS)])

    def emit(n):
        for _ in range(min(n, len(pending))):
            proj_chunk(*pending.pop(0))

    for j in range(N_GATE_TILES):
        cols = slice(j * GATE_TILE, (j + 1) * GATE_TILE)
        xc = cb_ref[:, cols] + cw_ref[CONV_A - 1:CONV_A, cols] * xpad[SUBLANES:SUBLANES + tc, cols]
        for kk in range(CONV_A - 1):
            sh = CONV_A - 1 - kk
            xc = xc + cw_ref[kk:kk + 1, cols] * xpad[SUBLANES - sh:SUBLANES - sh + tc, cols]
        xc_s[:, cols] = xc
        xcb_s[:, cols] = xc.astype(BF16)
        emit(1)
    xpad[0:SUBLANES, :] = xpad[tc:tc + SUBLANES, :]

    sp = jax.nn.softplus(-lam_ref[...])
    lane_chunks = GATE_TILE // LANES
    for j in range(N_GATE_TILES):
        ks = GATE_STARTS[j]
        cols = slice(j * GATE_TILE, (j + 1) * GATE_TILE)
        g = jnp.dot(xcb_s[:, ks:ks + GATE_WIN], wg_ref[j, :, :2 * GATE_TILE],
                    preferred_element_type=F32)
        r = jax.nn.sigmoid(g[:, :GATE_TILE] + ba_ref[:, cols])
        i = jax.nn.sigmoid(g[:, GATE_TILE:] + bx_ref[:, cols])
        log_a = -LRU_C * r * sp[:, cols]
        a = jnp.exp(log_a)
        b = jnp.sqrt(-jnp.tanh(log_a) * (a * a + 1.0)) * (i * xc_s[:, cols])
        for c in range(lane_chunks):
            for s in range(SUBLANES):
                dst = slice(s * pitch, s * pitch + seg)
                a_s[j * lane_chunks + c, dst, :] = a[s * seg:(s + 1) * seg, c * LANES:(c + 1) * LANES]
                b_s[j * lane_chunks + c, dst, :] = b[s * seg:(s + 1) * seg, c * LANES:(c + 1) * LANES]
        emit(3)

    h8 = [jnp.zeros((SUBLANES, LANES), F32) for _ in range(n_lane)]
    p8 = [jnp.ones((SUBLANES, LANES), F32) for _ in range(n_lane)]
    for j in range(seg):
        rows = pl.ds(j, SUBLANES, stride=pitch)
        for c in range(n_lane):
            a_j = a_s[c, rows, :]
            h8[c] = a_j * h8[c] + b_s[c, rows, :]
            p8[c] = p8[c] * a_j
            h_s[c, rows, :] = h8[c]
            p_s[c, rows, :] = p8[c]
        if j % 4 == 3:
            emit(1)

    pa = jnp.zeros((tc, D_MODEL), F32)
    for c in range(n_lane):
        cols = slice(c * LANES, (c + 1) * LANES)
        start = hprev[:, cols]
        for s in range(SUBLANES):
            rows = slice(s * seg, (s + 1) * seg)
            src = slice(s * pitch, s * pitch + seg)
            hfin = h_s[c, src, :] + p_s[c, src, :] * start
            y_s[rows, cols] = (jax.nn.gelu(ga_s[rows, cols]) * hfin).astype(BF16)
            start = h8[c][s:s + 1, :] + p8[c][s:s + 1, :] * start
        hprev[:, cols] = start
        emit(1)
        if c % lane_chunks == lane_chunks - 1:
            slab = slice((c - lane_chunks + 1) * LANES, (c + 1) * LANES)
            pa = pa + jnp.dot(y_s[:, slab], wp_ref[slab, :D_MODEL], preferred_element_type=F32)
    emit(len(pending))
    pa_ref[...] = pa


def _mix_in(x2, gain, w_main, w_gate, w_f, cw, cb, wg, ba, bx, lam, wp, tc, seq):
    t = x2.shape[0]
    n_tiles = t // tc
    row = lambda n: pl.BlockSpec((tc, n), lambda i: (jnp.minimum(i, n_tiles - 1), 0))
    scan_shape = (D_RNN // LANES, SUBLANES * _scan_pitch(tc), LANES)
    consts = (gain, w_main, w_gate, w_f, cw, cb, wg, ba, bx, lam, wp)
    return pl.pallas_call(
        functools.partial(_mix_in_kernel, tc=tc, tiles_per_seq=seq // tc),
        grid=(n_tiles + 1,),
        in_specs=[row(D_MODEL)] + [_const_spec(c.shape) for c in consts],
        out_specs=[row(D_ATT), row(D_ATT), row(D_ATT), row(2 * D_MODEL), row(LANES),
                   pl.BlockSpec((tc, D_MODEL), lambda i: (jnp.maximum(i - 1, 0), 0))],
        out_shape=[jax.ShapeDtypeStruct((t, D_ATT), BF16), jax.ShapeDtypeStruct((t, D_ATT), BF16),
                   jax.ShapeDtypeStruct((t, D_ATT), BF16),
                   jax.ShapeDtypeStruct((t, 2 * D_MODEL), F32),
                   jax.ShapeDtypeStruct((t, LANES), F32),
                   jax.ShapeDtypeStruct((t, D_MODEL), F32)],
        scratch_shapes=[pltpu.VMEM((tc, D_RNN), F32), pltpu.VMEM((tc, D_RNN), F32),
                        pltpu.VMEM((tc + SUBLANES, D_RNN), F32), pltpu.VMEM((tc, D_RNN), F32),
                        pltpu.VMEM((tc, D_RNN), F32), pltpu.VMEM((tc, D_RNN), BF16),
                        pltpu.VMEM(scan_shape, F32), pltpu.VMEM(scan_shape, F32),
                        pltpu.VMEM(scan_shape, F32), pltpu.VMEM(scan_shape, F32),
                        pltpu.VMEM((tc, D_RNN), BF16), pltpu.VMEM((1, D_RNN), F32)],
        compiler_params=pltpu.CompilerParams(dimension_semantics=("arbitrary",),
                                             vmem_limit_bytes=VMEM_LIMIT),
        name="mix_in",
    )(x2, *consts)


def _fox_attn_kernel(q_ref, k_ref, v_ref, f_ref, bf_ref, o_ref, cum, cum_t, km_s, vt_s, ck_s,
                     cq_s, sc_s, p_s, m_s, off_s, o0_s, *, s, tq):
    pair = pl.program_id(1)

    @pl.when(pair == 0)
    def _():
        c = jax.nn.log_sigmoid(f_ref[0] + bf_ref[...])
        row = lax.broadcasted_iota(jnp.int32, (s, LANES), 0)
        d = 1
        while d < s:
            c = c + jnp.where(row >= d, pltpu.roll(c, d, axis=0), 0.0)
            d *= 2
        c = c * LOG2E
        cum[...] = c
        cum_t[...] = c.T

    lane = lax.broadcasted_iota(jnp.int32, (1, LANES), 1)
    n_q = s // tq

    k = k_ref[0]
    v_t = v_ref[0].astype(F32).T
    v_row = lax.broadcasted_iota(jnp.int32, (LANES, 1), 0)
    for hh in range(2):
        head = 2 * pair + hh
        in_head = (lane >= hh * HEAD_DIM) & (lane < (hh + 1) * HEAD_DIM)
        km_s[hh] = jnp.where(in_head, k, jnp.zeros_like(k))
        own_rows = (v_row >= hh * HEAD_DIM) & (v_row < (hh + 1) * HEAD_DIM)
        vt_s[hh] = jnp.where(own_rows, v_t, 1.0).astype(BF16)
        ck = jnp.sum(jnp.where(lane == head, cum[...], 0.0), axis=-1, keepdims=True)
        ck_s[hh] = jnp.broadcast_to(ck, (s, LANES))
        cq_s[hh] = cum_t[pl.ds(head, 1), :]

    tasks = [(hh, i) for hh in range(2) for i in range(n_q)]

    def pieces(i, j):
        if j == i:
            return [(0, tq // 2, 0, tq), (tq // 2, tq, tq // 2, tq)]
        return [(0, tq, 0, tq)]

    def score_tile(n, j):
        hh, i = tasks[n]
        slot = n % 2
        ks = slice(j * tq, (j + 1) * tq)
        sc = lax.dot_general(km_s[hh, ks, :], q_ref[0, i * tq:(i + 1) * tq, :],
                             (((1,), (1,)), ((), ())), preferred_element_type=F32)
        for n_piece, (r0, r1, l0, l1) in enumerate(pieces(i, j)):
            rows = slice(j * tq + r0, j * tq + r1)
            blk = jnp.concatenate([sc[r0:r1, c * LANES:(c + 1) * LANES] - ck_s[hh, rows, :]
                                   for c in range(l0 // LANES, l1 // LANES)], axis=1)
            if j == i:
                key = lax.broadcasted_iota(jnp.int32, blk.shape, 0) + r0
                qry = lax.broadcasted_iota(jnp.int32, blk.shape, 1) + l0
                blk = jnp.where(key <= qry, blk, -jnp.inf)
            sc_s[slot, rows, l0:l1] = blk
            part = blk[0:SUBLANES, :]
            for r in range(1, (r1 - r0) // SUBLANES):
                part = jnp.maximum(part, blk[r * SUBLANES:(r + 1) * SUBLANES, :])
            if j == 0 and n_piece == 0:
                m_s[slot] = part
            else:
                m_s[slot, :, l0:l1] = jnp.maximum(m_s[slot, :, l0:l1], part)

    def finish_scores(n):
        hh, i = tasks[n]
        slot = n % 2
        m = jnp.max(m_s[slot], axis=0, keepdims=True)
        cq_i = cq_s[hh, :, i * tq:(i + 1) * tq]
        off_s[slot] = (m + cq_i) - cq_i

    def exp_tile(n, j):
        slot = n % 2
        i = tasks[n][1]
        for r0, r1, l0, l1 in pieces(i, j):
            rows = slice(j * tq + r0, j * tq + r1)
            p_s[slot, rows, l0:l1] = jnp.exp2(sc_s[slot, rows, l0:l1]
                                              - off_s[slot, :, l0:l1]).astype(BF16)
        if j == i:
            p_s[slot, j * tq + tq // 2:(j + 1) * tq, 0:tq // 2] = jnp.zeros((tq // 2, tq // 2), BF16)

    def weighted_values(n):
        hh, i = tasks[n]
        slot = n % 2
        qs, qe = i * tq, (i + 1) * tq
        o = jnp.dot(vt_s[hh, :, 0:qe], p_s[slot, 0:qe, :], preferred_element_type=F32)
        sum_row = (1 - hh) * HEAD_DIM
        o = o / o[sum_row:sum_row + 1, :]
        if hh == 0:
            o0_s[i] = o
        else:
            row = lax.broadcasted_iota(jnp.int32, (LANES, 1), 0)
            o_ref[0, qs:qe, :] = jnp.where(row < HEAD_DIM, o0_s[i], o).T.astype(BF16)

    for j in range(tasks[0][1] + 1):
        score_tile(0, j)
    finish_scores(0)
    for n in range(len(tasks)):
        n_exp = tasks[n][1] + 1
        n_score = tasks[n + 1][1] + 1 if n + 1 < len(tasks) else 0
        for j in range(max(n_exp, n_score)):
            if j < n_score:
                score_tile(n + 1, j)
            if j < n_exp:
                exp_tile(n, j)
        if n_score:
            finish_scores(n + 1)
        weighted_values(n)


def _fox_attn(q3, k3, v3, f3, bf, tq):
    b, s, _ = q3.shape
    blk = pl.BlockSpec((1, s, LANES), lambda bi, pi: (bi, 0, pi))
    return pl.pallas_call(
        functools.partial(_fox_attn_kernel, s=s, tq=tq),
        grid=(b, H_ATT // 2),
        in_specs=[blk, blk, blk, pl.BlockSpec((1, s, LANES), lambda bi, pi: (bi, 0, 0)),
                  _const_spec((1, LANES))],
        out_specs=blk,
        out_shape=jax.ShapeDtypeStruct((b, s, D_ATT), BF16),
        scratch_shapes=[pltpu.VMEM((s, LANES), F32), pltpu.VMEM((LANES, s), F32),
                        pltpu.VMEM((2, s, LANES), BF16), pltpu.VMEM((2, LANES, s), BF16),
                        pltpu.VMEM((2, s, LANES), F32), pltpu.VMEM((2, 1, s), F32),
                        pltpu.VMEM((2, s, tq), F32), pltpu.VMEM((2, s, tq), BF16),
                        pltpu.VMEM((2, SUBLANES, tq), F32), pltpu.VMEM((2, 1, tq), F32),
                        pltpu.VMEM((s // tq, LANES, tq), F32)],
        compiler_params=pltpu.CompilerParams(dimension_semantics=("arbitrary", "arbitrary"),
                                             vmem_limit_bytes=VMEM_LIMIT),
        name="fox_attn",
    )(q3, k3, v3, f3, bf)


FF_CHUNK = 512
N_MERGE_STAGES = 6


def _merge_ffn_kernel(x_ref, pa_ref, yb_ref, gab_ref, bm_ref, wpb_ref, wout_ref, gpost_ref,
                      gpre2_ref, gpost2_ref, perm_ref, wup_ref, cw_ref, cb_ref, wdn_ref, o_ref,
                      tail, unperm, x1_s, hp_s, *, tm, tiles_per_seq):
    i = pl.program_id(0)
    seg = tm // SUBLANES
    pitch = _scan_pitch(tm)

    @pl.when(i == 0)
    def _():
        x1_s[...] = jnp.zeros_like(x1_s)
        hp_s[...] = jnp.zeros_like(hp_s)
        tail[...] = jnp.zeros_like(tail)

    @pl.when((i + tiles_per_seq - 1) % tiles_per_seq == 0)
    def _():
        tail[...] = jnp.zeros_like(tail)

    x1_prev = x1_s[...]
    hp = hp_s[...]

    def merge_stage(k, st):
        if k == 0:
            st["gates"] = jax.nn.sigmoid(gab_ref[...] + bm_ref[...])
        elif k == 1:
            st["pb"] = jnp.dot(yb_ref[...], wpb_ref[:, :D_MODEL], preferred_element_type=F32)
        elif k == 2:
            g = st.pop("gates")
            st["merged"] = (g[:, :D_MODEL] * pa_ref[...] + g[:, D_MODEL:] * st.pop("pb")).astype(BF16)
        elif k == 3:
            st["mix"] = jnp.dot(st.pop("merged"), wout_ref[:, :D_MODEL], preferred_element_type=F32)
        elif k == 4:
            x1_s[...] = x_ref[...] + _rms(st.pop("mix"), gpost_ref[...])
        elif k == 5:
            h = _rms(x1_s[...], gpre2_ref[...]).astype(BF16)
            hp_s[...] = jnp.dot(perm_ref[...], h, preferred_element_type=F32).astype(BF16)

    first = lax.broadcasted_iota(jnp.int32, (SUBLANES, FF_CHUNK), 0) == 0

    def up_proj(c):
        return [jnp.dot(hp, wup_ref[:, off + c * FF_CHUNK:off + (c + 1) * FF_CHUNK],
                        preferred_element_type=F32) for off in (0, D_FF)]

    def conv(u, cols):
        prev = [jnp.where(first, pltpu.roll(tail[g * SUBLANES:(g + 1) * SUBLANES, cols], 1, axis=0),
                          pltpu.roll(u[tm - (2 - g) * SUBLANES:tm - (1 - g) * SUBLANES, :], 1, axis=0))
                for g in range(CONV_F - 1)]
        tail[:, cols] = u[tm - (CONV_F - 1) * SUBLANES:, :]
        u1 = jnp.concatenate([prev[1], u[:tm - SUBLANES, :]], axis=0)
        u2 = jnp.concatenate([prev[0], prev[1], u[:tm - 2 * SUBLANES, :]], axis=0)
        return (cb_ref[:, cols] + cw_ref[2:3, cols] * u + cw_ref[1:2, cols] * u1
                + cw_ref[0:1, cols] * u2)

    n_chunks = D_FF // FF_CHUNK
    stages = {}
    acc = jnp.zeros((tm, D_MODEL), F32)
    u_next = up_proj(0)
    for c in range(n_chunks):
        u_gate, u_val = u_next
        if c + 1 < n_chunks:
            u_next = up_proj(c + 1)
        for k in range(c * N_MERGE_STAGES // n_chunks, (c + 1) * N_MERGE_STAGES // n_chunks):
            merge_stage(k, stages)
        gate = conv(u_gate, slice(c * FF_CHUNK, (c + 1) * FF_CHUNK))
        val = conv(u_val, slice(D_FF + c * FF_CHUNK, D_FF + (c + 1) * FF_CHUNK))
        act = (jax.nn.gelu(gate) * val).astype(BF16)
        acc = acc + jnp.dot(act, wdn_ref[c * FF_CHUNK:(c + 1) * FF_CHUNK, :D_MODEL],
                            preferred_element_type=F32)

    for j in range(seg):
        for c in range(D_MODEL // LANES):
            unperm[c, pl.ds(j, SUBLANES, stride=pitch), :] = (
                acc[j * SUBLANES:(j + 1) * SUBLANES, c * LANES:(c + 1) * LANES])
    ffn = jnp.concatenate(
        [jnp.concatenate([unperm[c, s * pitch:s * pitch + seg, :] for s in range(SUBLANES)], axis=0)
         for c in range(D_MODEL // LANES)], axis=1)
    o_ref[...] = x1_prev + _rms(ffn, gpost2_ref[...])


def _merge_ffn(x2, pa2, yb2, gab2, bm, wpb, wout, gpost, gpre2, gpost2, wup, cw, cb, wdn,
               tm, seq):
    t = x2.shape[0]
    n_tiles = t // tm
    row = lambda n: pl.BlockSpec((tm, n), lambda i: (jnp.minimum(i, n_tiles - 1), 0))
    r = jnp.arange(tm)
    perm = (r[None, :] == ((r % SUBLANES) * (tm // SUBLANES) + r // SUBLANES)[:, None]).astype(BF16)
    return pl.pallas_call(
        functools.partial(_merge_ffn_kernel, tm=tm, tiles_per_seq=seq // tm),
        grid=(n_tiles + 1,),
        in_specs=[row(D_MODEL), row(D_MODEL), row(D_ATT), row(2 * D_MODEL),
                  _const_spec((1, 2 * D_MODEL)), _const_spec((D_ATT, _padded(D_MODEL))),
                  _const_spec((D_MODEL, _padded(D_MODEL))), _const_spec((1, D_MODEL)),
                  _const_spec((1, D_MODEL)), _const_spec((1, D_MODEL)), _const_spec((tm, tm)),
                  _const_spec((D_MODEL, _padded(2 * D_FF))), _const_spec((CONV_F, 2 * D_FF)),
                  _const_spec((1, 2 * D_FF)), _const_spec((D_FF, _padded(D_MODEL)))],
        out_specs=pl.BlockSpec((tm, D_MODEL), lambda i: (jnp.maximum(i - 1, 0), 0)),
        out_shape=jax.ShapeDtypeStruct((t, D_MODEL), F32),
        scratch_shapes=[pltpu.VMEM(((CONV_F - 1) * SUBLANES, 2 * D_FF), F32),
                        pltpu.VMEM((D_MODEL // LANES, SUBLANES * _scan_pitch(tm), LANES), F32),
                        pltpu.VMEM((tm, D_MODEL), F32), pltpu.VMEM((tm, D_MODEL), BF16)],
        compiler_params=pltpu.CompilerParams(dimension_semantics=("arbitrary",),
                                             vmem_limit_bytes=VMEM_LIMIT),
        name="merge_ffn",
    )(x2, pa2, yb2, gab2, bm, wpb, wout, gpost, gpre2, gpost2, perm, wup, cw, cb, wdn)


def _block_diag(w):
    h, n, _ = w.shape
    col = jnp.arange(h * n)
    tile = (col[None, :] % n == jnp.arange(n)[:, None]).astype(w.dtype)
    rep = jnp.dot(w.reshape(h * n, n), tile, precision=lax.Precision.HIGHEST)
    return jnp.where(col[:, None] // n == col[None, :] // n, rep, 0.0)


def _gate_windows(w_a, w_x):
    da, dx = _block_diag(w_a), _block_diag(w_x)
    tiles = []
    for j, ks in enumerate(GATE_STARTS):
        cols = slice(j * GATE_TILE, (j + 1) * GATE_TILE)
        tiles.append(jnp.concatenate([da[ks:ks + GATE_WIN, cols], dx[ks:ks + GATE_WIN, cols]],
                                     axis=1))
    return jnp.stack([_bf16_weight(t) for t in tiles])


def _layer(x, mix_norm_pre, mix_norm_post, w_in, conv_a_w, conv_a_b, w_rg_a, b_rg_a, w_rg_x,
           b_rg_x, lru_lambda, b_forget, b_merge, w_proj_a, w_proj_b, w_out, ffn_norm_pre,
           ffn_norm_post, w_up, conv_f_w, conv_f_b, w_down):
    bsz, seq, _ = x.shape
    t = bsz * seq
    row = lambda a: a.reshape(1, -1)

    w_main = _bf16_weight(w_in[:, :_OFF_F])
    w_gate = _bf16_weight(w_in[:, _OFF_G:])
    w_f = jnp.pad(w_in[:, _OFF_F:_OFF_G].astype(BF16), ((0, 0), (0, LANES - H_ATT)))
    bf = jnp.concatenate([b_forget, jnp.zeros((LANES - H_ATT,), b_forget.dtype)]).reshape(1, LANES)
    wg = _gate_windows(w_rg_a, w_rg_x)

    x2 = x.reshape(t, D_MODEL)
    q, k, v, gab, f, pa = _mix_in(x2, row(mix_norm_pre), w_main, w_gate, w_f, conv_a_w,
                                  row(conv_a_b), wg, row(b_rg_a), row(b_rg_x), row(lru_lambda),
                                  _bf16_weight(w_proj_a), tc=MIX_IN_ROWS, seq=seq)

    r3 = lambda a: a.reshape(bsz, seq, a.shape[-1])
    yb = _fox_attn(r3(q), r3(k), r3(v), r3(f), bf, tq=ATTN_Q_ROWS)

    out = _merge_ffn(x2, pa, yb.reshape(t, D_ATT), gab, row(b_merge),
                     _bf16_weight(w_proj_b), _bf16_weight(w_out), row(mix_norm_post),
                     row(ffn_norm_pre), row(ffn_norm_post), _bf16_weight(w_up), conv_f_w,
                     row(conv_f_b), _bf16_weight(w_down), tm=FFN_ROWS, seq=seq)
    return out.reshape(bsz, seq, D_MODEL)


def kernel(x, mix_norm_pre, mix_norm_post, w_in, conv_a_w, conv_a_b, w_rg_a, b_rg_a, w_rg_x, b_rg_x, lru_lambda, b_forget, b_merge, w_proj_a, w_proj_b, w_out, ffn_norm_pre, ffn_norm_post, w_up, conv_f_w, conv_f_b, w_down):
    depth = w_in.shape[0]
    for layer in range(depth):
        x = _layer(x, mix_norm_pre[layer], mix_norm_post[layer], w_in[layer], conv_a_w[layer],
                   conv_a_b[layer], w_rg_a[layer], b_rg_a[layer], w_rg_x[layer], b_rg_x[layer],
                   lru_lambda[layer], b_forget[layer], b_merge[layer], w_proj_a[layer],
                   w_proj_b[layer], w_out[layer], ffn_norm_pre[layer], ffn_norm_post[layer],
                   w_up[layer], conv_f_w[layer], conv_f_b[layer], w_down[layer])
    return x
```

```python
import functools
import math

import jax
import jax.numpy as jnp
from jax import lax
from jax.experimental import pallas as pl
from jax.experimental.pallas import tpu as pltpu

D_MODEL = 1024
D_RNN = 1280
H_RNN = 16
RNN_BLOCK = D_RNN // H_RNN
CONV_A = 4
LRU_C = 8.0
H_ATT = 16
HEAD_DIM = 64
D_ATT = H_ATT * HEAD_DIM
D_FF = 3 * D_MODEL
CONV_F = 3
RMS_EPS = 1e-6

LANES = 128
SUBLANES = 8
MXU_COLS = 256
VMEM_LIMIT = 56 * 1024 * 1024

MIX_IN_ROWS = 256
ATTN_Q_ROWS = 256
FFN_ROWS = 256

LOG2E = math.log2(math.e)

F32 = jnp.float32
BF16 = jnp.bfloat16

_OFF_XA = 0
_OFF_GA = _OFF_XA + D_RNN
_OFF_Q = _OFF_GA + D_RNN
_OFF_K = _OFF_Q + D_ATT
_OFF_V = _OFF_K + D_ATT
_OFF_F = _OFF_V + D_ATT
_OFF_G = _OFF_F + H_ATT

GATE_TILE = MXU_COLS
GATE_WIN = 2 * MXU_COLS
N_GATE_TILES = D_RNN // GATE_TILE


def _gate_window_start(j):
    lo = (j * GATE_TILE // RNN_BLOCK) * RNN_BLOCK
    start = min((lo // LANES) * LANES, D_RNN - GATE_WIN)
    hi = -(-((j + 1) * GATE_TILE) // RNN_BLOCK) * RNN_BLOCK
    assert start <= lo and hi <= start + GATE_WIN
    return start


GATE_STARTS = tuple(_gate_window_start(j) for j in range(N_GATE_TILES))


def _rms(x, gain):
    y = x * lax.rsqrt(jnp.mean(x * x, axis=-1, keepdims=True) + RMS_EPS)
    return y * gain


def _padded(n):
    return n + LANES if (n // 32) % 8 == 0 else n


def _bf16_weight(w):
    k, n = w.shape
    return jnp.pad(w, ((0, 0), (0, _padded(n) - n))).astype(BF16)


def _const_spec(shape):
    nd = len(shape)
    return pl.BlockSpec(shape, lambda *_: (0,) * nd, pipeline_mode=pl.Buffered(1))


def _scan_pitch(tc):
    seg = tc // SUBLANES
    return seg + SUBLANES if (seg // SUBLANES) % 2 == 0 else seg


def _mix_in_kernel(x_ref, g_ref, w_ref, wgate_ref, wf_ref, cw_ref, cb_ref, wg_ref, ba_ref, bx_ref,
                   lam_ref, wp_ref, q_ref, k_ref, v_ref, gab_ref, f_ref, pa_ref,
                   xa_s, ga_in_s, xpad, ga_s, xc_s, xcb_s, a_s, b_s, h_s, p_s, y_s, hprev,
                   *, tc, tiles_per_seq):
    seg = tc // SUBLANES
    pitch = _scan_pitch(tc)
    n_lane = D_RNN // LANES
    step = pl.program_id(0)

    @pl.when(step == 0)
    def _():
        xa_s[...] = jnp.zeros_like(xa_s)
        ga_in_s[...] = jnp.zeros_like(ga_in_s)
        xpad[0:SUBLANES, :] = jnp.zeros((SUBLANES, D_RNN), F32)
        hprev[...] = jnp.zeros_like(hprev)

    @pl.when((step + tiles_per_seq - 1) % tiles_per_seq == 0)
    def _():
        xpad[0:SUBLANES, :] = jnp.zeros((SUBLANES, D_RNN), F32)
        hprev[...] = jnp.zeros_like(hprev)

    xpad[SUBLANES:SUBLANES + tc, :] = xa_s[...]
    ga_s[...] = ga_in_s[...]

    h = _rms(x_ref[...], g_ref[...]).astype(BF16)

    def mm(lo, hi):
        return jnp.dot(h, w_ref[:, lo:hi], preferred_element_type=F32)

    def proj_chunk(kind, c):
        lo, hi = c * MXU_COLS, (c + 1) * MXU_COLS
        if kind == "q":
            q_ref[:, lo:hi] = (mm(_OFF_Q + lo, _OFF_Q + hi)
                               * (LOG2E / math.sqrt(HEAD_DIM))).astype(BF16)
        elif kind == "k":
            k_ref[:, lo:hi] = mm(_OFF_K + lo, _OFF_K + hi).astype(BF16)
        elif kind == "v":
            v_ref[:, lo:hi] = mm(_OFF_V + lo, _OFF_V + hi).astype(BF16)
        elif kind == "g":
            gab_ref[:, lo:hi] = jnp.dot(h, wgate_ref[:, lo:hi],
                                        preferred_element_type=F32).astype(BF16)
        elif kind == "xa":
            xa_s[:, lo:hi] = mm(_OFF_XA + lo, _OFF_XA + hi)
        elif kind == "ga":
            ga_in_s[:, lo:hi] = mm(_OFF_GA + lo, _OFF_GA + hi)
        else:
            f_ref[...] = jnp.dot(h, wf_ref[...], preferred_element_type=F32)

    pending = ([("q", c) for c in range(D_ATT // MXU_COLS)]
               + [("k", c) for c in range(D_ATT // MXU_COLS)]
               + [("v", c) for c in range(D_ATT // MXU_COLS)]
               + [("g", c) for c in range(2 * D_MODEL // MXU_COLS)] + [("f", 0)]
               + [("xa", c) for c in range(D_RNN // MXU_COLS)]
               + [("ga", c) for c in range(D_RNN // MXU_COLS)])

    def emit(n):
        for _ in range(min(n, len(pending))):
            proj_chunk(*pending.pop(0))

    for j in range(N_GATE_TILES):
        cols = slice(j * GATE_TILE, (j + 1) * GATE_TILE)
        xc = cb_ref[:, cols] + cw_ref[CONV_A - 1:CONV_A, cols] * xpad[SUBLANES:SUBLANES + tc, cols]
        for kk in range(CONV_A - 1):
            sh = CONV_A - 1 - kk
            xc = xc + cw_ref[kk:kk + 1, cols] * xpad[SUBLANES - sh:SUBLANES - sh + tc, cols]
        xc_s[:, cols] = xc
        xcb_s[:, cols] = xc.astype(BF16)
        emit(1)
    xpad[0:SUBLANES, :] = xpad[tc:tc + SUBLANES, :]

    sp = jax.nn.softplus(-lam_ref[...])
    lane_chunks = GATE_TILE // LANES
    for j in range(N_GATE_TILES):
        ks = GATE_STARTS[j]
        cols = slice(j * GATE_TILE, (j + 1) * GATE_TILE)
        g = jnp.dot(xcb_s[:, ks:ks + GATE_WIN], wg_ref[j, :, :2 * GATE_TILE],
                    preferred_element_type=F32)
        r = jax.nn.sigmoid(g[:, :GATE_TILE] + ba_ref[:, cols])
        i = jax.nn.sigmoid(g[:, GATE_TILE:] + bx_ref[:, cols])
        log_a = -LRU_C * r * sp[:, cols]
        a = jnp.exp(log_a)
        b = jnp.sqrt(-jnp.tanh(log_a) * (a * a + 1.0)) * (i * xc_s[:, cols])
        for c in range(lane_chunks):
            for s in range(SUBLANES):
                dst = slice(s * pitch, s * pitch + seg)
                a_s[j * lane_chunks + c, dst, :] = a[s * seg:(s + 1) * seg, c * LANES:(c + 1) * LANES]
                b_s[j * lane_chunks + c, dst, :] = b[s * seg:(s + 1) * seg, c * LANES:(c + 1) * LANES]
        emit(3)

    h8 = [jnp.zeros((SUBLANES, LANES), F32) for _ in range(n_lane)]
    p8 = [jnp.ones((SUBLANES, LANES), F32) for _ in range(n_lane)]
    for j in range(seg):
        rows = pl.ds(j, SUBLANES, stride=pitch)
        for c in range(n_lane):
            a_j = a_s[c, rows, :]
            h8[c] = a_j * h8[c] + b_s[c, rows, :]
            p8[c] = p8[c] * a_j
            h_s[c, rows, :] = h8[c]
            p_s[c, rows, :] = p8[c]
        if j % 4 == 3:
            emit(1)

    pa = jnp.zeros((tc, D_MODEL), F32)
    for c in range(n_lane):
        cols = slice(c * LANES, (c + 1) * LANES)
        start = hprev[:, cols]
        for s in range(SUBLANES):
            rows = slice(s * seg, (s + 1) * seg)
            src = slice(s * pitch, s * pitch + seg)
            hfin = h_s[c, src, :] + p_s[c, src, :] * start
            y_s[rows, cols] = (jax.nn.gelu(ga_s[rows, cols]) * hfin).astype(BF16)
            start = h8[c][s:s + 1, :] + p8[c][s:s + 1, :] * start
        hprev[:, cols] = start
        emit(1)
        if c % lane_chunks == lane_chunks - 1:
            slab = slice((c - lane_chunks + 1) * LANES, (c + 1) * LANES)
            pa = pa + jnp.dot(y_s[:, slab], wp_ref[slab, :D_MODEL], preferred_element_type=F32)
    emit(len(pending))
    pa_ref[...] = pa


def _mix_in(x2, gain, w_main, w_gate, w_f, cw, cb, wg, ba, bx, lam, wp, tc, seq):
    t = x2.shape[0]
    n_tiles = t // tc
    row = lambda n: pl.BlockSpec((tc, n), lambda i: (jnp.minimum(i, n_tiles - 1), 0))
    scan_shape = (D_RNN // LANES, SUBLANES * _scan_pitch(tc), LANES)
    consts = (gain, w_main, w_gate, w_f, cw, cb, wg, ba, bx, lam, wp)
    return pl.pallas_call(
        functools.partial(_mix_in_kernel, tc=tc, tiles_per_seq=seq // tc),
        grid=(n_tiles + 1,),
        in_specs=[row(D_MODEL)] + [_const_spec(c.shape) for c in consts],
        out_specs=[row(D_ATT), row(D_ATT), row(D_ATT), row(2 * D_MODEL), row(LANES),
                   pl.BlockSpec((tc, D_MODEL), lambda i: (jnp.maximum(i - 1, 0), 0))],
        out_shape=[jax.ShapeDtypeStruct((t, D_ATT), BF16), jax.ShapeDtypeStruct((t, D_ATT), BF16),
                   jax.ShapeDtypeStruct((t, D_ATT), BF16),
                   jax.ShapeDtypeStruct((t, 2 * D_MODEL), BF16),
                   jax.ShapeDtypeStruct((t, LANES), F32),
                   jax.ShapeDtypeStruct((t, D_MODEL), F32)],
        scratch_shapes=[pltpu.VMEM((tc, D_RNN), F32), pltpu.VMEM((tc, D_RNN), F32),
                        pltpu.VMEM((tc + SUBLANES, D_RNN), F32), pltpu.VMEM((tc, D_RNN), F32),
                        pltpu.VMEM((tc, D_RNN), F32), pltpu.VMEM((tc, D_RNN), BF16),
                        pltpu.VMEM(scan_shape, F32), pltpu.VMEM(scan_shape, F32),
                        pltpu.VMEM(scan_shape, F32), pltpu.VMEM(scan_shape, F32),
                        pltpu.VMEM((tc, D_RNN), BF16), pltpu.VMEM((1, D_RNN), F32)],
        compiler_params=pltpu.CompilerParams(dimension_semantics=("arbitrary",),
                                             vmem_limit_bytes=VMEM_LIMIT),
        name="mix_in",
    )(x2, *consts)


def _fox_attn_kernel(q_ref, k_ref, v_ref, f_ref, bf_ref, o_ref, cum, cum_t, km_s, vt_s, ck_s,
                     cq_s, sc_s, p_s, m_s, off_s, o0_s, *, s, tq):
    pair = pl.program_id(1)

    @pl.when(pair == 0)
    def _():
        c = jax.nn.log_sigmoid(f_ref[0] + bf_ref[...])
        row = lax.broadcasted_iota(jnp.int32, (s, LANES), 0)
        d = 1
        while d < s:
            c = c + jnp.where(row >= d, pltpu.roll(c, d, axis=0), 0.0)
            d *= 2
        c = c * LOG2E
        cum[...] = c
        cum_t[...] = c.T

    lane = lax.broadcasted_iota(jnp.int32, (1, LANES), 1)
    n_q = s // tq

    k = k_ref[0]
    v_t = v_ref[0].astype(F32).T
    v_row = lax.broadcasted_iota(jnp.int32, (LANES, 1), 0)
    for hh in range(2):
        head = 2 * pair + hh
        in_head = (lane >= hh * HEAD_DIM) & (lane < (hh + 1) * HEAD_DIM)
        km_s[hh] = jnp.where(in_head, k, jnp.zeros_like(k))
        own_rows = (v_row >= hh * HEAD_DIM) & (v_row < (hh + 1) * HEAD_DIM)
        vt_s[hh] = jnp.where(own_rows, v_t, 1.0).astype(BF16)
        ck = jnp.sum(jnp.where(lane == head, cum[...], 0.0), axis=-1, keepdims=True)
        ck_s[hh] = jnp.broadcast_to(ck, (s, LANES))
        cq_s[hh] = cum_t[pl.ds(head, 1), :]

    tasks = [(hh, i) for hh in range(2) for i in range(n_q)]

    def pieces(i, j):
        if j == i:
            return [(0, tq // 2, 0, tq), (tq // 2, tq, tq // 2, tq)]
        return [(0, tq, 0, tq)]

    def score_tile(n, j):
        hh, i = tasks[n]
        slot = n % 2
        ks = slice(j * tq, (j + 1) * tq)
        sc = lax.dot_general(km_s[hh, ks, :], q_ref[0, i * tq:(i + 1) * tq, :],
                             (((1,), (1,)), ((), ())), preferred_element_type=F32)
        for n_piece, (r0, r1, l0, l1) in enumerate(pieces(i, j)):
            rows = slice(j * tq + r0, j * tq + r1)
            blk = jnp.concatenate([sc[r0:r1, c * LANES:(c + 1) * LANES] - ck_s[hh, rows, :]
                                   for c in range(l0 // LANES, l1 // LANES)], axis=1)
            if j == i:
                key = lax.broadcasted_iota(jnp.int32, blk.shape, 0) + r0
                qry = lax.broadcasted_iota(jnp.int32, blk.shape, 1) + l0
                blk = jnp.where(key <= qry, blk, -jnp.inf)
            sc_s[slot, rows, l0:l1] = blk
            part = blk[0:SUBLANES, :]
            for r in range(1, (r1 - r0) // SUBLANES):
                part = jnp.maximum(part, blk[r * SUBLANES:(r + 1) * SUBLANES, :])
            if j == 0 and n_piece == 0:
                m_s[slot] = part
            else:
                m_s[slot, :, l0:l1] = jnp.maximum(m_s[slot, :, l0:l1], part)

    def finish_scores(n):
        hh, i = tasks[n]
        slot = n % 2
        m = jnp.max(m_s[slot], axis=0, keepdims=True)
        cq_i = cq_s[hh, :, i * tq:(i + 1) * tq]
        off_s[slot] = (m + cq_i) - cq_i

    def exp_tile(n, j):
        slot = n % 2
        i = tasks[n][1]
        for r0, r1, l0, l1 in pieces(i, j):
            rows = slice(j * tq + r0, j * tq + r1)
            p_s[slot, rows, l0:l1] = jnp.exp2(sc_s[slot, rows, l0:l1]
                                              - off_s[slot, :, l0:l1]).astype(BF16)
        if j == i:
            p_s[slot, j * tq + tq // 2:(j + 1) * tq, 0:tq // 2] = jnp.zeros((tq // 2, tq // 2), BF16)

    def weighted_values(n):
        hh, i = tasks[n]
        slot = n % 2
        qs, qe = i * tq, (i + 1) * tq
        o = jnp.dot(vt_s[hh, :, 0:qe], p_s[slot, 0:qe, :], preferred_element_type=F32)
        sum_row = (1 - hh) * HEAD_DIM
        o = o / o[sum_row:sum_row + 1, :]
        if hh == 0:
            o0_s[i] = o
        else:
            row = lax.broadcasted_iota(jnp.int32, (LANES, 1), 0)
            o_ref[0, qs:qe, :] = jnp.where(row < HEAD_DIM, o0_s[i], o).T.astype(BF16)

    for j in range(tasks[0][1] + 1):
        score_tile(0, j)
    finish_scores(0)
    for n in range(len(tasks)):
        n_exp = tasks[n][1] + 1
        n_score = tasks[n + 1][1] + 1 if n + 1 < len(tasks) else 0
        for j in range(max(n_exp, n_score)):
            if j < n_score:
                score_tile(n + 1, j)
            if j < n_exp:
                exp_tile(n, j)
        if n_score:
            finish_scores(n + 1)
        weighted_values(n)


def _fox_attn(q3, k3, v3, f3, bf, tq):
    b, s, _ = q3.shape
    blk = pl.BlockSpec((1, s, LANES), lambda bi, pi: (bi, 0, pi))
    return pl.pallas_call(
        functools.partial(_fox_attn_kernel, s=s, tq=tq),
        grid=(b, H_ATT // 2),
        in_specs=[blk, blk, blk, pl.BlockSpec((1, s, LANES), lambda bi, pi: (bi, 0, 0)),
                  _const_spec((1, LANES))],
        out_specs=blk,
        out_shape=jax.ShapeDtypeStruct((b, s, D_ATT), BF16),
        scratch_shapes=[pltpu.VMEM((s, LANES), F32), pltpu.VMEM((LANES, s), F32),
                        pltpu.VMEM((2, s, LANES), BF16), pltpu.VMEM((2, LANES, s), BF16),
                        pltpu.VMEM((2, s, LANES), F32), pltpu.VMEM((2, 1, s), F32),
                        pltpu.VMEM((2, s, tq), F32), pltpu.VMEM((2, s, tq), BF16),
                        pltpu.VMEM((2, SUBLANES, tq), F32), pltpu.VMEM((2, 1, tq), F32),
                        pltpu.VMEM((s // tq, LANES, tq), F32)],
        compiler_params=pltpu.CompilerParams(dimension_semantics=("arbitrary", "arbitrary"),
                                             vmem_limit_bytes=VMEM_LIMIT),
        name="fox_attn",
    )(q3, k3, v3, f3, bf)


FF_CHUNK = 512
N_MERGE_STAGES = 6


def _merge_ffn_kernel(x_ref, pa_ref, yb_ref, gab_ref, bm_ref, wpb_ref, wout_ref, gpost_ref,
                      gpre2_ref, gpost2_ref, perm_ref, wup_ref, cw_ref, cb_ref, wdn_ref, o_ref,
                      tail, unperm, x1_s, hp_s, *, tm, tiles_per_seq):
    i = pl.program_id(0)
    seg = tm // SUBLANES
    pitch = _scan_pitch(tm)

    @pl.when(i == 0)
    def _():
        x1_s[...] = jnp.zeros_like(x1_s)
        hp_s[...] = jnp.zeros_like(hp_s)
        tail[...] = jnp.zeros_like(tail)

    @pl.when((i + tiles_per_seq - 1) % tiles_per_seq == 0)
    def _():
        tail[...] = jnp.zeros_like(tail)

    x1_prev = x1_s[...]
    hp = hp_s[...]

    def merge_stage(k, st):
        if k == 0:
            st["gates"] = jax.nn.sigmoid(gab_ref[...].astype(F32) + bm_ref[...])
        elif k == 1:
            st["pb"] = jnp.dot(yb_ref[...], wpb_ref[:, :D_MODEL], preferred_element_type=F32)
        elif k == 2:
            g = st.pop("gates")
            st["merged"] = (g[:, :D_MODEL] * pa_ref[...] + g[:, D_MODEL:] * st.pop("pb")).astype(BF16)
        elif k == 3:
            st["mix"] = jnp.dot(st.pop("merged"), wout_ref[:, :D_MODEL], preferred_element_type=F32)
        elif k == 4:
            x1_s[...] = x_ref[...] + _rms(st.pop("mix"), gpost_ref[...])
        elif k == 5:
            h = _rms(x1_s[...], gpre2_ref[...]).astype(BF16)
            hp_s[...] = jnp.dot(perm_ref[...], h, preferred_element_type=F32).astype(BF16)

    first = lax.broadcasted_iota(jnp.int32, (SUBLANES, FF_CHUNK), 0) == 0

    def up_proj(c):
        return [jnp.dot(hp, wup_ref[:, off + c * FF_CHUNK:off + (c + 1) * FF_CHUNK],
                        preferred_element_type=F32) for off in (0, D_FF)]

    def conv(u, cols):
        prev = [jnp.where(first, pltpu.roll(tail[g * SUBLANES:(g + 1) * SUBLANES, cols], 1, axis=0),
                          pltpu.roll(u[tm - (2 - g) * SUBLANES:tm - (1 - g) * SUBLANES, :], 1, axis=0))
                for g in range(CONV_F - 1)]
        tail[:, cols] = u[tm - (CONV_F - 1) * SUBLANES:, :]
        u1 = jnp.concatenate([prev[1], u[:tm - SUBLANES, :]], axis=0)
        u2 = jnp.concatenate([prev[0], prev[1], u[:tm - 2 * SUBLANES, :]], axis=0)
        return (cb_ref[:, cols] + cw_ref[2:3, cols] * u + cw_ref[1:2, cols] * u1
                + cw_ref[0:1, cols] * u2)

    n_chunks = D_FF // FF_CHUNK
    stages = {}
    acc = jnp.zeros((tm, D_MODEL), F32)
    u_next = up_proj(0)
    for c in range(n_chunks):
        u_gate, u_val = u_next
        if c + 1 < n_chunks:
            u_next = up_proj(c + 1)
        for k in range(c * N_MERGE_STAGES // n_chunks, (c + 1) * N_MERGE_STAGES // n_chunks):
            merge_stage(k, stages)
        gate = conv(u_gate, slice(c * FF_CHUNK, (c + 1) * FF_CHUNK))
        val = conv(u_val, slice(D_FF + c * FF_CHUNK, D_FF + (c + 1) * FF_CHUNK))
        act = (jax.nn.gelu(gate) * val).astype(BF16)
        acc = acc + jnp.dot(act, wdn_ref[c * FF_CHUNK:(c + 1) * FF_CHUNK, :D_MODEL],
                            preferred_element_type=F32)

    for j in range(seg):
        for c in range(D_MODEL // LANES):
            unperm[c, pl.ds(j, SUBLANES, stride=pitch), :] = (
                acc[j * SUBLANES:(j + 1) * SUBLANES, c * LANES:(c + 1) * LANES])
    ffn = jnp.concatenate(
        [jnp.concatenate([unperm[c, s * pitch:s * pitch + seg, :] for s in range(SUBLANES)], axis=0)
         for c in range(D_MODEL // LANES)], axis=1)
    o_ref[...] = x1_prev + _rms(ffn, gpost2_ref[...])


def _merge_ffn(x2, pa2, yb2, gab2, bm, wpb, wout, gpost, gpre2, gpost2, wup, cw, cb, wdn,
               tm, seq):
    t = x2.shape[0]
    n_tiles = t // tm
    row = lambda n: pl.BlockSpec((tm, n), lambda i: (jnp.minimum(i, n_tiles - 1), 0))
    r = jnp.arange(tm)
    perm = (r[None, :] == ((r % SUBLANES) * (tm // SUBLANES) + r // SUBLANES)[:, None]).astype(BF16)
    return pl.pallas_call(
        functools.partial(_merge_ffn_kernel, tm=tm, tiles_per_seq=seq // tm),
        grid=(n_tiles + 1,),
        in_specs=[row(D_MODEL), row(D_MODEL), row(D_ATT), row(2 * D_MODEL),
                  _const_spec((1, 2 * D_MODEL)), _const_spec((D_ATT, _padded(D_MODEL))),
                  _const_spec((D_MODEL, _padded(D_MODEL))), _const_spec((1, D_MODEL)),
                  _const_spec((1, D_MODEL)), _const_spec((1, D_MODEL)), _const_spec((tm, tm)),
                  _const_spec((D_MODEL, _padded(2 * D_FF))), _const_spec((CONV_F, 2 * D_FF)),
                  _const_spec((1, 2 * D_FF)), _const_spec((D_FF, _padded(D_MODEL)))],
        out_specs=pl.BlockSpec((tm, D_MODEL), lambda i: (jnp.maximum(i - 1, 0), 0)),
        out_shape=jax.ShapeDtypeStruct((t, D_MODEL), F32),
        scratch_shapes=[pltpu.VMEM(((CONV_F - 1) * SUBLANES, 2 * D_FF), F32),
                        pltpu.VMEM((D_MODEL // LANES, SUBLANES * _scan_pitch(tm), LANES), F32),
                        pltpu.VMEM((tm, D_MODEL), F32), pltpu.VMEM((tm, D_MODEL), BF16)],
        compiler_params=pltpu.CompilerParams(dimension_semantics=("arbitrary",),
                                             vmem_limit_bytes=VMEM_LIMIT),
        name="merge_ffn",
    )(x2, pa2, yb2, gab2, bm, wpb, wout, gpost, gpre2, gpost2, perm, wup, cw, cb, wdn)


def _block_diag(w):
    h, n, _ = w.shape
    col = jnp.arange(h * n)
    tile = (col[None, :] % n == jnp.arange(n)[:, None]).astype(w.dtype)
    rep = jnp.dot(w.reshape(h * n, n), tile, precision=lax.Precision.HIGHEST)
    return jnp.where(col[:, None] // n == col[None, :] // n, rep, 0.0)


def _gate_windows(w_a, w_x):
    da, dx = _block_diag(w_a), _block_diag(w_x)
    tiles = []
    for j, ks in enumerate(GATE_STARTS):
        cols = slice(j * GATE_TILE, (j + 1) * GATE_TILE)
        tiles.append(jnp.concatenate([da[ks:ks + GATE_WIN, cols], dx[ks:ks + GATE_WIN, cols]],
                                     axis=1))
    return jnp.stack([_bf16_weight(t) for t in tiles])


def _layer(x, mix_norm_pre, mix_norm_post, w_in, conv_a_w, conv_a_b, w_rg_a, b_rg_a, w_rg_x,
           b_rg_x, lru_lambda, b_forget, b_merge, w_proj_a, w_proj_b, w_out, ffn_norm_pre,
           ffn_norm_post, w_up, conv_f_w, conv_f_b, w_down):
    bsz, seq, _ = x.shape
    t = bsz * seq
    row = lambda a: a.reshape(1, -1)

    w_main = _bf16_weight(w_in[:, :_OFF_F])
    w_gate = _bf16_weight(w_in[:, _OFF_G:])
    w_f = jnp.pad(w_in[:, _OFF_F:_OFF_G].astype(BF16), ((0, 0), (0, LANES - H_ATT)))
    bf = jnp.concatenate([b_forget, jnp.zeros((LANES - H_ATT,), b_forget.dtype)]).reshape(1, LANES)
    wg = _gate_windows(w_rg_a, w_rg_x)

    x2 = x.reshape(t, D_MODEL)
    q, k, v, gab, f, pa = _mix_in(x2, row(mix_norm_pre), w_main, w_gate, w_f, conv_a_w,
                                  row(conv_a_b), wg, row(b_rg_a), row(b_rg_x), row(lru_lambda),
                                  _bf16_weight(w_proj_a), tc=MIX_IN_ROWS, seq=seq)

    r3 = lambda a: a.reshape(bsz, seq, a.shape[-1])
    yb = _fox_attn(r3(q), r3(k), r3(v), r3(f), bf, tq=ATTN_Q_ROWS)

    out = _merge_ffn(x2, pa, yb.reshape(t, D_ATT), gab, row(b_merge),
                     _bf16_weight(w_proj_b), _bf16_weight(w_out), row(mix_norm_post),
                     row(ffn_norm_pre), row(ffn_norm_post), _bf16_weight(w_up), conv_f_w,
                     row(conv_f_b), _bf16_weight(w_down), tm=FFN_ROWS, seq=seq)
    return out.reshape(bsz, seq, D_MODEL)


def kernel(x, mix_norm_pre, mix_norm_post, w_in, conv_a_w, conv_a_b, w_rg_a, b_rg_a, w_rg_x, b_rg_x, lru_lambda, b_forget, b_merge, w_proj_a, w_proj_b, w_out, ffn_norm_pre, ffn_norm_post, w_up, conv_f_w, conv_f_b, w_down):
    depth = w_in.shape[0]
    for layer in range(depth):
        x = _layer(x, mix_norm_pre[layer], mix_norm_post[layer], w_in[layer], conv_a_w[layer],
                   conv_a_b[layer], w_rg_a[layer], b_rg_a[layer], w_rg_x[layer], b_rg_x[layer],
                   lru_lambda[layer], b_forget[layer], b_merge[layer], w_proj_a[layer],
                   w_proj_b[layer], w_out[layer], ffn_norm_pre[layer], ffn_norm_post[layer],
                   w_up[layer], conv_f_w[layer], conv_f_b[layer], w_down[layer])
    return x
```

```python
import functools
import math

import jax
import jax.numpy as jnp
from jax import lax
from jax.experimental import pallas as pl
from jax.experimental.pallas import tpu as pltpu

D_MODEL = 1024
D_RNN = 1280
H_RNN = 16
RNN_BLOCK = D_RNN // H_RNN
CONV_A = 4
LRU_C = 8.0
H_ATT = 16
HEAD_DIM = 64
D_ATT = H_ATT * HEAD_DIM
N_PAIRS = H_ATT // 2
D_FF = 3 * D_MODEL
CONV_F = 3
RMS_EPS = 1e-6

LANES = 128
SUBLANES = 8
MXU_COLS = 256
VMEM_LIMIT = 56 * 1024 * 1024

MIX_IN_ROWS = 256
ATTN_Q_ROWS = 256
FFN_ROWS = 256

LOG2E = math.log2(math.e)

F32 = jnp.float32
BF16 = jnp.bfloat16

_OFF_XA = 0
_OFF_GA = _OFF_XA + D_RNN
_OFF_Q = _OFF_GA + D_RNN
_OFF_K = _OFF_Q + D_ATT
_OFF_V = _OFF_K + D_ATT
_OFF_F = _OFF_V + D_ATT
_OFF_G = _OFF_F + H_ATT

GATE_TILE = MXU_COLS
GATE_WIN = 2 * MXU_COLS
N_GATE_TILES = D_RNN // GATE_TILE


def _gate_window_start(j):
    lo = (j * GATE_TILE // RNN_BLOCK) * RNN_BLOCK
    start = min((lo // LANES) * LANES, D_RNN - GATE_WIN)
    hi = -(-((j + 1) * GATE_TILE) // RNN_BLOCK) * RNN_BLOCK
    assert start <= lo and hi <= start + GATE_WIN
    return start


GATE_STARTS = tuple(_gate_window_start(j) for j in range(N_GATE_TILES))


def _rms(x, gain):
    y = x * lax.rsqrt(jnp.mean(x * x, axis=-1, keepdims=True) + RMS_EPS)
    return y * gain


def _padded(n):
    return n + LANES if (n // 32) % 8 == 0 else n


def _bf16_weight(w):
    k, n = w.shape
    return jnp.pad(w, ((0, 0), (0, _padded(n) - n))).astype(BF16)


def _const_spec(shape):
    nd = len(shape)
    return pl.BlockSpec(shape, lambda *_: (0,) * nd, pipeline_mode=pl.Buffered(1))


def _scan_pitch(tc):
    seg = tc // SUBLANES
    return seg + SUBLANES if (seg // SUBLANES) % 2 == 0 else seg


def _mix_in_kernel(x_ref, g_ref, w_ref, wgate_ref, wf_ref, cw_ref, cb_ref, wg_ref, ba_ref, bx_ref,
                   lam_ref, wp_ref, q_ref, k_ref, v_ref, gab_ref, f_ref, pa_ref,
                   xa_s, ga_in_s, xpad, ga_s, xc_s, xcb_s, a_s, b_s, h_s, p_s, y_s, hprev,
                   *, tc, tiles_per_seq):
    seg = tc // SUBLANES
    pitch = _scan_pitch(tc)
    n_lane = D_RNN // LANES
    step = pl.program_id(0)

    @pl.when(step == 0)
    def _():
        xa_s[...] = jnp.zeros_like(xa_s)
        ga_in_s[...] = jnp.zeros_like(ga_in_s)
        xpad[0:SUBLANES, :] = jnp.zeros((SUBLANES, D_RNN), F32)
        hprev[...] = jnp.zeros_like(hprev)

    @pl.when((step + tiles_per_seq - 1) % tiles_per_seq == 0)
    def _():
        xpad[0:SUBLANES, :] = jnp.zeros((SUBLANES, D_RNN), F32)
        hprev[...] = jnp.zeros_like(hprev)

    xpad[SUBLANES:SUBLANES + tc, :] = xa_s[...]
    ga_s[...] = ga_in_s[...]

    h = _rms(x_ref[...], g_ref[...]).astype(BF16)

    def mm(lo, hi):
        return jnp.dot(h, w_ref[:, lo:hi], preferred_element_type=F32)

    def proj_chunk(kind, c):
        lo, hi = c * MXU_COLS, (c + 1) * MXU_COLS

        def put_pairs(ref, val):
            for p in range(MXU_COLS // LANES):
                ref[c * (MXU_COLS // LANES) + p] = val[:, p * LANES:(p + 1) * LANES]

        if kind == "q":
            put_pairs(q_ref, (mm(_OFF_Q + lo, _OFF_Q + hi)
                              * (LOG2E / math.sqrt(HEAD_DIM))).astype(BF16))
        elif kind == "k":
            put_pairs(k_ref, mm(_OFF_K + lo, _OFF_K + hi).astype(BF16))
        elif kind == "v":
            put_pairs(v_ref, mm(_OFF_V + lo, _OFF_V + hi).astype(BF16))
        elif kind == "g":
            gab_ref[:, lo:hi] = jnp.dot(h, wgate_ref[:, lo:hi], preferred_element_type=F32)
        elif kind == "xa":
            xa_s[:, lo:hi] = mm(_OFF_XA + lo, _OFF_XA + hi)
        elif kind == "ga":
            ga_in_s[:, lo:hi] = mm(_OFF_GA + lo, _OFF_GA + hi)
        else:
            f_ref[...] = jnp.dot(h, wf_ref[...], preferred_element_type=F32)

    pending = ([("q", c) for c in range(D_ATT // MXU_COLS)]
               + [("k", c) for c in range(D_ATT // MXU_COLS)]
               + [("v", c) for c in range(D_ATT // MXU_COLS)]
               + [("g", c) for c in range(2 * D_MODEL // MXU_COLS)] + [("f", 0)]
               + [("xa", c) for c in range(D_RNN // MXU_COLS)]
               + [("ga", c) for c in range(D_RNN // MXU_COLS)])

    def emit(n):
        for _ in range(min(n, len(pending))):
            proj_chunk(*pending.pop(0))

    for j in range(N_GATE_TILES):
        cols = slice(j * GATE_TILE, (j + 1) * GATE_TILE)
        xc = cb_ref[:, cols] + cw_ref[CONV_A - 1:CONV_A, cols] * xpad[SUBLANES:SUBLANES + tc, cols]
        for kk in range(CONV_A - 1):
            sh = CONV_A - 1 - kk
            xc = xc + cw_ref[kk:kk + 1, cols] * xpad[SUBLANES - sh:SUBLANES - sh + tc, cols]
        xc_s[:, cols] = xc
        xcb_s[:, cols] = xc.astype(BF16)
        emit(1)
    xpad[0:SUBLANES, :] = xpad[tc:tc + SUBLANES, :]

    sp = jax.nn.softplus(-lam_ref[...])
    lane_chunks = GATE_TILE // LANES
    for j in range(N_GATE_TILES):
        ks = GATE_STARTS[j]
        cols = slice(j * GATE_TILE, (j + 1) * GATE_TILE)
        g = jnp.dot(xcb_s[:, ks:ks + GATE_WIN], wg_ref[j, :, :2 * GATE_TILE],
                    preferred_element_type=F32)
        r = jax.nn.sigmoid(g[:, :GATE_TILE] + ba_ref[:, cols])
        i = jax.nn.sigmoid(g[:, GATE_TILE:] + bx_ref[:, cols])
        log_a = -LRU_C * r * sp[:, cols]
        a = jnp.exp(log_a)
        b = jnp.sqrt(-jnp.tanh(log_a) * (a * a + 1.0)) * (i * xc_s[:, cols])
        for c in range(lane_chunks):
            for s in range(SUBLANES):
                dst = slice(s * pitch, s * pitch + seg)
                a_s[j * lane_chunks + c, dst, :] = a[s * seg:(s + 1) * seg, c * LANES:(c + 1) * LANES]
                b_s[j * lane_chunks + c, dst, :] = b[s * seg:(s + 1) * seg, c * LANES:(c + 1) * LANES]
        emit(3)

    h8 = [jnp.zeros((SUBLANES, LANES), F32) for _ in range(n_lane)]
    p8 = [jnp.ones((SUBLANES, LANES), F32) for _ in range(n_lane)]
    for j in range(seg):
        rows = pl.ds(j, SUBLANES, stride=pitch)
        for c in range(n_lane):
            a_j = a_s[c, rows, :]
            h8[c] = a_j * h8[c] + b_s[c, rows, :]
            p8[c] = p8[c] * a_j
            h_s[c, rows, :] = h8[c]
            p_s[c, rows, :] = p8[c]
        if j % 4 == 3:
            emit(1)

    pa = jnp.zeros((tc, D_MODEL), F32)
    for c in range(n_lane):
        cols = slice(c * LANES, (c + 1) * LANES)
        start = hprev[:, cols]
        for s in range(SUBLANES):
            rows = slice(s * seg, (s + 1) * seg)
            src = slice(s * pitch, s * pitch + seg)
            hfin = h_s[c, src, :] + p_s[c, src, :] * start
            y_s[rows, cols] = (jax.nn.gelu(ga_s[rows, cols]) * hfin).astype(BF16)
            start = h8[c][s:s + 1, :] + p8[c][s:s + 1, :] * start
        hprev[:, cols] = start
        emit(1)
        if c % lane_chunks == lane_chunks - 1:
            slab = slice((c - lane_chunks + 1) * LANES, (c + 1) * LANES)
            pa = pa + jnp.dot(y_s[:, slab], wp_ref[slab, :D_MODEL], preferred_element_type=F32)
    emit(len(pending))
    pa_ref[...] = pa


def _mix_in(x2, gain, w_main, w_gate, w_f, cw, cb, wg, ba, bx, lam, wp, tc, seq):
    t = x2.shape[0]
    n_tiles = t // tc
    row = lambda n: pl.BlockSpec((tc, n), lambda i: (jnp.minimum(i, n_tiles - 1), 0))
    pairs = pl.BlockSpec((N_PAIRS, tc, LANES), lambda i: (0, jnp.minimum(i, n_tiles - 1), 0))
    scan_shape = (D_RNN // LANES, SUBLANES * _scan_pitch(tc), LANES)
    consts = (gain, w_main, w_gate, w_f, cw, cb, wg, ba, bx, lam, wp)
    return pl.pallas_call(
        functools.partial(_mix_in_kernel, tc=tc, tiles_per_seq=seq // tc),
        grid=(n_tiles + 1,),
        in_specs=[row(D_MODEL)] + [_const_spec(c.shape) for c in consts],
        out_specs=[pairs, pairs, pairs, row(2 * D_MODEL), row(LANES),
                   pl.BlockSpec((tc, D_MODEL), lambda i: (jnp.maximum(i - 1, 0), 0))],
        out_shape=[jax.ShapeDtypeStruct((N_PAIRS, t, LANES), BF16),
                   jax.ShapeDtypeStruct((N_PAIRS, t, LANES), BF16),
                   jax.ShapeDtypeStruct((N_PAIRS, t, LANES), BF16),
                   jax.ShapeDtypeStruct((t, 2 * D_MODEL), F32),
                   jax.ShapeDtypeStruct((t, LANES), F32),
                   jax.ShapeDtypeStruct((t, D_MODEL), F32)],
        scratch_shapes=[pltpu.VMEM((tc, D_RNN), F32), pltpu.VMEM((tc, D_RNN), F32),
                        pltpu.VMEM((tc + SUBLANES, D_RNN), F32), pltpu.VMEM((tc, D_RNN), F32),
                        pltpu.VMEM((tc, D_RNN), F32), pltpu.VMEM((tc, D_RNN), BF16),
                        pltpu.VMEM(scan_shape, F32), pltpu.VMEM(scan_shape, F32),
                        pltpu.VMEM(scan_shape, F32), pltpu.VMEM(scan_shape, F32),
                        pltpu.VMEM((tc, D_RNN), BF16), pltpu.VMEM((1, D_RNN), F32)],
        compiler_params=pltpu.CompilerParams(dimension_semantics=("arbitrary",),
                                             vmem_limit_bytes=VMEM_LIMIT),
        name="mix_in",
    )(x2, *consts)


def _fox_attn_kernel(q_ref, k_ref, v_ref, f_ref, bf_ref, o_ref, cum, cum_t, km_s, vt_s, ck_s,
                     cq_s, sc_s, p_s, m_s, off_s, o0_s, *, s, tq):
    pair = pl.program_id(1)

    @pl.when(pair == 0)
    def _():
        c = jax.nn.log_sigmoid(f_ref[0] + bf_ref[...])
        row = lax.broadcasted_iota(jnp.int32, (s, LANES), 0)
        d = 1
        while d < s:
            c = c + jnp.where(row >= d, pltpu.roll(c, d, axis=0), 0.0)
            d *= 2
        c = c * LOG2E
        cum[...] = c
        cum_t[...] = c.T

    lane = lax.broadcasted_iota(jnp.int32, (1, LANES), 1)
    n_q = s // tq

    k = k_ref[0, 0]
    v_t = v_ref[0, 0].astype(F32).T
    v_row = lax.broadcasted_iota(jnp.int32, (LANES, 1), 0)
    for hh in range(2):
        head = 2 * pair + hh
        in_head = (lane >= hh * HEAD_DIM) & (lane < (hh + 1) * HEAD_DIM)
        km_s[hh] = jnp.where(in_head, k, jnp.zeros_like(k))
        own_rows = (v_row >= hh * HEAD_DIM) & (v_row < (hh + 1) * HEAD_DIM)
        vt_s[hh] = jnp.where(own_rows, v_t, 1.0).astype(BF16)
        ck = jnp.sum(jnp.where(lane == head, cum[...], 0.0), axis=-1, keepdims=True)
        ck_s[hh] = jnp.broadcast_to(ck, (s, LANES))
        cq_s[hh] = cum_t[pl.ds(head, 1), :]

    tasks = [(hh, i) for hh in range(2) for i in range(n_q)]

    def pieces(i, j):
        if j == i:
            return [(0, tq // 2, 0, tq), (tq // 2, tq, tq // 2, tq)]
        return [(0, tq, 0, tq)]

    def score_tile(n, j):
        hh, i = tasks[n]
        slot = n % 2
        ks = slice(j * tq, (j + 1) * tq)
        sc = lax.dot_general(km_s[hh, ks, :], q_ref[0, 0, i * tq:(i + 1) * tq, :],
                             (((1,), (1,)), ((), ())), preferred_element_type=F32)
        for n_piece, (r0, r1, l0, l1) in enumerate(pieces(i, j)):
            rows = slice(j * tq + r0, j * tq + r1)
            blk = jnp.concatenate([sc[r0:r1, c * LANES:(c + 1) * LANES] - ck_s[hh, rows, :]
                                   for c in range(l0 // LANES, l1 // LANES)], axis=1)
            if j == i:
                key = lax.broadcasted_iota(jnp.int32, blk.shape, 0) + r0
                qry = lax.broadcasted_iota(jnp.int32, blk.shape, 1) + l0
                blk = jnp.where(key <= qry, blk, -jnp.inf)
            sc_s[slot, rows, l0:l1] = blk
            part = blk[0:SUBLANES, :]
            for r in range(1, (r1 - r0) // SUBLANES):
                part = jnp.maximum(part, blk[r * SUBLANES:(r + 1) * SUBLANES, :])
            if j == 0 and n_piece == 0:
                m_s[slot] = part
            else:
                m_s[slot, :, l0:l1] = jnp.maximum(m_s[slot, :, l0:l1], part)

    def finish_scores(n):
        hh, i = tasks[n]
        slot = n % 2
        m = jnp.max(m_s[slot], axis=0, keepdims=True)
        cq_i = cq_s[hh, :, i * tq:(i + 1) * tq]
        off_s[slot] = (m + cq_i) - cq_i

    def exp_tile(n, j):
        slot = n % 2
        i = tasks[n][1]
        for r0, r1, l0, l1 in pieces(i, j):
            rows = slice(j * tq + r0, j * tq + r1)
            p_s[slot, rows, l0:l1] = jnp.exp2(sc_s[slot, rows, l0:l1]
                                              - off_s[slot, :, l0:l1]).astype(BF16)
        if j == i:
            p_s[slot, j * tq + tq // 2:(j + 1) * tq, 0:tq // 2] = jnp.zeros((tq // 2, tq // 2), BF16)

    def weighted_values(n):
        hh, i = tasks[n]
        slot = n % 2
        qs, qe = i * tq, (i + 1) * tq
        o = jnp.dot(vt_s[hh, :, 0:qe], p_s[slot, 0:qe, :], preferred_element_type=F32)
        sum_row = (1 - hh) * HEAD_DIM
        o = o / o[sum_row:sum_row + 1, :]
        if hh == 0:
            o0_s[i] = o
        else:
            row = lax.broadcasted_iota(jnp.int32, (LANES, 1), 0)
            o_ref[0, 0, qs:qe, :] =jnp.where(row < HEAD_DIM, o0_s[i], o).T.astype(BF16)

    for j in range(tasks[0][1] + 1):
        score_tile(0, j)
    finish_scores(0)
    for n in range(len(tasks)):
        n_exp = tasks[n][1] + 1
        n_score = tasks[n + 1][1] + 1 if n + 1 < len(tasks) else 0
        for j in range(max(n_exp, n_score)):
            if j < n_score:
                score_tile(n + 1, j)
            if j < n_exp:
                exp_tile(n, j)
        if n_score:
            finish_scores(n + 1)
        weighted_values(n)


def _fox_attn(q3, k3, v3, f3, bf, tq):
    _, b, s, _ = q3.shape
    blk = pl.BlockSpec((1, 1, s, LANES), lambda bi, pi: (pi, bi, 0, 0))
    return pl.pallas_call(
        functools.partial(_fox_attn_kernel, s=s, tq=tq),
        grid=(b, N_PAIRS),
        in_specs=[blk, blk, blk, pl.BlockSpec((1, s, LANES), lambda bi, pi: (bi, 0, 0)),
                  _const_spec((1, LANES))],
        out_specs=blk,
        out_shape=jax.ShapeDtypeStruct((N_PAIRS, b, s, LANES), BF16),
        scratch_shapes=[pltpu.VMEM((s, LANES), F32), pltpu.VMEM((LANES, s), F32),
                        pltpu.VMEM((2, s, LANES), BF16), pltpu.VMEM((2, LANES, s), BF16),
                        pltpu.VMEM((2, s, LANES), F32), pltpu.VMEM((2, 1, s), F32),
                        pltpu.VMEM((2, s, tq), F32), pltpu.VMEM((2, s, tq), BF16),
                        pltpu.VMEM((2, SUBLANES, tq), F32), pltpu.VMEM((2, 1, tq), F32),
                        pltpu.VMEM((s // tq, LANES, tq), F32)],
        compiler_params=pltpu.CompilerParams(dimension_semantics=("arbitrary", "arbitrary"),
                                             vmem_limit_bytes=VMEM_LIMIT),
        name="fox_attn",
    )(q3, k3, v3, f3, bf)


FF_CHUNK = 512
N_MERGE_STAGES = 6


def _merge_ffn_kernel(x_ref, pa_ref, yb_ref, gab_ref, bm_ref, wpb_ref, wout_ref, gpost_ref,
                      gpre2_ref, gpost2_ref, perm_ref, wup_ref, cw_ref, cb_ref, wdn_ref, o_ref,
                      tail, unperm, x1_s, hp_s, *, tm, tiles_per_seq):
    i = pl.program_id(0)
    seg = tm // SUBLANES
    pitch = _scan_pitch(tm)

    @pl.when(i == 0)
    def _():
        x1_s[...] = jnp.zeros_like(x1_s)
        hp_s[...] = jnp.zeros_like(hp_s)
        tail[...] = jnp.zeros_like(tail)

    @pl.when((i + tiles_per_seq - 1) % tiles_per_seq == 0)
    def _():
        tail[...] = jnp.zeros_like(tail)

    x1_prev = x1_s[...]
    hp = hp_s[...]

    def merge_stage(k, st):
        if k == 0:
            st["gates"] = jax.nn.sigmoid(gab_ref[...] + bm_ref[...])
        elif k == 1:
            yb = jnp.concatenate([yb_ref[p] for p in range(N_PAIRS)], axis=1)
            st["pb"] = jnp.dot(yb, wpb_ref[:, :D_MODEL], preferred_element_type=F32)
        elif k == 2:
            g = st.pop("gates")
            st["merged"] = (g[:, :D_MODEL] * pa_ref[...] + g[:, D_MODEL:] * st.pop("pb")).astype(BF16)
        elif k == 3:
            st["mix"] = jnp.dot(st.pop("merged"), wout_ref[:, :D_MODEL], preferred_element_type=F32)
        elif k == 4:
            x1_s[...] = x_ref[...] + _rms(st.pop("mix"), gpost_ref[...])
        elif k == 5:
            h = _rms(x1_s[...], gpre2_ref[...]).astype(BF16)
            hp_s[...] = jnp.dot(perm_ref[...], h, preferred_element_type=F32).astype(BF16)

    first = lax.broadcasted_iota(jnp.int32, (SUBLANES, FF_CHUNK), 0) == 0

    def up_proj(c):
        return [jnp.dot(hp, wup_ref[:, off + c * FF_CHUNK:off + (c + 1) * FF_CHUNK],
                        preferred_element_type=F32) for off in (0, D_FF)]

    def conv(u, cols):
        prev = [jnp.where(first, pltpu.roll(tail[g * SUBLANES:(g + 1) * SUBLANES, cols], 1, axis=0),
                          pltpu.roll(u[tm - (2 - g) * SUBLANES:tm - (1 - g) * SUBLANES, :], 1, axis=0))
                for g in range(CONV_F - 1)]
        tail[:, cols] = u[tm - (CONV_F - 1) * SUBLANES:, :]
        u1 = jnp.concatenate([prev[1], u[:tm - SUBLANES, :]], axis=0)
        u2 = jnp.concatenate([prev[0], prev[1], u[:tm - 2 * SUBLANES, :]], axis=0)
        return (cb_ref[:, cols] + cw_ref[2:3, cols] * u + cw_ref[1:2, cols] * u1
                + cw_ref[0:1, cols] * u2)

    n_chunks = D_FF // FF_CHUNK
    stages = {}
    acc = jnp.zeros((tm, D_MODEL), F32)
    u_next = up_proj(0)
    for c in range(n_chunks):
        u_gate, u_val = u_next
        if c + 1 < n_chunks:
            u_next = up_proj(c + 1)
        for k in range(c * N_MERGE_STAGES // n_chunks, (c + 1) * N_MERGE_STAGES // n_chunks):
            merge_stage(k, stages)
        gate = conv(u_gate, slice(c * FF_CHUNK, (c + 1) * FF_CHUNK))
        val = conv(u_val, slice(D_FF + c * FF_CHUNK, D_FF + (c + 1) * FF_CHUNK))
        act = (jax.nn.gelu(gate) * val).astype(BF16)
        acc = acc + jnp.dot(act, wdn_ref[c * FF_CHUNK:(c + 1) * FF_CHUNK, :D_MODEL],
                            preferred_element_type=F32)

    for j in range(seg):
        for c in range(D_MODEL // LANES):
            unperm[c, pl.ds(j, SUBLANES, stride=pitch), :] = (
                acc[j * SUBLANES:(j + 1) * SUBLANES, c * LANES:(c + 1) * LANES])
    ffn = jnp.concatenate(
        [jnp.concatenate([unperm[c, s * pitch:s * pitch + seg, :] for s in range(SUBLANES)], axis=0)
         for c in range(D_MODEL // LANES)], axis=1)
    o_ref[...] = x1_prev + _rms(ffn, gpost2_ref[...])


def _merge_ffn(x2, pa2, yb2, gab2, bm, wpb, wout, gpost, gpre2, gpost2, wup, cw, cb, wdn,
               tm, seq):
    t = x2.shape[0]
    n_tiles = t // tm
    row = lambda n: pl.BlockSpec((tm, n), lambda i: (jnp.minimum(i, n_tiles - 1), 0))
    r = jnp.arange(tm)
    perm = (r[None, :] == ((r % SUBLANES) * (tm // SUBLANES) + r // SUBLANES)[:, None]).astype(BF16)
    return pl.pallas_call(
        functools.partial(_merge_ffn_kernel, tm=tm, tiles_per_seq=seq // tm),
        grid=(n_tiles + 1,),
        in_specs=[row(D_MODEL), row(D_MODEL),
                  pl.BlockSpec((N_PAIRS, tm, LANES), lambda i: (0, jnp.minimum(i, n_tiles - 1), 0)),
                  row(2 * D_MODEL),
                  _const_spec((1, 2 * D_MODEL)), _const_spec((D_ATT, _padded(D_MODEL))),
                  _const_spec((D_MODEL, _padded(D_MODEL))), _const_spec((1, D_MODEL)),
                  _const_spec((1, D_MODEL)), _const_spec((1, D_MODEL)), _const_spec((tm, tm)),
                  _const_spec((D_MODEL, _padded(2 * D_FF))), _const_spec((CONV_F, 2 * D_FF)),
                  _const_spec((1, 2 * D_FF)), _const_spec((D_FF, _padded(D_MODEL)))],
        out_specs=pl.BlockSpec((tm, D_MODEL), lambda i: (jnp.maximum(i - 1, 0), 0)),
        out_shape=jax.ShapeDtypeStruct((t, D_MODEL), F32),
        scratch_shapes=[pltpu.VMEM(((CONV_F - 1) * SUBLANES, 2 * D_FF), F32),
                        pltpu.VMEM((D_MODEL // LANES, SUBLANES * _scan_pitch(tm), LANES), F32),
                        pltpu.VMEM((tm, D_MODEL), F32), pltpu.VMEM((tm, D_MODEL), BF16)],
        compiler_params=pltpu.CompilerParams(dimension_semantics=("arbitrary",),
                                             vmem_limit_bytes=VMEM_LIMIT),
        name="merge_ffn",
    )(x2, pa2, yb2, gab2, bm, wpb, wout, gpost, gpre2, gpost2, perm, wup, cw, cb, wdn)


def _block_diag(w):
    h, n, _ = w.shape
    col = jnp.arange(h * n)
    tile = (col[None, :] % n == jnp.arange(n)[:, None]).astype(w.dtype)
    rep = jnp.dot(w.reshape(h * n, n), tile, precision=lax.Precision.HIGHEST)
    return jnp.where(col[:, None] // n == col[None, :] // n, rep, 0.0)


def _gate_windows(w_a, w_x):
    da, dx = _block_diag(w_a), _block_diag(w_x)
    tiles = []
    for j, ks in enumerate(GATE_STARTS):
        cols = slice(j * GATE_TILE, (j + 1) * GATE_TILE)
        tiles.append(jnp.concatenate([da[ks:ks + GATE_WIN, cols], dx[ks:ks + GATE_WIN, cols]],
                                     axis=1))
    return jnp.stack([_bf16_weight(t) for t in tiles])


def _layer(x, mix_norm_pre, mix_norm_post, w_in, conv_a_w, conv_a_b, w_rg_a, b_rg_a, w_rg_x,
           b_rg_x, lru_lambda, b_forget, b_merge, w_proj_a, w_proj_b, w_out, ffn_norm_pre,
           ffn_norm_post, w_up, conv_f_w, conv_f_b, w_down):
    bsz, seq, _ = x.shape
    t = bsz * seq
    row = lambda a: a.reshape(1, -1)

    w_main = _bf16_weight(w_in[:, :_OFF_F])
    w_gate = _bf16_weight(w_in[:, _OFF_G:])
    w_f = jnp.pad(w_in[:, _OFF_F:_OFF_G].astype(BF16), ((0, 0), (0, LANES - H_ATT)))
    bf = jnp.concatenate([b_forget, jnp.zeros((LANES - H_ATT,), b_forget.dtype)]).reshape(1, LANES)
    wg = _gate_windows(w_rg_a, w_rg_x)

    x2 = x.reshape(t, D_MODEL)
    q, k, v, gab, f, pa = _mix_in(x2, row(mix_norm_pre), w_main, w_gate, w_f, conv_a_w,
                                  row(conv_a_b), wg, row(b_rg_a), row(b_rg_x), row(lru_lambda),
                                  _bf16_weight(w_proj_a), tc=MIX_IN_ROWS, seq=seq)

    p4 = lambda a: a.reshape(N_PAIRS, bsz, seq, LANES)
    yb = _fox_attn(p4(q), p4(k), p4(v), f.reshape(bsz, seq, LANES), bf, tq=ATTN_Q_ROWS)

    out = _merge_ffn(x2, pa, yb.reshape(N_PAIRS, t, LANES), gab, row(b_merge),
                     _bf16_weight(w_proj_b), _bf16_weight(w_out), row(mix_norm_post),
                     row(ffn_norm_pre), row(ffn_norm_post), _bf16_weight(w_up), conv_f_w,
                     row(conv_f_b), _bf16_weight(w_down), tm=FFN_ROWS, seq=seq)
    return out.reshape(bsz, seq, D_MODEL)


def kernel(x, mix_norm_pre, mix_norm_post, w_in, conv_a_w, conv_a_b, w_rg_a, b_rg_a, w_rg_x, b_rg_x, lru_lambda, b_forget, b_merge, w_proj_a, w_proj_b, w_out, ffn_norm_pre, ffn_norm_post, w_up, conv_f_w, conv_f_b, w_down):
    depth = w_in.shape[0]
    for layer in range(depth):
        x = _layer(x, mix_norm_pre[layer], mix_norm_post[layer], w_in[layer], conv_a_w[layer],
                   conv_a_b[layer], w_rg_a[layer], b_rg_a[layer], w_rg_x[layer], b_rg_x[layer],
                   lru_lambda[layer], b_forget[layer], b_merge[layer], w_proj_a[layer],
                   w_proj_b[layer], w_out[layer], ffn_norm_pre[layer], ffn_norm_post[layer],
                   w_up[layer], conv_f_w[layer], conv_f_b[layer], w_down[layer])
    return x
```

```python
import functools
import math

import jax
import jax.numpy as jnp
from jax import lax
from jax.experimental import pallas as pl
from jax.experimental.pallas import tpu as pltpu

D_MODEL = 1024
D_RNN = 1280
H_RNN = 16
RNN_BLOCK = D_RNN // H_RNN
CONV_A = 4
LRU_C = 8.0
H_ATT = 16
HEAD_DIM = 64
D_ATT = H_ATT * HEAD_DIM
N_PAIRS = H_ATT // 2
D_FF = 3 * D_MODEL
CONV_F = 3
RMS_EPS = 1e-6

LANES = 128
SUBLANES = 8
MXU_COLS = 256
VMEM_LIMIT = 56 * 1024 * 1024

MIX_IN_ROWS = 256
ATTN_Q_ROWS = 256
ATTN_PAIRS = 2
FFN_ROWS = 256

LOG2E = math.log2(math.e)

F32 = jnp.float32
BF16 = jnp.bfloat16

_OFF_XA = 0
_OFF_GA = _OFF_XA + D_RNN
_OFF_Q = _OFF_GA + D_RNN
_OFF_K = _OFF_Q + D_ATT
_OFF_V = _OFF_K + D_ATT
_OFF_F = _OFF_V + D_ATT
_OFF_G = _OFF_F + H_ATT

GATE_TILE = MXU_COLS
GATE_WIN = 2 * MXU_COLS
N_GATE_TILES = D_RNN // GATE_TILE


def _gate_window_start(j):
    lo = (j * GATE_TILE // RNN_BLOCK) * RNN_BLOCK
    start = min((lo // LANES) * LANES, D_RNN - GATE_WIN)
    hi = -(-((j + 1) * GATE_TILE) // RNN_BLOCK) * RNN_BLOCK
    assert start <= lo and hi <= start + GATE_WIN
    return start


GATE_STARTS = tuple(_gate_window_start(j) for j in range(N_GATE_TILES))


def _rms(x, gain):
    y = x * lax.rsqrt(jnp.mean(x * x, axis=-1, keepdims=True) + RMS_EPS)
    return y * gain


def _padded(n):
    return n + LANES if (n // 32) % 8 == 0 else n


def _bf16_weight(w):
    k, n = w.shape
    return jnp.pad(w, ((0, 0), (0, _padded(n) - n))).astype(BF16)


def _const_spec(shape):
    nd = len(shape)
    return pl.BlockSpec(shape, lambda *_: (0,) * nd, pipeline_mode=pl.Buffered(1))


def _scan_pitch(tc):
    seg = tc // SUBLANES
    return seg + SUBLANES if (seg // SUBLANES) % 2 == 0 else seg


def _mix_in_kernel(x_ref, g_ref, w_ref, wgate_ref, wf_ref, cw_ref, cb_ref, wg_ref, ba_ref, bx_ref,
                   lam_ref, wp_ref, q_ref, k_ref, v_ref, gab_ref, f_ref, pa_ref,
                   xa_s, ga_in_s, xpad, ga_s, xc_s, xcb_s, a_s, b_s, h_s, p_s, y_s, hprev,
                   *, tc, tiles_per_seq):
    seg = tc // SUBLANES
    pitch = _scan_pitch(tc)
    n_lane = D_RNN // LANES
    step = pl.program_id(0)

    @pl.when(step == 0)
    def _():
        xa_s[...] = jnp.zeros_like(xa_s)
        ga_in_s[...] = jnp.zeros_like(ga_in_s)
        xpad[0:SUBLANES, :] = jnp.zeros((SUBLANES, D_RNN), F32)
        hprev[...] = jnp.zeros_like(hprev)

    @pl.when((step + tiles_per_seq - 1) % tiles_per_seq == 0)
    def _():
        xpad[0:SUBLANES, :] = jnp.zeros((SUBLANES, D_RNN), F32)
        hprev[...] = jnp.zeros_like(hprev)

    xpad[SUBLANES:SUBLANES + tc, :] = xa_s[...]
    ga_s[...] = ga_in_s[...]

    h = _rms(x_ref[...], g_ref[...]).astype(BF16)

    def mm(lo, hi):
        return jnp.dot(h, w_ref[:, lo:hi], preferred_element_type=F32)

    def proj_chunk(kind, c):
        lo, hi = c * MXU_COLS, (c + 1) * MXU_COLS

        def put_pairs(ref, val):
            for p in range(MXU_COLS // LANES):
                ref[c * (MXU_COLS // LANES) + p] = val[:, p * LANES:(p + 1) * LANES]

        if kind == "q":
            put_pairs(q_ref, (mm(_OFF_Q + lo, _OFF_Q + hi)
                              * (LOG2E / math.sqrt(HEAD_DIM))).astype(BF16))
        elif kind == "k":
            put_pairs(k_ref, mm(_OFF_K + lo, _OFF_K + hi).astype(BF16))
        elif kind == "v":
            put_pairs(v_ref, mm(_OFF_V + lo, _OFF_V + hi).astype(BF16))
        elif kind == "g":
            gab_ref[:, lo:hi] = jnp.dot(h, wgate_ref[:, lo:hi], preferred_element_type=F32)
        elif kind == "xa":
            xa_s[:, lo:hi] = mm(_OFF_XA + lo, _OFF_XA + hi)
        elif kind == "ga":
            ga_in_s[:, lo:hi] = mm(_OFF_GA + lo, _OFF_GA + hi)
        else:
            f_ref[...] = jnp.dot(h, wf_ref[...], preferred_element_type=F32)

    pending = ([("q", c) for c in range(D_ATT // MXU_COLS)]
               + [("k", c) for c in range(D_ATT // MXU_COLS)]
               + [("v", c) for c in range(D_ATT // MXU_COLS)]
               + [("g", c) for c in range(2 * D_MODEL // MXU_COLS)] + [("f", 0)]
               + [("xa", c) for c in range(D_RNN // MXU_COLS)]
               + [("ga", c) for c in range(D_RNN // MXU_COLS)])

    def emit(n):
        for _ in range(min(n, len(pending))):
            proj_chunk(*pending.pop(0))

    for j in range(N_GATE_TILES):
        cols = slice(j * GATE_TILE, (j + 1) * GATE_TILE)
        xc = cb_ref[:, cols] + cw_ref[CONV_A - 1:CONV_A, cols] * xpad[SUBLANES:SUBLANES + tc, cols]
        for kk in range(CONV_A - 1):
            sh = CONV_A - 1 - kk
            xc = xc + cw_ref[kk:kk + 1, cols] * xpad[SUBLANES - sh:SUBLANES - sh + tc, cols]
        xc_s[:, cols] = xc
        xcb_s[:, cols] = xc.astype(BF16)
        emit(1)
    xpad[0:SUBLANES, :] = xpad[tc:tc + SUBLANES, :]

    sp = jax.nn.softplus(-lam_ref[...])
    lane_chunks = GATE_TILE // LANES
    for j in range(N_GATE_TILES):
        ks = GATE_STARTS[j]
        cols = slice(j * GATE_TILE, (j + 1) * GATE_TILE)
        g = jnp.dot(xcb_s[:, ks:ks + GATE_WIN], wg_ref[j, :, :2 * GATE_TILE],
                    preferred_element_type=F32)
        r = jax.nn.sigmoid(g[:, :GATE_TILE] + ba_ref[:, cols])
        i = jax.nn.sigmoid(g[:, GATE_TILE:] + bx_ref[:, cols])
        log_a = -LRU_C * r * sp[:, cols]
        a = jnp.exp(log_a)
        b = jnp.sqrt(-jnp.tanh(log_a) * (a * a + 1.0)) * (i * xc_s[:, cols])
        for c in range(lane_chunks):
            for s in range(SUBLANES):
                dst = slice(s * pitch, s * pitch + seg)
                a_s[j * lane_chunks + c, dst, :] = a[s * seg:(s + 1) * seg, c * LANES:(c + 1) * LANES]
                b_s[j * lane_chunks + c, dst, :] = b[s * seg:(s + 1) * seg, c * LANES:(c + 1) * LANES]
        emit(3)

    h8 = [jnp.zeros((SUBLANES, LANES), F32) for _ in range(n_lane)]
    p8 = [jnp.ones((SUBLANES, LANES), F32) for _ in range(n_lane)]
    for j in range(seg):
        rows = pl.ds(j, SUBLANES, stride=pitch)
        for c in range(n_lane):
            a_j = a_s[c, rows, :]
            h8[c] = a_j * h8[c] + b_s[c, rows, :]
            p8[c] = p8[c] * a_j
            h_s[c, rows, :] = h8[c]
            p_s[c, rows, :] = p8[c]
        if j % 4 == 3:
            emit(1)

    pa = jnp.zeros((tc, D_MODEL), F32)
    for c in range(n_lane):
        cols = slice(c * LANES, (c + 1) * LANES)
        start = hprev[:, cols]
        for s in range(SUBLANES):
            rows = slice(s * seg, (s + 1) * seg)
            src = slice(s * pitch, s * pitch + seg)
            hfin = h_s[c, src, :] + p_s[c, src, :] * start
            y_s[rows, cols] = (jax.nn.gelu(ga_s[rows, cols]) * hfin).astype(BF16)
            start = h8[c][s:s + 1, :] + p8[c][s:s + 1, :] * start
        hprev[:, cols] = start
        emit(1)
        if c % lane_chunks == lane_chunks - 1:
            slab = slice((c - lane_chunks + 1) * LANES, (c + 1) * LANES)
            pa = pa + jnp.dot(y_s[:, slab], wp_ref[slab, :D_MODEL], preferred_element_type=F32)
    emit(len(pending))
    pa_ref[...] = pa


def _mix_in(x2, gain, w_main, w_gate, w_f, cw, cb, wg, ba, bx, lam, wp, tc, seq):
    t = x2.shape[0]
    n_tiles = t // tc
    row = lambda n: pl.BlockSpec((tc, n), lambda i: (jnp.minimum(i, n_tiles - 1), 0))
    pairs = pl.BlockSpec((N_PAIRS, tc, LANES), lambda i: (0, jnp.minimum(i, n_tiles - 1), 0))
    scan_shape = (D_RNN // LANES, SUBLANES * _scan_pitch(tc), LANES)
    consts = (gain, w_main, w_gate, w_f, cw, cb, wg, ba, bx, lam, wp)
    return pl.pallas_call(
        functools.partial(_mix_in_kernel, tc=tc, tiles_per_seq=seq // tc),
        grid=(n_tiles + 1,),
        in_specs=[row(D_MODEL)] + [_const_spec(c.shape) for c in consts],
        out_specs=[pairs, pairs, pairs, row(2 * D_MODEL), row(LANES),
                   pl.BlockSpec((tc, D_MODEL), lambda i: (jnp.maximum(i - 1, 0), 0))],
        out_shape=[jax.ShapeDtypeStruct((N_PAIRS, t, LANES), BF16),
                   jax.ShapeDtypeStruct((N_PAIRS, t, LANES), BF16),
                   jax.ShapeDtypeStruct((N_PAIRS, t, LANES), BF16),
                   jax.ShapeDtypeStruct((t, 2 * D_MODEL), F32),
                   jax.ShapeDtypeStruct((t, LANES), F32),
                   jax.ShapeDtypeStruct((t, D_MODEL), F32)],
        scratch_shapes=[pltpu.VMEM((tc, D_RNN), F32), pltpu.VMEM((tc, D_RNN), F32),
                        pltpu.VMEM((tc + SUBLANES, D_RNN), F32), pltpu.VMEM((tc, D_RNN), F32),
                        pltpu.VMEM((tc, D_RNN), F32), pltpu.VMEM((tc, D_RNN), BF16),
                        pltpu.VMEM(scan_shape, F32), pltpu.VMEM(scan_shape, F32),
                        pltpu.VMEM(scan_shape, F32), pltpu.VMEM(scan_shape, F32),
                        pltpu.VMEM((tc, D_RNN), BF16), pltpu.VMEM((1, D_RNN), F32)],
        compiler_params=pltpu.CompilerParams(dimension_semantics=("arbitrary",),
                                             vmem_limit_bytes=VMEM_LIMIT),
        name="mix_in",
    )(x2, *consts)


def _fox_attn_kernel(q_ref, k_ref, v_ref, f_ref, bf_ref, o_ref, cum, cum_t, km_s, vt_s, ck_s,
                     cq_s, sc_s, p_s, m_s, off_s, o0_s, *, s, tq):
    pair = pl.program_id(1)

    @pl.when(pair == 0)
    def _():
        c = jax.nn.log_sigmoid(f_ref[0] + bf_ref[...])
        row = lax.broadcasted_iota(jnp.int32, (s, LANES), 0)
        d = 1
        while d < s:
            c = c + jnp.where(row >= d, pltpu.roll(c, d, axis=0), 0.0)
            d *= 2
        c = c * LOG2E
        cum[...] = c
        cum_t[...] = c.T

    lane = lax.broadcasted_iota(jnp.int32, (1, LANES), 1)
    n_q = s // tq

    n_pairs = q_ref.shape[0]
    v_row = lax.broadcasted_iota(jnp.int32, (LANES, 1), 0)
    for hs in range(2 * n_pairs):
        pp, hh = hs // 2, hs % 2
        head = 2 * (n_pairs * pair + pp) + hh
        k = k_ref[pp, 0]
        v_t = v_ref[pp, 0].astype(F32).T
        in_head = (lane >= hh * HEAD_DIM) & (lane < (hh + 1) * HEAD_DIM)
        km_s[hs] = jnp.where(in_head, k, jnp.zeros_like(k))
        own_rows = (v_row >= hh * HEAD_DIM) & (v_row < (hh + 1) * HEAD_DIM)
        vt_s[hs] = jnp.where(own_rows, v_t, 1.0).astype(BF16)
        ck = jnp.sum(jnp.where(lane == head, cum[...], 0.0), axis=-1, keepdims=True)
        ck_s[hs] = jnp.broadcast_to(ck, (s, LANES))
        cq_s[hs] = cum_t[pl.ds(head, 1), :]

    tasks = [(hs, i) for hs in range(2 * n_pairs) for i in range(n_q)]

    def pieces(i, j):
        if j == i:
            return [(0, tq // 2, 0, tq), (tq // 2, tq, tq // 2, tq)]
        return [(0, tq, 0, tq)]

    def score_tile(n, j):
        hh, i = tasks[n]
        slot = n % 2
        ks = slice(j * tq, (j + 1) * tq)
        sc = lax.dot_general(km_s[hh, ks, :], q_ref[hh // 2, 0, i * tq:(i + 1) * tq, :],
                             (((1,), (1,)), ((), ())), preferred_element_type=F32)
        for n_piece, (r0, r1, l0, l1) in enumerate(pieces(i, j)):
            rows = slice(j * tq + r0, j * tq + r1)
            blk = jnp.concatenate([sc[r0:r1, c * LANES:(c + 1) * LANES] - ck_s[hh, rows, :]
                                   for c in range(l0 // LANES, l1 // LANES)], axis=1)
            if j == i:
                key = lax.broadcasted_iota(jnp.int32, blk.shape, 0) + r0
                qry = lax.broadcasted_iota(jnp.int32, blk.shape, 1) + l0
                blk = jnp.where(key <= qry, blk, -jnp.inf)
            sc_s[slot, rows, l0:l1] = blk
            part = blk[0:SUBLANES, :]
            for r in range(1, (r1 - r0) // SUBLANES):
                part = jnp.maximum(part, blk[r * SUBLANES:(r + 1) * SUBLANES, :])
            if j == 0 and n_piece == 0:
                m_s[slot] = part
            else:
                m_s[slot, :, l0:l1] = jnp.maximum(m_s[slot, :, l0:l1], part)

    def finish_scores(n):
        hh, i = tasks[n]
        slot = n % 2
        m = jnp.max(m_s[slot], axis=0, keepdims=True)
        cq_i = cq_s[hh, :, i * tq:(i + 1) * tq]
        off_s[slot] = (m + cq_i) - cq_i

    def exp_tile(n, j):
        slot = n % 2
        i = tasks[n][1]
        for r0, r1, l0, l1 in pieces(i, j):
            rows = slice(j * tq + r0, j * tq + r1)
            p_s[slot, rows, l0:l1] = jnp.exp2(sc_s[slot, rows, l0:l1]
                                              - off_s[slot, :, l0:l1]).astype(BF16)
        if j == i:
            p_s[slot, j * tq + tq // 2:(j + 1) * tq, 0:tq // 2] = jnp.zeros((tq // 2, tq // 2), BF16)

    def weighted_values(n):
        hh, i = tasks[n]
        slot = n % 2
        qs, qe = i * tq, (i + 1) * tq
        o = jnp.dot(vt_s[hh, :, 0:qe], p_s[slot, 0:qe, :], preferred_element_type=F32)
        sum_row = (1 - hh % 2) * HEAD_DIM
        o = o / o[sum_row:sum_row + 1, :]
        if hh % 2 == 0:
            o0_s[i] = o
        else:
            row = lax.broadcasted_iota(jnp.int32, (LANES, 1), 0)
            o_ref[hh // 2, 0, qs:qe, :] = jnp.where(row < HEAD_DIM, o0_s[i], o).T.astype(BF16)

    for j in range(tasks[0][1] + 1):
        score_tile(0, j)
    finish_scores(0)
    for n in range(len(tasks)):
        n_exp = tasks[n][1] + 1
        n_score = tasks[n + 1][1] + 1 if n + 1 < len(tasks) else 0
        for j in range(max(n_exp, n_score)):
            if j < n_score:
                score_tile(n + 1, j)
            if j < n_exp:
                exp_tile(n, j)
        if n_score:
            finish_scores(n + 1)
        weighted_values(n)


def _fox_attn(q3, k3, v3, f3, bf, tq):
    _, b, s, _ = q3.shape
    blk = pl.BlockSpec((ATTN_PAIRS, 1, s, LANES), lambda bi, pi: (pi, bi, 0, 0))
    nh = 2 * ATTN_PAIRS
    return pl.pallas_call(
        functools.partial(_fox_attn_kernel, s=s, tq=tq),
        grid=(b, N_PAIRS // ATTN_PAIRS),
        in_specs=[blk, blk, blk, pl.BlockSpec((1, s, LANES), lambda bi, pi: (bi, 0, 0)),
                  _const_spec((1, LANES))],
        out_specs=blk,
        out_shape=jax.ShapeDtypeStruct((N_PAIRS, b, s, LANES), BF16),
        scratch_shapes=[pltpu.VMEM((s, LANES), F32), pltpu.VMEM((LANES, s), F32),
                        pltpu.VMEM((nh, s, LANES), BF16), pltpu.VMEM((nh, LANES, s), BF16),
                        pltpu.VMEM((nh, s, LANES), F32), pltpu.VMEM((nh, 1, s), F32),
                        pltpu.VMEM((2, s, tq), F32), pltpu.VMEM((2, s, tq), BF16),
                        pltpu.VMEM((2, SUBLANES, tq), F32), pltpu.VMEM((2, 1, tq), F32),
                        pltpu.VMEM((s // tq, LANES, tq), F32)],
        compiler_params=pltpu.CompilerParams(dimension_semantics=("arbitrary", "arbitrary"),
                                             vmem_limit_bytes=VMEM_LIMIT),
        name="fox_attn",
    )(q3, k3, v3, f3, bf)


FF_CHUNK = 512
N_MERGE_STAGES = 6


def _merge_ffn_kernel(x_ref, pa_ref, yb_ref, gab_ref, bm_ref, wpb_ref, wout_ref, gpost_ref,
                      gpre2_ref, gpost2_ref, perm_ref, wup_ref, cw_ref, cb_ref, wdn_ref, o_ref,
                      tail, unperm, x1_s, hp_s, *, tm, tiles_per_seq):
    i = pl.program_id(0)
    seg = tm // SUBLANES
    pitch = _scan_pitch(tm)

    @pl.when(i == 0)
    def _():
        x1_s[...] = jnp.zeros_like(x1_s)
        hp_s[...] = jnp.zeros_like(hp_s)
        tail[...] = jnp.zeros_like(tail)

    @pl.when((i + tiles_per_seq - 1) % tiles_per_seq == 0)
    def _():
        tail[...] = jnp.zeros_like(tail)

    x1_prev = x1_s[...]
    hp = hp_s[...]

    def merge_stage(k, st):
        if k == 0:
            st["gates"] = jax.nn.sigmoid(gab_ref[...] + bm_ref[...])
        elif k == 1:
            yb = jnp.concatenate([yb_ref[p] for p in range(N_PAIRS)], axis=1)
            st["pb"] = jnp.dot(yb, wpb_ref[:, :D_MODEL], preferred_element_type=F32)
        elif k == 2:
            g = st.pop("gates")
            st["merged"] = (g[:, :D_MODEL] * pa_ref[...] + g[:, D_MODEL:] * st.pop("pb")).astype(BF16)
        elif k == 3:
            st["mix"] = jnp.dot(st.pop("merged"), wout_ref[:, :D_MODEL], preferred_element_type=F32)
        elif k == 4:
            x1_s[...] = x_ref[...] + _rms(st.pop("mix"), gpost_ref[...])
        elif k == 5:
            h = _rms(x1_s[...], gpre2_ref[...]).astype(BF16)
            hp_s[...] = jnp.dot(perm_ref[...], h, preferred_element_type=F32).astype(BF16)

    first = lax.broadcasted_iota(jnp.int32, (SUBLANES, FF_CHUNK), 0) == 0

    def up_proj(c):
        return [jnp.dot(hp, wup_ref[:, off + c * FF_CHUNK:off + (c + 1) * FF_CHUNK],
                        preferred_element_type=F32) for off in (0, D_FF)]

    def conv(u, cols):
        prev = [jnp.where(first, pltpu.roll(tail[g * SUBLANES:(g + 1) * SUBLANES, cols], 1, axis=0),
                          pltpu.roll(u[tm - (2 - g) * SUBLANES:tm - (1 - g) * SUBLANES, :], 1, axis=0))
                for g in range(CONV_F - 1)]
        tail[:, cols] = u[tm - (CONV_F - 1) * SUBLANES:, :]
        u1 = jnp.concatenate([prev[1], u[:tm - SUBLANES, :]], axis=0)
        u2 = jnp.concatenate([prev[0], prev[1], u[:tm - 2 * SUBLANES, :]], axis=0)
        return (cb_ref[:, cols] + cw_ref[2:3, cols] * u + cw_ref[1:2, cols] * u1
                + cw_ref[0:1, cols] * u2)

    n_chunks = D_FF // FF_CHUNK
    stages = {}
    acc = jnp.zeros((tm, D_MODEL), F32)
    u_next = up_proj(0)
    for c in range(n_chunks):
        u_gate, u_val = u_next
        if c + 1 < n_chunks:
            u_next = up_proj(c + 1)
        for k in range(c * N_MERGE_STAGES // n_chunks, (c + 1) * N_MERGE_STAGES // n_chunks):
            merge_stage(k, stages)
        gate = conv(u_gate, slice(c * FF_CHUNK, (c + 1) * FF_CHUNK))
        val = conv(u_val, slice(D_FF + c * FF_CHUNK, D_FF + (c + 1) * FF_CHUNK))
        act = (jax.nn.gelu(gate) * val).astype(BF16)
        acc = acc + jnp.dot(act, wdn_ref[c * FF_CHUNK:(c + 1) * FF_CHUNK, :D_MODEL],
                            preferred_element_type=F32)

    for j in range(seg):
        for c in range(D_MODEL // LANES):
            unperm[c, pl.ds(j, SUBLANES, stride=pitch), :] = (
                acc[j * SUBLANES:(j + 1) * SUBLANES, c * LANES:(c + 1) * LANES])
    ffn = jnp.concatenate(
        [jnp.concatenate([unperm[c, s * pitch:s * pitch + seg, :] for s in range(SUBLANES)], axis=0)
         for c in range(D_MODEL // LANES)], axis=1)
    o_ref[...] = x1_prev + _rms(ffn, gpost2_ref[...])


def _merge_ffn(x2, pa2, yb2, gab2, bm, wpb, wout, gpost, gpre2, gpost2, wup, cw, cb, wdn,
               tm, seq):
    t = x2.shape[0]
    n_tiles = t // tm
    row = lambda n: pl.BlockSpec((tm, n), lambda i: (jnp.minimum(i, n_tiles - 1), 0))
    r = jnp.arange(tm)
    perm = (r[None, :] == ((r % SUBLANES) * (tm // SUBLANES) + r // SUBLANES)[:, None]).astype(BF16)
    return pl.pallas_call(
        functools.partial(_merge_ffn_kernel, tm=tm, tiles_per_seq=seq // tm),
        grid=(n_tiles + 1,),
        in_specs=[row(D_MODEL), row(D_MODEL),
                  pl.BlockSpec((N_PAIRS, tm, LANES), lambda i: (0, jnp.minimum(i, n_tiles - 1), 0)),
                  row(2 * D_MODEL),
                  _const_spec((1, 2 * D_MODEL)), _const_spec((D_ATT, _padded(D_MODEL))),
                  _const_spec((D_MODEL, _padded(D_MODEL))), _const_spec((1, D_MODEL)),
                  _const_spec((1, D_MODEL)), _const_spec((1, D_MODEL)), _const_spec((tm, tm)),
                  _const_spec((D_MODEL, _padded(2 * D_FF))), _const_spec((CONV_F, 2 * D_FF)),
                  _const_spec((1, 2 * D_FF)), _const_spec((D_FF, _padded(D_MODEL)))],
        out_specs=pl.BlockSpec((tm, D_MODEL), lambda i: (jnp.maximum(i - 1, 0), 0)),
        out_shape=jax.ShapeDtypeStruct((t, D_MODEL), F32),
        scratch_shapes=[pltpu.VMEM(((CONV_F - 1) * SUBLANES, 2 * D_FF), F32),
                        pltpu.VMEM((D_MODEL // LANES, SUBLANES * _scan_pitch(tm), LANES), F32),
                        pltpu.VMEM((tm, D_MODEL), F32), pltpu.VMEM((tm, D_MODEL), BF16)],
        compiler_params=pltpu.CompilerParams(dimension_semantics=("arbitrary",),
                                             vmem_limit_bytes=VMEM_LIMIT),
        name="merge_ffn",
    )(x2, pa2, yb2, gab2, bm, wpb, wout, gpost, gpre2, gpost2, perm, wup, cw, cb, wdn)


def _block_diag(w):
    h, n, _ = w.shape
    col = jnp.arange(h * n)
    tile = (col[None, :] % n == jnp.arange(n)[:, None]).astype(w.dtype)
    rep = jnp.dot(w.reshape(h * n, n), tile, precision=lax.Precision.HIGHEST)
    return jnp.where(col[:, None] // n == col[None, :] // n, rep, 0.0)


def _gate_windows(w_a, w_x):
    da, dx = _block_diag(w_a), _block_diag(w_x)
    tiles = []
    for j, ks in enumerate(GATE_STARTS):
        cols = slice(j * GATE_TILE, (j + 1) * GATE_TILE)
        tiles.append(jnp.concatenate([da[ks:ks + GATE_WIN, cols], dx[ks:ks + GATE_WIN, cols]],
                                     axis=1))
    return jnp.stack([_bf16_weight(t) for t in tiles])


def _layer(x, mix_norm_pre, mix_norm_post, w_in, conv_a_w, conv_a_b, w_rg_a, b_rg_a, w_rg_x,
           b_rg_x, lru_lambda, b_forget, b_merge, w_proj_a, w_proj_b, w_out, ffn_norm_pre,
           ffn_norm_post, w_up, conv_f_w, conv_f_b, w_down):
    bsz, seq, _ = x.shape
    t = bsz * seq
    row = lambda a: a.reshape(1, -1)

    w_main = _bf16_weight(w_in[:, :_OFF_F])
    w_gate = _bf16_weight(w_in[:, _OFF_G:])
    w_f = jnp.pad(w_in[:, _OFF_F:_OFF_G].astype(BF16), ((0, 0), (0, LANES - H_ATT)))
    bf = jnp.concatenate([b_forget, jnp.zeros((LANES - H_ATT,), b_forget.dtype)]).reshape(1, LANES)
    wg = _gate_windows(w_rg_a, w_rg_x)

    x2 = x.reshape(t, D_MODEL)
    q, k, v, gab, f, pa = _mix_in(x2, row(mix_norm_pre), w_main, w_gate, w_f, conv_a_w,
                                  row(conv_a_b), wg, row(b_rg_a), row(b_rg_x), row(lru_lambda),
                                  _bf16_weight(w_proj_a), tc=MIX_IN_ROWS, seq=seq)

    p4 = lambda a: a.reshape(N_PAIRS, bsz, seq, LANES)
    yb = _fox_attn(p4(q), p4(k), p4(v), f.reshape(bsz, seq, LANES), bf, tq=ATTN_Q_ROWS)

    out = _merge_ffn(x2, pa, yb.reshape(N_PAIRS, t, LANES), gab, row(b_merge),
                     _bf16_weight(w_proj_b), _bf16_weight(w_out), row(mix_norm_post),
                     row(ffn_norm_pre), row(ffn_norm_post), _bf16_weight(w_up), conv_f_w,
                     row(conv_f_b), _bf16_weight(w_down), tm=FFN_ROWS, seq=seq)
    return out.reshape(bsz, seq, D_MODEL)


def kernel(x, mix_norm_pre, mix_norm_post, w_in, conv_a_w, conv_a_b, w_rg_a, b_rg_a, w_rg_x, b_rg_x, lru_lambda, b_forget, b_merge, w_proj_a, w_proj_b, w_out, ffn_norm_pre, ffn_norm_post, w_up, conv_f_w, conv_f_b, w_down):
    depth = w_in.shape[0]
    for layer in range(depth):
        x = _layer(x, mix_norm_pre[layer], mix_norm_post[layer], w_in[layer], conv_a_w[layer],
                   conv_a_b[layer], w_rg_a[layer], b_rg_a[layer], w_rg_x[layer], b_rg_x[layer],
                   lru_lambda[layer], b_forget[layer], b_merge[layer], w_proj_a[layer],
                   w_proj_b[layer], w_out[layer], ffn_norm_pre[layer], ffn_norm_post[layer],
                   w_up[layer], conv_f_w[layer], conv_f_b[layer], w_down[layer])
    return x
```

```python
import functools
import math

import jax
import jax.numpy as jnp
from jax import lax
from jax.experimental import pallas as pl
from jax.experimental.pallas import tpu as pltpu

D_MODEL = 1024
D_RNN = 1280
H_RNN = 16
RNN_BLOCK = D_RNN // H_RNN
CONV_A = 4
LRU_C = 8.0
H_ATT = 16
HEAD_DIM = 64
D_ATT = H_ATT * HEAD_DIM
N_PAIRS = H_ATT // 2
D_FF = 3 * D_MODEL
CONV_F = 3
RMS_EPS = 1e-6

LANES = 128
SUBLANES = 8
MXU_COLS = 256
VMEM_LIMIT = 56 * 1024 * 1024

MIX_IN_ROWS = 256
ATTN_Q_ROWS = 256
ATTN_PAIRS = 4
FFN_ROWS = 256

LOG2E = math.log2(math.e)

F32 = jnp.float32
BF16 = jnp.bfloat16

_OFF_XA = 0
_OFF_GA = _OFF_XA + D_RNN
_OFF_Q = _OFF_GA + D_RNN
_OFF_K = _OFF_Q + D_ATT
_OFF_V = _OFF_K + D_ATT
_OFF_F = _OFF_V + D_ATT
_OFF_G = _OFF_F + H_ATT

GATE_TILE = MXU_COLS
GATE_WIN = 2 * MXU_COLS
N_GATE_TILES = D_RNN // GATE_TILE


def _gate_window_start(j):
    lo = (j * GATE_TILE // RNN_BLOCK) * RNN_BLOCK
    start = min((lo // LANES) * LANES, D_RNN - GATE_WIN)
    hi = -(-((j + 1) * GATE_TILE) // RNN_BLOCK) * RNN_BLOCK
    assert start <= lo and hi <= start + GATE_WIN
    return start


GATE_STARTS = tuple(_gate_window_start(j) for j in range(N_GATE_TILES))


def _rms(x, gain):
    y = x * lax.rsqrt(jnp.mean(x * x, axis=-1, keepdims=True) + RMS_EPS)
    return y * gain


def _padded(n):
    return n + LANES if (n // 32) % 8 == 0 else n


def _bf16_weight(w):
    k, n = w.shape
    return jnp.pad(w, ((0, 0), (0, _padded(n) - n))).astype(BF16)


def _const_spec(shape):
    nd = len(shape)
    return pl.BlockSpec(shape, lambda *_: (0,) * nd, pipeline_mode=pl.Buffered(1))


def _scan_pitch(tc):
    seg = tc // SUBLANES
    return seg + SUBLANES if (seg // SUBLANES) % 2 == 0 else seg


def _mix_in_kernel(x_ref, g_ref, w_ref, wgate_ref, wf_ref, cw_ref, cb_ref, wg_ref, ba_ref, bx_ref,
                   lam_ref, wp_ref, q_ref, k_ref, v_ref, gab_ref, f_ref, pa_ref,
                   xa_s, ga_in_s, xpad, ga_s, xc_s, xcb_s, a_s, b_s, h_s, p_s, y_s, hprev,
                   *, tc, tiles_per_seq):
    seg = tc // SUBLANES
    pitch = _scan_pitch(tc)
    n_lane = D_RNN // LANES
    step = pl.program_id(0)

    @pl.when(step == 0)
    def _():
        xa_s[...] = jnp.zeros_like(xa_s)
        ga_in_s[...] = jnp.zeros_like(ga_in_s)
        xpad[0:SUBLANES, :] = jnp.zeros((SUBLANES, D_RNN), F32)
        hprev[...] = jnp.zeros_like(hprev)

    @pl.when((step + tiles_per_seq - 1) % tiles_per_seq == 0)
    def _():
        xpad[0:SUBLANES, :] = jnp.zeros((SUBLANES, D_RNN), F32)
        hprev[...] = jnp.zeros_like(hprev)

    xpad[SUBLANES:SUBLANES + tc, :] = xa_s[...]
    ga_s[...] = ga_in_s[...]

    h = _rms(x_ref[...], g_ref[...]).astype(BF16)

    def mm(lo, hi):
        return jnp.dot(h, w_ref[:, lo:hi], preferred_element_type=F32)

    def proj_chunk(kind, c):
        lo, hi = c * MXU_COLS, (c + 1) * MXU_COLS

        def put_pairs(ref, val):
            for p in range(MXU_COLS // LANES):
                ref[c * (MXU_COLS // LANES) + p] = val[:, p * LANES:(p + 1) * LANES]

        if kind == "q":
            put_pairs(q_ref, (mm(_OFF_Q + lo, _OFF_Q + hi)
                              * (LOG2E / math.sqrt(HEAD_DIM))).astype(BF16))
        elif kind == "k":
            put_pairs(k_ref, mm(_OFF_K + lo, _OFF_K + hi).astype(BF16))
        elif kind == "v":
            put_pairs(v_ref, mm(_OFF_V + lo, _OFF_V + hi).astype(BF16))
        elif kind == "g":
            gab_ref[:, lo:hi] = jnp.dot(h, wgate_ref[:, lo:hi], preferred_element_type=F32)
        elif kind == "xa":
            xa_s[:, lo:hi] = mm(_OFF_XA + lo, _OFF_XA + hi)
        elif kind == "ga":
            ga_in_s[:, lo:hi] = mm(_OFF_GA + lo, _OFF_GA + hi)
        else:
            f_ref[...] = jnp.dot(h, wf_ref[...], preferred_element_type=F32)

    pending = ([("q", c) for c in range(D_ATT // MXU_COLS)]
               + [("k", c) for c in range(D_ATT // MXU_COLS)]
               + [("v", c) for c in range(D_ATT // MXU_COLS)]
               + [("g", c) for c in range(2 * D_MODEL // MXU_COLS)] + [("f", 0)]
               + [("xa", c) for c in range(D_RNN // MXU_COLS)]
               + [("ga", c) for c in range(D_RNN // MXU_COLS)])

    def emit(n):
        for _ in range(min(n, len(pending))):
            proj_chunk(*pending.pop(0))

    for j in range(N_GATE_TILES):
        cols = slice(j * GATE_TILE, (j + 1) * GATE_TILE)
        xc = cb_ref[:, cols] + cw_ref[CONV_A - 1:CONV_A, cols] * xpad[SUBLANES:SUBLANES + tc, cols]
        for kk in range(CONV_A - 1):
            sh = CONV_A - 1 - kk
            xc = xc + cw_ref[kk:kk + 1, cols] * xpad[SUBLANES - sh:SUBLANES - sh + tc, cols]
        xc_s[:, cols] = xc
        xcb_s[:, cols] = xc.astype(BF16)
        emit(1)
    xpad[0:SUBLANES, :] = xpad[tc:tc + SUBLANES, :]

    sp = jax.nn.softplus(-lam_ref[...])
    lane_chunks = GATE_TILE // LANES
    for j in range(N_GATE_TILES):
        ks = GATE_STARTS[j]
        cols = slice(j * GATE_TILE, (j + 1) * GATE_TILE)
        g = jnp.dot(xcb_s[:, ks:ks + GATE_WIN], wg_ref[j, :, :2 * GATE_TILE],
                    preferred_element_type=F32)
        r = jax.nn.sigmoid(g[:, :GATE_TILE] + ba_ref[:, cols])
        i = jax.nn.sigmoid(g[:, GATE_TILE:] + bx_ref[:, cols])
        log_a = -LRU_C * r * sp[:, cols]
        a = jnp.exp(log_a)
        b = jnp.sqrt(-jnp.tanh(log_a) * (a * a + 1.0)) * (i * xc_s[:, cols])
        for c in range(lane_chunks):
            for s in range(SUBLANES):
                dst = slice(s * pitch, s * pitch + seg)
                a_s[j * lane_chunks + c, dst, :] = a[s * seg:(s + 1) * seg, c * LANES:(c + 1) * LANES]
                b_s[j * lane_chunks + c, dst, :] = b[s * seg:(s + 1) * seg, c * LANES:(c + 1) * LANES]
        emit(3)

    h8 = [jnp.zeros((SUBLANES, LANES), F32) for _ in range(n_lane)]
    p8 = [jnp.ones((SUBLANES, LANES), F32) for _ in range(n_lane)]
    for j in range(seg):
        rows = pl.ds(j, SUBLANES, stride=pitch)
        for c in range(n_lane):
            a_j = a_s[c, rows, :]
            h8[c] = a_j * h8[c] + b_s[c, rows, :]
            p8[c] = p8[c] * a_j
            h_s[c, rows, :] = h8[c]
            p_s[c, rows, :] = p8[c]
        if j % 4 == 3:
            emit(1)

    pa = jnp.zeros((tc, D_MODEL), F32)
    for c in range(n_lane):
        cols = slice(c * LANES, (c + 1) * LANES)
        start = hprev[:, cols]
        for s in range(SUBLANES):
            rows = slice(s * seg, (s + 1) * seg)
            src = slice(s * pitch, s * pitch + seg)
            hfin = h_s[c, src, :] + p_s[c, src, :] * start
            y_s[rows, cols] = (jax.nn.gelu(ga_s[rows, cols]) * hfin).astype(BF16)
            start = h8[c][s:s + 1, :] + p8[c][s:s + 1, :] * start
        hprev[:, cols] = start
        emit(1)
        if c % lane_chunks == lane_chunks - 1:
            slab = slice((c - lane_chunks + 1) * LANES, (c + 1) * LANES)
            pa = pa + jnp.dot(y_s[:, slab], wp_ref[slab, :D_MODEL], preferred_element_type=F32)
    emit(len(pending))
    pa_ref[...] = pa


def _mix_in(x2, gain, w_main, w_gate, w_f, cw, cb, wg, ba, bx, lam, wp, tc, seq):
    t = x2.shape[0]
    n_tiles = t // tc
    row = lambda n: pl.BlockSpec((tc, n), lambda i: (jnp.minimum(i, n_tiles - 1), 0))
    pairs = pl.BlockSpec((N_PAIRS, tc, LANES), lambda i: (0, jnp.minimum(i, n_tiles - 1), 0))
    scan_shape = (D_RNN // LANES, SUBLANES * _scan_pitch(tc), LANES)
    consts = (gain, w_main, w_gate, w_f, cw, cb, wg, ba, bx, lam, wp)
    return pl.pallas_call(
        functools.partial(_mix_in_kernel, tc=tc, tiles_per_seq=seq // tc),
        grid=(n_tiles + 1,),
        in_specs=[row(D_MODEL)] + [_const_spec(c.shape) for c in consts],
        out_specs=[pairs, pairs, pairs, row(2 * D_MODEL), row(LANES),
                   pl.BlockSpec((tc, D_MODEL), lambda i: (jnp.maximum(i - 1, 0), 0))],
        out_shape=[jax.ShapeDtypeStruct((N_PAIRS, t, LANES), BF16),
                   jax.ShapeDtypeStruct((N_PAIRS, t, LANES), BF16),
                   jax.ShapeDtypeStruct((N_PAIRS, t, LANES), BF16),
                   jax.ShapeDtypeStruct((t, 2 * D_MODEL), F32),
                   jax.ShapeDtypeStruct((t, LANES), F32),
                   jax.ShapeDtypeStruct((t, D_MODEL), F32)],
        scratch_shapes=[pltpu.VMEM((tc, D_RNN), F32), pltpu.VMEM((tc, D_RNN), F32),
                        pltpu.VMEM((tc + SUBLANES, D_RNN), F32), pltpu.VMEM((tc, D_RNN), F32),
                        pltpu.VMEM((tc, D_RNN), F32), pltpu.VMEM((tc, D_RNN), BF16),
                        pltpu.VMEM(scan_shape, F32), pltpu.VMEM(scan_shape, F32),
                        pltpu.VMEM(scan_shape, F32), pltpu.VMEM(scan_shape, F32),
                        pltpu.VMEM((tc, D_RNN), BF16), pltpu.VMEM((1, D_RNN), F32)],
        compiler_params=pltpu.CompilerParams(dimension_semantics=("arbitrary",),
                                             vmem_limit_bytes=VMEM_LIMIT),
        name="mix_in",
    )(x2, *consts)


def _fox_attn_kernel(q_ref, k_ref, v_ref, f_ref, bf_ref, o_ref, cum, cum_t, km_s, vt_s, ck_s,
                     cq_s, sc_s, p_s, m_s, off_s, o0_s, *, s, tq):
    pair = pl.program_id(1)

    @pl.when(pair == 0)
    def _():
        c = jax.nn.log_sigmoid(f_ref[0] + bf_ref[...])
        row = lax.broadcasted_iota(jnp.int32, (s, LANES), 0)
        d = 1
        while d < s:
            c = c + jnp.where(row >= d, pltpu.roll(c, d, axis=0), 0.0)
            d *= 2
        c = c * LOG2E
        cum[...] = c
        cum_t[...] = c.T

    lane = lax.broadcasted_iota(jnp.int32, (1, LANES), 1)
    n_q = s // tq

    n_pairs = q_ref.shape[0]
    v_row = lax.broadcasted_iota(jnp.int32, (LANES, 1), 0)
    for hs in range(2 * n_pairs):
        pp, hh = hs // 2, hs % 2
        head = 2 * (n_pairs * pair + pp) + hh
        k = k_ref[pp, 0]
        v_t = v_ref[pp, 0].astype(F32).T
        in_head = (lane >= hh * HEAD_DIM) & (lane < (hh + 1) * HEAD_DIM)
        km_s[hs] = jnp.where(in_head, k, jnp.zeros_like(k))
        own_rows = (v_row >= hh * HEAD_DIM) & (v_row < (hh + 1) * HEAD_DIM)
        vt_s[hs] = jnp.where(own_rows, v_t, 1.0).astype(BF16)
        ck = jnp.sum(jnp.where(lane == head, cum[...], 0.0), axis=-1, keepdims=True)
        ck_s[hs] = jnp.broadcast_to(ck, (s, LANES))
        cq_s[hs] = cum_t[pl.ds(head, 1), :]

    tasks = [(hs, i) for hs in range(2 * n_pairs) for i in range(n_q)]

    def pieces(i, j):
        if j == i:
            return [(0, tq // 2, 0, tq), (tq // 2, tq, tq // 2, tq)]
        return [(0, tq, 0, tq)]

    def score_tile(n, j):
        hh, i = tasks[n]
        slot = n % 2
        ks = slice(j * tq, (j + 1) * tq)
        sc = lax.dot_general(km_s[hh, ks, :], q_ref[hh // 2, 0, i * tq:(i + 1) * tq, :],
                             (((1,), (1,)), ((), ())), preferred_element_type=F32)
        for n_piece, (r0, r1, l0, l1) in enumerate(pieces(i, j)):
            rows = slice(j * tq + r0, j * tq + r1)
            blk = jnp.concatenate([sc[r0:r1, c * LANES:(c + 1) * LANES] - ck_s[hh, rows, :]
                                   for c in range(l0 // LANES, l1 // LANES)], axis=1)
            if j == i:
                key = lax.broadcasted_iota(jnp.int32, blk.shape, 0) + r0
                qry = lax.broadcasted_iota(jnp.int32, blk.shape, 1) + l0
                blk = jnp.where(key <= qry, blk, -jnp.inf)
            sc_s[slot, rows, l0:l1] = blk
            part = blk[0:SUBLANES, :]
            for r in range(1, (r1 - r0) // SUBLANES):
                part = jnp.maximum(part, blk[r * SUBLANES:(r + 1) * SUBLANES, :])
            if j == 0 and n_piece == 0:
                m_s[slot] = part
            else:
                m_s[slot, :, l0:l1] = jnp.maximum(m_s[slot, :, l0:l1], part)

    def finish_scores(n):
        hh, i = tasks[n]
        slot = n % 2
        m = jnp.max(m_s[slot], axis=0, keepdims=True)
        cq_i = cq_s[hh, :, i * tq:(i + 1) * tq]
        off_s[slot] = (m + cq_i) - cq_i

    def exp_tile(n, j):
        slot = n % 2
        i = tasks[n][1]
        for r0, r1, l0, l1 in pieces(i, j):
            rows = slice(j * tq + r0, j * tq + r1)
            p_s[slot, rows, l0:l1] = jnp.exp2(sc_s[slot, rows, l0:l1]
                                              - off_s[slot, :, l0:l1]).astype(BF16)
        if j == i:
            p_s[slot, j * tq + tq // 2:(j + 1) * tq, 0:tq // 2] = jnp.zeros((tq // 2, tq // 2), BF16)

    def weighted_values(n):
        hh, i = tasks[n]
        slot = n % 2
        qs, qe = i * tq, (i + 1) * tq
        o = jnp.dot(vt_s[hh, :, 0:qe], p_s[slot, 0:qe, :], preferred_element_type=F32)
        sum_row = (1 - hh % 2) * HEAD_DIM
        o = o / o[sum_row:sum_row + 1, :]
        if hh % 2 == 0:
            o0_s[i] = o
        else:
            row = lax.broadcasted_iota(jnp.int32, (LANES, 1), 0)
            o_ref[hh // 2, 0, qs:qe, :] = jnp.where(row < HEAD_DIM, o0_s[i], o).T.astype(BF16)

    for j in range(tasks[0][1] + 1):
        score_tile(0, j)
    finish_scores(0)
    for n in range(len(tasks)):
        n_exp = tasks[n][1] + 1
        n_score = tasks[n + 1][1] + 1 if n + 1 < len(tasks) else 0
        for j in range(max(n_exp, n_score)):
            if j < n_score:
                score_tile(n + 1, j)
            if j < n_exp:
                exp_tile(n, j)
        if n_score:
            finish_scores(n + 1)
        weighted_values(n)


def _fox_attn(q3, k3, v3, f3, bf, tq):
    _, b, s, _ = q3.shape
    blk = pl.BlockSpec((ATTN_PAIRS, 1, s, LANES), lambda bi, pi: (pi, bi, 0, 0))
    nh = 2 * ATTN_PAIRS
    return pl.pallas_call(
        functools.partial(_fox_attn_kernel, s=s, tq=tq),
        grid=(b, N_PAIRS // ATTN_PAIRS),
        in_specs=[blk, blk, blk, pl.BlockSpec((1, s, LANES), lambda bi, pi: (bi, 0, 0)),
                  _const_spec((1, LANES))],
        out_specs=blk,
        out_shape=jax.ShapeDtypeStruct((N_PAIRS, b, s, LANES), BF16),
        scratch_shapes=[pltpu.VMEM((s, LANES), F32), pltpu.VMEM((LANES, s), F32),
                        pltpu.VMEM((nh, s, LANES), BF16), pltpu.VMEM((nh, LANES, s), BF16),
                        pltpu.VMEM((nh, s, LANES), F32), pltpu.VMEM((nh, 1, s), F32),
                        pltpu.VMEM((2, s, tq), F32), pltpu.VMEM((2, s, tq), BF16),
                        pltpu.VMEM((2, SUBLANES, tq), F32), pltpu.VMEM((2, 1, tq), F32),
                        pltpu.VMEM((s // tq, LANES, tq), F32)],
        compiler_params=pltpu.CompilerParams(dimension_semantics=("arbitrary", "arbitrary"),
                                             vmem_limit_bytes=VMEM_LIMIT),
        name="fox_attn",
    )(q3, k3, v3, f3, bf)


FF_CHUNK = 512
N_MERGE_STAGES = 6


def _merge_ffn_kernel(x_ref, pa_ref, yb_ref, gab_ref, bm_ref, wpb_ref, wout_ref, gpost_ref,
                      gpre2_ref, gpost2_ref, perm_ref, wup_ref, cw_ref, cb_ref, wdn_ref, o_ref,
                      tail, unperm, x1_s, hp_s, *, tm, tiles_per_seq):
    i = pl.program_id(0)
    seg = tm // SUBLANES
    pitch = _scan_pitch(tm)

    @pl.when(i == 0)
    def _():
        x1_s[...] = jnp.zeros_like(x1_s)
        hp_s[...] = jnp.zeros_like(hp_s)
        tail[...] = jnp.zeros_like(tail)

    @pl.when((i + tiles_per_seq - 1) % tiles_per_seq == 0)
    def _():
        tail[...] = jnp.zeros_like(tail)

    x1_prev = x1_s[...]
    hp = hp_s[...]

    def merge_stage(k, st):
        if k == 0:
            st["gates"] = jax.nn.sigmoid(gab_ref[...] + bm_ref[...])
        elif k == 1:
            yb = jnp.concatenate([yb_ref[p] for p in range(N_PAIRS)], axis=1)
            st["pb"] = jnp.dot(yb, wpb_ref[:, :D_MODEL], preferred_element_type=F32)
        elif k == 2:
            g = st.pop("gates")
            st["merged"] = (g[:, :D_MODEL] * pa_ref[...] + g[:, D_MODEL:] * st.pop("pb")).astype(BF16)
        elif k == 3:
            st["mix"] = jnp.dot(st.pop("merged"), wout_ref[:, :D_MODEL], preferred_element_type=F32)
        elif k == 4:
            x1_s[...] = x_ref[...] + _rms(st.pop("mix"), gpost_ref[...])
        elif k == 5:
            h = _rms(x1_s[...], gpre2_ref[...]).astype(BF16)
            hp_s[...] = jnp.dot(perm_ref[...], h, preferred_element_type=F32).astype(BF16)

    first = lax.broadcasted_iota(jnp.int32, (SUBLANES, FF_CHUNK), 0) == 0

    def up_proj(c):
        return [jnp.dot(hp, wup_ref[:, off + c * FF_CHUNK:off + (c + 1) * FF_CHUNK],
                        preferred_element_type=F32) for off in (0, D_FF)]

    def conv(u, cols):
        prev = [jnp.where(first, pltpu.roll(tail[g * SUBLANES:(g + 1) * SUBLANES, cols], 1, axis=0),
                          pltpu.roll(u[tm - (2 - g) * SUBLANES:tm - (1 - g) * SUBLANES, :], 1, axis=0))
                for g in range(CONV_F - 1)]
        tail[:, cols] = u[tm - (CONV_F - 1) * SUBLANES:, :]
        u1 = jnp.concatenate([prev[1], u[:tm - SUBLANES, :]], axis=0)
        u2 = jnp.concatenate([prev[0], prev[1], u[:tm - 2 * SUBLANES, :]], axis=0)
        return (cb_ref[:, cols] + cw_ref[2:3, cols] * u + cw_ref[1:2, cols] * u1
                + cw_ref[0:1, cols] * u2)

    n_chunks = D_FF // FF_CHUNK
    stages = {}
    acc = jnp.zeros((tm, D_MODEL), F32)
    u_next = up_proj(0)
    for c in range(n_chunks):
        u_gate, u_val = u_next
        if c + 1 < n_chunks:
            u_next = up_proj(c + 1)
        for k in range(c * N_MERGE_STAGES // n_chunks, (c + 1) * N_MERGE_STAGES // n_chunks):
            merge_stage(k, stages)
        gate = conv(u_gate, slice(c * FF_CHUNK, (c + 1) * FF_CHUNK))
        val = conv(u_val, slice(D_FF + c * FF_CHUNK, D_FF + (c + 1) * FF_CHUNK))
        act = (jax.nn.gelu(gate) * val).astype(BF16)
        acc = acc + jnp.dot(act, wdn_ref[c * FF_CHUNK:(c + 1) * FF_CHUNK, :D_MODEL],
                            preferred_element_type=F32)

    for j in range(seg):
        for c in range(D_MODEL // LANES):
            unperm[c, pl.ds(j, SUBLANES, stride=pitch), :] = (
                acc[j * SUBLANES:(j + 1) * SUBLANES, c * LANES:(c + 1) * LANES])
    ffn = jnp.concatenate(
        [jnp.concatenate([unperm[c, s * pitch:s * pitch + seg, :] for s in range(SUBLANES)], axis=0)
         for c in range(D_MODEL // LANES)], axis=1)
    o_ref[...] = x1_prev + _rms(ffn, gpost2_ref[...])


def _merge_ffn(x2, pa2, yb2, gab2, bm, wpb, wout, gpost, gpre2, gpost2, wup, cw, cb, wdn,
               tm, seq):
    t = x2.shape[0]
    n_tiles = t // tm
    row = lambda n: pl.BlockSpec((tm, n), lambda i: (jnp.minimum(i, n_tiles - 1), 0))
    r = jnp.arange(tm)
    perm = (r[None, :] == ((r % SUBLANES) * (tm // SUBLANES) + r // SUBLANES)[:, None]).astype(BF16)
    return pl.pallas_call(
        functools.partial(_merge_ffn_kernel, tm=tm, tiles_per_seq=seq // tm),
        grid=(n_tiles + 1,),
        in_specs=[row(D_MODEL), row(D_MODEL),
                  pl.BlockSpec((N_PAIRS, tm, LANES), lambda i: (0, jnp.minimum(i, n_tiles - 1), 0)),
                  row(2 * D_MODEL),
                  _const_spec((1, 2 * D_MODEL)), _const_spec((D_ATT, _padded(D_MODEL))),
                  _const_spec((D_MODEL, _padded(D_MODEL))), _const_spec((1, D_MODEL)),
                  _const_spec((1, D_MODEL)), _const_spec((1, D_MODEL)), _const_spec((tm, tm)),
                  _const_spec((D_MODEL, _padded(2 * D_FF))), _const_spec((CONV_F, 2 * D_FF)),
                  _const_spec((1, 2 * D_FF)), _const_spec((D_FF, _padded(D_MODEL)))],
        out_specs=pl.BlockSpec((tm, D_MODEL), lambda i: (jnp.maximum(i - 1, 0), 0)),
        out_shape=jax.ShapeDtypeStruct((t, D_MODEL), F32),
        scratch_shapes=[pltpu.VMEM(((CONV_F - 1) * SUBLANES, 2 * D_FF), F32),
                        pltpu.VMEM((D_MODEL // LANES, SUBLANES * _scan_pitch(tm), LANES), F32),
                        pltpu.VMEM((tm, D_MODEL), F32), pltpu.VMEM((tm, D_MODEL), BF16)],
        compiler_params=pltpu.CompilerParams(dimension_semantics=("arbitrary",),
                                             vmem_limit_bytes=VMEM_LIMIT),
        name="merge_ffn",
    )(x2, pa2, yb2, gab2, bm, wpb, wout, gpost, gpre2, gpost2, perm, wup, cw, cb, wdn)


def _block_diag(w):
    h, n, _ = w.shape
    col = jnp.arange(h * n)
    tile = (col[None, :] % n == jnp.arange(n)[:, None]).astype(w.dtype)
    rep = jnp.dot(w.reshape(h * n, n), tile, precision=lax.Precision.HIGHEST)
    return jnp.where(col[:, None] // n == col[None, :] // n, rep, 0.0)


def _gate_windows(w_a, w_x):
    da, dx = _block_diag(w_a), _block_diag(w_x)
    tiles = []
    for j, ks in enumerate(GATE_STARTS):
        cols = slice(j * GATE_TILE, (j + 1) * GATE_TILE)
        tiles.append(jnp.concatenate([da[ks:ks + GATE_WIN, cols], dx[ks:ks + GATE_WIN, cols]],
                                     axis=1))
    return jnp.stack([_bf16_weight(t) for t in tiles])


def _layer(x, mix_norm_pre, mix_norm_post, w_in, conv_a_w, conv_a_b, w_rg_a, b_rg_a, w_rg_x,
           b_rg_x, lru_lambda, b_forget, b_merge, w_proj_a, w_proj_b, w_out, ffn_norm_pre,
           ffn_norm_post, w_up, conv_f_w, conv_f_b, w_down):
    bsz, seq, _ = x.shape
    t = bsz * seq
    row = lambda a: a.reshape(1, -1)

    w_main = _bf16_weight(w_in[:, :_OFF_F])
    w_gate = _bf16_weight(w_in[:, _OFF_G:])
    w_f = jnp.pad(w_in[:, _OFF_F:_OFF_G].astype(BF16), ((0, 0), (0, LANES - H_ATT)))
    bf = jnp.concatenate([b_forget, jnp.zeros((LANES - H_ATT,), b_forget.dtype)]).reshape(1, LANES)
    wg = _gate_windows(w_rg_a, w_rg_x)

    x2 = x.reshape(t, D_MODEL)
    q, k, v, gab, f, pa = _mix_in(x2, row(mix_norm_pre), w_main, w_gate, w_f, conv_a_w,
                                  row(conv_a_b), wg, row(b_rg_a), row(b_rg_x), row(lru_lambda),
                                  _bf16_weight(w_proj_a), tc=MIX_IN_ROWS, seq=seq)

    p4 = lambda a: a.reshape(N_PAIRS, bsz, seq, LANES)
    yb = _fox_attn(p4(q), p4(k), p4(v), f.reshape(bsz, seq, LANES), bf, tq=ATTN_Q_ROWS)

    out = _merge_ffn(x2, pa, yb.reshape(N_PAIRS, t, LANES), gab, row(b_merge),
                     _bf16_weight(w_proj_b), _bf16_weight(w_out), row(mix_norm_post),
                     row(ffn_norm_pre), row(ffn_norm_post), _bf16_weight(w_up), conv_f_w,
                     row(conv_f_b), _bf16_weight(w_down), tm=FFN_ROWS, seq=seq)
    return out.reshape(bsz, seq, D_MODEL)


def kernel(x, mix_norm_pre, mix_norm_post, w_in, conv_a_w, conv_a_b, w_rg_a, b_rg_a, w_rg_x, b_rg_x, lru_lambda, b_forget, b_merge, w_proj_a, w_proj_b, w_out, ffn_norm_pre, ffn_norm_post, w_up, conv_f_w, conv_f_b, w_down):
    depth = w_in.shape[0]
    for layer in range(depth):
        x = _layer(x, mix_norm_pre[layer], mix_norm_post[layer], w_in[layer], conv_a_w[layer],
                   conv_a_b[layer], w_rg_a[layer], b_rg_a[layer], w_rg_x[layer], b_rg_x[layer],
                   lru_lambda[layer], b_forget[layer], b_merge[layer], w_proj_a[layer],
                   w_proj_b[layer], w_out[layer], ffn_norm_pre[layer], ffn_norm_post[layer],
                   w_up[layer], conv_f_w[layer], conv_f_b[layer], w_down[layer])
    return x
```

```python
import functools
import math

import jax
import jax.numpy as jnp
from jax import lax
from jax.experimental import pallas as pl
from jax.experimental.pallas import tpu as pltpu

D_MODEL = 1024
D_RNN = 1280
H_RNN = 16
RNN_BLOCK = D_RNN // H_RNN
CONV_A = 4
LRU_C = 8.0
H_ATT = 16
HEAD_DIM = 64
D_ATT = H_ATT * HEAD_DIM
N_PAIRS = H_ATT // 2
D_FF = 3 * D_MODEL
CONV_F = 3
RMS_EPS = 1e-6

LANES = 128
SUBLANES = 8
MXU_COLS = 256
VMEM_LIMIT = 56 * 1024 * 1024

MIX_IN_ROWS = 256
ATTN_Q_ROWS = 256
ATTN_PAIRS = 4
FFN_ROWS = 256

LOG2E = math.log2(math.e)

F32 = jnp.float32
BF16 = jnp.bfloat16

_OFF_XA = 0
_OFF_GA = _OFF_XA + D_RNN
_OFF_Q = _OFF_GA + D_RNN
_OFF_K = _OFF_Q + D_ATT
_OFF_V = _OFF_K + D_ATT
_OFF_F = _OFF_V + D_ATT
_OFF_G = _OFF_F + H_ATT

GATE_TILE = MXU_COLS
GATE_WIN = 2 * MXU_COLS
N_GATE_TILES = D_RNN // GATE_TILE


def _gate_window_start(j):
    lo = (j * GATE_TILE // RNN_BLOCK) * RNN_BLOCK
    start = min((lo // LANES) * LANES, D_RNN - GATE_WIN)
    hi = -(-((j + 1) * GATE_TILE) // RNN_BLOCK) * RNN_BLOCK
    assert start <= lo and hi <= start + GATE_WIN
    return start


GATE_STARTS = tuple(_gate_window_start(j) for j in range(N_GATE_TILES))


def _rms(x, gain):
    y = x * lax.rsqrt(jnp.mean(x * x, axis=-1, keepdims=True) + RMS_EPS)
    return y * gain


def _padded(n):
    stride = n * SUBLANES // (2 * LANES)
    return n + LANES if stride % SUBLANES == 0 else n


def _bf16_weight(w):
    n = w.shape[1]
    return jnp.pad(w, ((0, 0), (0, _padded(n) - n))).astype(BF16)


def _const_spec(shape):
    nd = len(shape)
    return pl.BlockSpec(shape, lambda *_: (0,) * nd, pipeline_mode=pl.Buffered(1))


def _scan_pitch(tc):
    seg = tc // SUBLANES
    return seg + SUBLANES if (seg // SUBLANES) % 2 == 0 else seg


def _mix_in_kernel(x_ref, g_ref, w_ref, wgate_ref, wf_ref, cw_ref, cb_ref, wg_ref, ba_ref, bx_ref,
                   lam_ref, wp_ref, q_ref, k_ref, v_ref, gab_ref, f_ref, pa_ref,
                   xa_s, ga_in_s, xpad, ga_s, xc_s, xcb_s, a_s, b_s, h_s, p_s, y_s, hprev,
                   *, tc, tiles_per_seq):
    seg = tc // SUBLANES
    pitch = _scan_pitch(tc)
    n_lane = D_RNN // LANES
    step = pl.program_id(0)

    @pl.when(step == 0)
    def _():
        xa_s[...] = jnp.zeros_like(xa_s)
        ga_in_s[...] = jnp.zeros_like(ga_in_s)
        xpad[0:SUBLANES, :] = jnp.zeros((SUBLANES, D_RNN), F32)
        hprev[...] = jnp.zeros_like(hprev)

    @pl.when((step + tiles_per_seq - 1) % tiles_per_seq == 0)
    def _():
        xpad[0:SUBLANES, :] = jnp.zeros((SUBLANES, D_RNN), F32)
        hprev[...] = jnp.zeros_like(hprev)

    xpad[SUBLANES:SUBLANES + tc, :] = xa_s[...]
    ga_s[...] = ga_in_s[...]

    h = _rms(x_ref[...], g_ref[...]).astype(BF16)

    def mm(lo, hi):
        return jnp.dot(h, w_ref[:, lo:hi], preferred_element_type=F32)

    def proj_chunk(kind, c):
        lo, hi = c * MXU_COLS, (c + 1) * MXU_COLS

        def put_pairs(ref, val):
            for p in range(MXU_COLS // LANES):
                ref[c * (MXU_COLS // LANES) + p] = val[:, p * LANES:(p + 1) * LANES]

        if kind == "q":
            put_pairs(q_ref, (mm(_OFF_Q + lo, _OFF_Q + hi)
                              * (LOG2E / math.sqrt(HEAD_DIM))).astype(BF16))
        elif kind == "k":
            put_pairs(k_ref, mm(_OFF_K + lo, _OFF_K + hi).astype(BF16))
        elif kind == "v":
            put_pairs(v_ref, mm(_OFF_V + lo, _OFF_V + hi).astype(BF16))
        elif kind == "g":
            gab_ref[:, lo:hi] = jnp.dot(h, wgate_ref[:, lo:hi], preferred_element_type=F32)
        elif kind == "xa":
            xa_s[:, lo:hi] = mm(_OFF_XA + lo, _OFF_XA + hi)
        elif kind == "ga":
            ga_in_s[:, lo:hi] = mm(_OFF_GA + lo, _OFF_GA + hi)
        else:
            f_ref[...] = jnp.dot(h, wf_ref[...], preferred_element_type=F32)

    pending = ([("q", c) for c in range(D_ATT // MXU_COLS)]
               + [("k", c) for c in range(D_ATT // MXU_COLS)]
               + [("v", c) for c in range(D_ATT // MXU_COLS)]
               + [("g", c) for c in range(2 * D_MODEL // MXU_COLS)] + [("f", 0)]
               + [("xa", c) for c in range(D_RNN // MXU_COLS)]
               + [("ga", c) for c in range(D_RNN // MXU_COLS)])

    def emit(n):
        for _ in range(min(n, len(pending))):
            proj_chunk(*pending.pop(0))

    for j in range(N_GATE_TILES):
        cols = slice(j * GATE_TILE, (j + 1) * GATE_TILE)
        xc = cb_ref[:, cols] + cw_ref[CONV_A - 1:CONV_A, cols] * xpad[SUBLANES:SUBLANES + tc, cols]
        for kk in range(CONV_A - 1):
            sh = CONV_A - 1 - kk
            xc = xc + cw_ref[kk:kk + 1, cols] * xpad[SUBLANES - sh:SUBLANES - sh + tc, cols]
        xc_s[:, cols] = xc
        xcb_s[:, cols] = xc.astype(BF16)
        emit(1)
    xpad[0:SUBLANES, :] = xpad[tc:tc + SUBLANES, :]

    sp = jax.nn.softplus(-lam_ref[...])
    lane_chunks = GATE_TILE // LANES
    for j in range(N_GATE_TILES):
        ks = GATE_STARTS[j]
        cols = slice(j * GATE_TILE, (j + 1) * GATE_TILE)
        g = jnp.dot(xcb_s[:, ks:ks + GATE_WIN], wg_ref[j, :, :2 * GATE_TILE],
                    preferred_element_type=F32)
        r = jax.nn.sigmoid(g[:, :GATE_TILE] + ba_ref[:, cols])
        i = jax.nn.sigmoid(g[:, GATE_TILE:] + bx_ref[:, cols])
        log_a = -LRU_C * r * sp[:, cols]
        a = jnp.exp(log_a)
        b = jnp.sqrt(-jnp.tanh(log_a) * (a * a + 1.0)) * (i * xc_s[:, cols])
        for c in range(lane_chunks):
            slab = j * lane_chunks + c
            lanes = slice(c * LANES, (c + 1) * LANES)
            for s in range(SUBLANES):
                dst = slice(s * pitch, s * pitch + seg)
                a_s[slab, dst, :] = a[s * seg:(s + 1) * seg, lanes]
                b_s[slab, dst, :] = b[s * seg:(s + 1) * seg, lanes]
        emit(3)

    h8 = [jnp.zeros((SUBLANES, LANES), F32) for _ in range(n_lane)]
    p8 = [jnp.ones((SUBLANES, LANES), F32) for _ in range(n_lane)]
    for j in range(seg):
        rows = pl.ds(j, SUBLANES, stride=pitch)
        for c in range(n_lane):
            a_j = a_s[c, rows, :]
            h8[c] = a_j * h8[c] + b_s[c, rows, :]
            p8[c] = p8[c] * a_j
            h_s[c, rows, :] = h8[c]
            p_s[c, rows, :] = p8[c]
        if j % 4 == 3:
            emit(1)

    pa = jnp.zeros((tc, D_MODEL), F32)
    for c in range(n_lane):
        cols = slice(c * LANES, (c + 1) * LANES)
        start = hprev[:, cols]
        for s in range(SUBLANES):
            rows = slice(s * seg, (s + 1) * seg)
            src = slice(s * pitch, s * pitch + seg)
            hfin = h_s[c, src, :] + p_s[c, src, :] * start
            y_s[rows, cols] = (jax.nn.gelu(ga_s[rows, cols]) * hfin).astype(BF16)
            start = h8[c][s:s + 1, :] + p8[c][s:s + 1, :] * start
        hprev[:, cols] = start
        emit(1)
        if c % lane_chunks == lane_chunks - 1:
            slab = slice((c - lane_chunks + 1) * LANES, (c + 1) * LANES)
            pa = pa + jnp.dot(y_s[:, slab], wp_ref[slab, :D_MODEL], preferred_element_type=F32)
    emit(len(pending))
    pa_ref[...] = pa


def _mix_in(x2, gain, w_main, w_gate, w_f, cw, cb, wg, ba, bx, lam, wp, tc, seq):
    t = x2.shape[0]
    n_tiles = t // tc
    row = lambda n: pl.BlockSpec((tc, n), lambda i: (jnp.minimum(i, n_tiles - 1), 0))
    pairs = pl.BlockSpec((N_PAIRS, tc, LANES), lambda i: (0, jnp.minimum(i, n_tiles - 1), 0))
    scan_shape = (D_RNN // LANES, SUBLANES * _scan_pitch(tc), LANES)
    consts = (gain, w_main, w_gate, w_f, cw, cb, wg, ba, bx, lam, wp)
    return pl.pallas_call(
        functools.partial(_mix_in_kernel, tc=tc, tiles_per_seq=seq // tc),
        grid=(n_tiles + 1,),
        in_specs=[row(D_MODEL)] + [_const_spec(c.shape) for c in consts],
        out_specs=[pairs, pairs, pairs, row(2 * D_MODEL), row(LANES),
                   pl.BlockSpec((tc, D_MODEL), lambda i: (jnp.maximum(i - 1, 0), 0))],
        out_shape=[jax.ShapeDtypeStruct((N_PAIRS, t, LANES), BF16),
                   jax.ShapeDtypeStruct((N_PAIRS, t, LANES), BF16),
                   jax.ShapeDtypeStruct((N_PAIRS, t, LANES), BF16),
                   jax.ShapeDtypeStruct((t, 2 * D_MODEL), F32),
                   jax.ShapeDtypeStruct((t, LANES), F32),
                   jax.ShapeDtypeStruct((t, D_MODEL), F32)],
        scratch_shapes=[pltpu.VMEM((tc, D_RNN), F32), pltpu.VMEM((tc, D_RNN), F32),
                        pltpu.VMEM((tc + SUBLANES, D_RNN), F32), pltpu.VMEM((tc, D_RNN), F32),
                        pltpu.VMEM((tc, D_RNN), F32), pltpu.VMEM((tc, D_RNN), BF16),
                        pltpu.VMEM(scan_shape, F32), pltpu.VMEM(scan_shape, F32),
                        pltpu.VMEM(scan_shape, F32), pltpu.VMEM(scan_shape, F32),
                        pltpu.VMEM((tc, D_RNN), BF16), pltpu.VMEM((1, D_RNN), F32)],
        compiler_params=pltpu.CompilerParams(dimension_semantics=("arbitrary",),
                                             vmem_limit_bytes=VMEM_LIMIT),
        name="mix_in",
    )(x2, *consts)


def _fox_attn_kernel(q_ref, k_ref, v_ref, f_ref, bf_ref, o_ref, cum, cum_t, km_s, vt_s, ck_s,
                     cq_s, sc_s, p_s, m_s, off_s, o0_s, *, s, tq):
    pair = pl.program_id(1)

    @pl.when(pair == 0)
    def _():
        c = jax.nn.log_sigmoid(f_ref[0] + bf_ref[...])
        row = lax.broadcasted_iota(jnp.int32, (s, LANES), 0)
        d = 1
        while d < s:
            c = c + jnp.where(row >= d, pltpu.roll(c, d, axis=0), 0.0)
            d *= 2
        c = c * LOG2E
        cum[...] = c
        cum_t[...] = c.T

    lane = lax.broadcasted_iota(jnp.int32, (1, LANES), 1)
    n_q = s // tq

    n_pairs = q_ref.shape[0]
    v_row = lax.broadcasted_iota(jnp.int32, (LANES, 1), 0)
    for pp in range(n_pairs):
        k = k_ref[pp, 0]
        v_t = v_ref[pp, 0].astype(F32).T
        for hh in range(2):
            hs = 2 * pp + hh
            head = 2 * (n_pairs * pair + pp) + hh
            in_head = (lane >= hh * HEAD_DIM) & (lane < (hh + 1) * HEAD_DIM)
            km_s[hs] = jnp.where(in_head, k, jnp.zeros_like(k))
            own_rows = (v_row >= hh * HEAD_DIM) & (v_row < (hh + 1) * HEAD_DIM)
            vt_s[hs] = jnp.where(own_rows, v_t, 1.0).astype(BF16)
            ck = jnp.sum(jnp.where(lane == head, cum[...], 0.0), axis=-1, keepdims=True)
            ck_s[hs] = jnp.broadcast_to(ck, (s, LANES))
            cq_s[hs] = cum_t[pl.ds(head, 1), :]

    tasks = [(hs, i) for hs in range(2 * n_pairs) for i in range(n_q)]

    def pieces(i, j):
        if j == i:
            return [(0, tq // 2, 0, tq), (tq // 2, tq, tq // 2, tq)]
        return [(0, tq, 0, tq)]

    def score_tile(n, j):
        hs, i = tasks[n]
        slot = n % 2
        ks = slice(j * tq, (j + 1) * tq)
        sc = lax.dot_general(km_s[hs, ks, :], q_ref[hs // 2, 0, i * tq:(i + 1) * tq, :],
                             (((1,), (1,)), ((), ())), preferred_element_type=F32)
        for n_piece, (r0, r1, l0, l1) in enumerate(pieces(i, j)):
            rows = slice(j * tq + r0, j * tq + r1)
            blk = jnp.concatenate([sc[r0:r1, c * LANES:(c + 1) * LANES] - ck_s[hs, rows, :]
                                   for c in range(l0 // LANES, l1 // LANES)], axis=1)
            if j == i:
                key = lax.broadcasted_iota(jnp.int32, blk.shape, 0) + r0
                qry = lax.broadcasted_iota(jnp.int32, blk.shape, 1) + l0
                blk = jnp.where(key <= qry, blk, -jnp.inf)
            sc_s[slot, rows, l0:l1] = blk
            part = blk[0:SUBLANES, :]
            for r in range(1, (r1 - r0) // SUBLANES):
                part = jnp.maximum(part, blk[r * SUBLANES:(r + 1) * SUBLANES, :])
            if j == 0 and n_piece == 0:
                m_s[slot] = part
            else:
                m_s[slot, :, l0:l1] = jnp.maximum(m_s[slot, :, l0:l1], part)

    def finish_scores(n):
        hs, i = tasks[n]
        slot = n % 2
        m = jnp.max(m_s[slot], axis=0, keepdims=True)
        cq_i = cq_s[hs, :, i * tq:(i + 1) * tq]
        off_s[slot] = (m + cq_i) - cq_i

    def exp_tile(n, j):
        slot = n % 2
        i = tasks[n][1]
        for r0, r1, l0, l1 in pieces(i, j):
            rows = slice(j * tq + r0, j * tq + r1)
            p_s[slot, rows, l0:l1] = jnp.exp2(sc_s[slot, rows, l0:l1]
                                              - off_s[slot, :, l0:l1]).astype(BF16)
        if j == i:
            half = tq // 2
            p_s[slot, j * tq + half:(j + 1) * tq, 0:half] = jnp.zeros((half, half), BF16)

    def weighted_values(n):
        hs, i = tasks[n]
        slot = n % 2
        qs, qe = i * tq, (i + 1) * tq
        o = jnp.dot(vt_s[hs, :, 0:qe], p_s[slot, 0:qe, :], preferred_element_type=F32)
        sum_row = (1 - hs % 2) * HEAD_DIM
        o = o / o[sum_row:sum_row + 1, :]
        if hs % 2 == 0:
            o0_s[i] = o
        else:
            row = lax.broadcasted_iota(jnp.int32, (LANES, 1), 0)
            o_ref[hs // 2, 0, qs:qe, :] = jnp.where(row < HEAD_DIM, o0_s[i], o).T.astype(BF16)

    for j in range(tasks[0][1] + 1):
        score_tile(0, j)
    finish_scores(0)
    for n in range(len(tasks)):
        n_exp = tasks[n][1] + 1
        n_score = tasks[n + 1][1] + 1 if n + 1 < len(tasks) else 0
        for j in range(max(n_exp, n_score)):
            if j < n_score:
                score_tile(n + 1, j)
            if j < n_exp:
                exp_tile(n, j)
        if n_score:
            finish_scores(n + 1)
        weighted_values(n)


def _fox_attn(q3, k3, v3, f3, bf, tq):
    _, b, s, _ = q3.shape
    blk = pl.BlockSpec((ATTN_PAIRS, 1, s, LANES), lambda bi, pi: (pi, bi, 0, 0))
    nh = 2 * ATTN_PAIRS
    return pl.pallas_call(
        functools.partial(_fox_attn_kernel, s=s, tq=tq),
        grid=(b, N_PAIRS // ATTN_PAIRS),
        in_specs=[blk, blk, blk, pl.BlockSpec((1, s, LANES), lambda bi, pi: (bi, 0, 0)),
                  _const_spec((1, LANES))],
        out_specs=blk,
        out_shape=jax.ShapeDtypeStruct((N_PAIRS, b, s, LANES), BF16),
        scratch_shapes=[pltpu.VMEM((s, LANES), F32), pltpu.VMEM((LANES, s), F32),
                        pltpu.VMEM((nh, s, LANES), BF16), pltpu.VMEM((nh, LANES, s), BF16),
                        pltpu.VMEM((nh, s, LANES), F32), pltpu.VMEM((nh, 1, s), F32),
                        pltpu.VMEM((2, s, tq), F32), pltpu.VMEM((2, s, tq), BF16),
                        pltpu.VMEM((2, SUBLANES, tq), F32), pltpu.VMEM((2, 1, tq), F32),
                        pltpu.VMEM((s // tq, LANES, tq), F32)],
        compiler_params=pltpu.CompilerParams(dimension_semantics=("arbitrary", "arbitrary"),
                                             vmem_limit_bytes=VMEM_LIMIT),
        name="fox_attn",
    )(q3, k3, v3, f3, bf)


FF_CHUNK = 512
N_MERGE_STAGES = 6


def _merge_ffn_kernel(x_ref, pa_ref, yb_ref, gab_ref, bm_ref, wpb_ref, wout_ref, gpost_ref,
                      gpre2_ref, gpost2_ref, perm_ref, wup_ref, cw_ref, cb_ref, wdn_ref, o_ref,
                      tail, unperm, x1_s, hp_s, *, tm, tiles_per_seq):
    i = pl.program_id(0)
    seg = tm // SUBLANES
    pitch = _scan_pitch(tm)

    @pl.when(i == 0)
    def _():
        x1_s[...] = jnp.zeros_like(x1_s)
        hp_s[...] = jnp.zeros_like(hp_s)
        tail[...] = jnp.zeros_like(tail)

    @pl.when((i + tiles_per_seq - 1) % tiles_per_seq == 0)
    def _():
        tail[...] = jnp.zeros_like(tail)

    x1_prev = x1_s[...]
    hp = hp_s[...]

    def merge_stage(k, st):
        if k == 0:
            st["gates"] = jax.nn.sigmoid(gab_ref[...] + bm_ref[...])
        elif k == 1:
            yb = jnp.concatenate([yb_ref[p] for p in range(N_PAIRS)], axis=1)
            st["pb"] = jnp.dot(yb, wpb_ref[:, :D_MODEL], preferred_element_type=F32)
        elif k == 2:
            g = st.pop("gates")
            merged = g[:, :D_MODEL] * pa_ref[...] + g[:, D_MODEL:] * st.pop("pb")
            st["merged"] = merged.astype(BF16)
        elif k == 3:
            st["mix"] = jnp.dot(st.pop("merged"), wout_ref[:, :D_MODEL], preferred_element_type=F32)
        elif k == 4:
            x1_s[...] = x_ref[...] + _rms(st.pop("mix"), gpost_ref[...])
        elif k == 5:
            h = _rms(x1_s[...], gpre2_ref[...]).astype(BF16)
            hp_s[...] = jnp.dot(perm_ref[...], h, preferred_element_type=F32).astype(BF16)

    first = lax.broadcasted_iota(jnp.int32, (SUBLANES, FF_CHUNK), 0) == 0

    def up_proj(c):
        return [jnp.dot(hp, wup_ref[:, off + c * FF_CHUNK:off + (c + 1) * FF_CHUNK],
                        preferred_element_type=F32) for off in (0, D_FF)]

    def conv(u, cols):
        prev = []
        for g in range(CONV_F - 1):
            old = tail[g * SUBLANES:(g + 1) * SUBLANES, cols]
            cur = u[tm - (2 - g) * SUBLANES:tm - (1 - g) * SUBLANES, :]
            prev.append(jnp.where(first, pltpu.roll(old, 1, axis=0), pltpu.roll(cur, 1, axis=0)))
        tail[:, cols] = u[tm - (CONV_F - 1) * SUBLANES:, :]
        u1 = jnp.concatenate([prev[1], u[:tm - SUBLANES, :]], axis=0)
        u2 = jnp.concatenate([prev[0], prev[1], u[:tm - 2 * SUBLANES, :]], axis=0)
        return (cb_ref[:, cols] + cw_ref[2:3, cols] * u + cw_ref[1:2, cols] * u1
                + cw_ref[0:1, cols] * u2)

    n_chunks = D_FF // FF_CHUNK
    stages = {}
    acc = jnp.zeros((tm, D_MODEL), F32)
    u_next = up_proj(0)
    for c in range(n_chunks):
        u_gate, u_val = u_next
        if c + 1 < n_chunks:
            u_next = up_proj(c + 1)
        for k in range(c * N_MERGE_STAGES // n_chunks, (c + 1) * N_MERGE_STAGES // n_chunks):
            merge_stage(k, stages)
        gate = conv(u_gate, slice(c * FF_CHUNK, (c + 1) * FF_CHUNK))
        val = conv(u_val, slice(D_FF + c * FF_CHUNK, D_FF + (c + 1) * FF_CHUNK))
        act = (jax.nn.gelu(gate) * val).astype(BF16)
        acc = acc + jnp.dot(act, wdn_ref[c * FF_CHUNK:(c + 1) * FF_CHUNK, :D_MODEL],
                            preferred_element_type=F32)

    for j in range(seg):
        for c in range(D_MODEL // LANES):
            unperm[c, pl.ds(j, SUBLANES, stride=pitch), :] = (
                acc[j * SUBLANES:(j + 1) * SUBLANES, c * LANES:(c + 1) * LANES])
    ffn = jnp.concatenate(
        [jnp.concatenate([unperm[c, s * pitch:s * pitch + seg, :] for s in range(SUBLANES)], axis=0)
         for c in range(D_MODEL // LANES)], axis=1)
    o_ref[...] = x1_prev + _rms(ffn, gpost2_ref[...])


def _merge_ffn(x2, pa2, yb2, gab2, bm, wpb, wout, gpost, gpre2, gpost2, wup, cw, cb, wdn,
               tm, seq):
    t = x2.shape[0]
    n_tiles = t // tm
    row = lambda n: pl.BlockSpec((tm, n), lambda i: (jnp.minimum(i, n_tiles - 1), 0))
    r = jnp.arange(tm)
    perm = (r[None, :] == ((r % SUBLANES) * (tm // SUBLANES) + r // SUBLANES)[:, None]).astype(BF16)
    return pl.pallas_call(
        functools.partial(_merge_ffn_kernel, tm=tm, tiles_per_seq=seq // tm),
        grid=(n_tiles + 1,),
        in_specs=[row(D_MODEL), row(D_MODEL),
                  pl.BlockSpec((N_PAIRS, tm, LANES), lambda i: (0, jnp.minimum(i, n_tiles - 1), 0)),
                  row(2 * D_MODEL),
                  _const_spec((1, 2 * D_MODEL)), _const_spec((D_ATT, _padded(D_MODEL))),
                  _const_spec((D_MODEL, _padded(D_MODEL))), _const_spec((1, D_MODEL)),
                  _const_spec((1, D_MODEL)), _const_spec((1, D_MODEL)), _const_spec((tm, tm)),
                  _const_spec((D_MODEL, _padded(2 * D_FF))), _const_spec((CONV_F, 2 * D_FF)),
                  _const_spec((1, 2 * D_FF)), _const_spec((D_FF, _padded(D_MODEL)))],
        out_specs=pl.BlockSpec((tm, D_MODEL), lambda i: (jnp.maximum(i - 1, 0), 0)),
        out_shape=jax.ShapeDtypeStruct((t, D_MODEL), F32),
        scratch_shapes=[pltpu.VMEM(((CONV_F - 1) * SUBLANES, 2 * D_FF), F32),
                        pltpu.VMEM((D_MODEL // LANES, SUBLANES * _scan_pitch(tm), LANES), F32),
                        pltpu.VMEM((tm, D_MODEL), F32), pltpu.VMEM((tm, D_MODEL), BF16)],
        compiler_params=pltpu.CompilerParams(dimension_semantics=("arbitrary",),
                                             vmem_limit_bytes=VMEM_LIMIT),
        name="merge_ffn",
    )(x2, pa2, yb2, gab2, bm, wpb, wout, gpost, gpre2, gpost2, perm, wup, cw, cb, wdn)


def _block_diag(w):
    h, n, _ = w.shape
    col = jnp.arange(h * n)
    tile = (col[None, :] % n == jnp.arange(n)[:, None]).astype(w.dtype)
    rep = jnp.dot(w.reshape(h * n, n), tile, precision=lax.Precision.HIGHEST)
    return jnp.where(col[:, None] // n == col[None, :] // n, rep, 0.0)


def _gate_windows(w_a, w_x):
    da, dx = _block_diag(w_a), _block_diag(w_x)
    tiles = []
    for j, ks in enumerate(GATE_STARTS):
        cols = slice(j * GATE_TILE, (j + 1) * GATE_TILE)
        tiles.append(jnp.concatenate([da[ks:ks + GATE_WIN, cols], dx[ks:ks + GATE_WIN, cols]],
                                     axis=1))
    return jnp.stack([_bf16_weight(t) for t in tiles])


def _layer(x, mix_norm_pre, mix_norm_post, w_in, conv_a_w, conv_a_b, w_rg_a, b_rg_a, w_rg_x,
           b_rg_x, lru_lambda, b_forget, b_merge, w_proj_a, w_proj_b, w_out, ffn_norm_pre,
           ffn_norm_post, w_up, conv_f_w, conv_f_b, w_down):
    bsz, seq, _ = x.shape
    t = bsz * seq
    row = lambda a: a.reshape(1, -1)

    w_main = _bf16_weight(w_in[:, :_OFF_F])
    w_gate = _bf16_weight(w_in[:, _OFF_G:])
    w_f = jnp.pad(w_in[:, _OFF_F:_OFF_G].astype(BF16), ((0, 0), (0, LANES - H_ATT)))
    bf = jnp.concatenate([b_forget, jnp.zeros((LANES - H_ATT,), b_forget.dtype)]).reshape(1, LANES)
    wg = _gate_windows(w_rg_a, w_rg_x)

    x2 = x.reshape(t, D_MODEL)
    q, k, v, gab, f, pa = _mix_in(x2, row(mix_norm_pre), w_main, w_gate, w_f, conv_a_w,
                                  row(conv_a_b), wg, row(b_rg_a), row(b_rg_x), row(lru_lambda),
                                  _bf16_weight(w_proj_a), tc=MIX_IN_ROWS, seq=seq)

    p4 = lambda a: a.reshape(N_PAIRS, bsz, seq, LANES)
    yb = _fox_attn(p4(q), p4(k), p4(v), f.reshape(bsz, seq, LANES), bf, tq=ATTN_Q_ROWS)

    out = _merge_ffn(x2, pa, yb.reshape(N_PAIRS, t, LANES), gab, row(b_merge),
                     _bf16_weight(w_proj_b), _bf16_weight(w_out), row(mix_norm_post),
                     row(ffn_norm_pre), row(ffn_norm_post), _bf16_weight(w_up), conv_f_w,
                     row(conv_f_b), _bf16_weight(w_down), tm=FFN_ROWS, seq=seq)
    return out.reshape(bsz, seq, D_MODEL)


def kernel(x, mix_norm_pre, mix_norm_post, w_in, conv_a_w, conv_a_b, w_rg_a, b_rg_a, w_rg_x, b_rg_x, lru_lambda, b_forget, b_merge, w_proj_a, w_proj_b, w_out, ffn_norm_pre, ffn_norm_post, w_up, conv_f_w, conv_f_b, w_down):
    depth = w_in.shape[0]
    for layer in range(depth):
        x = _layer(x, mix_norm_pre[layer], mix_norm_post[layer], w_in[layer], conv_a_w[layer],
                   conv_a_b[layer], w_rg_a[layer], b_rg_a[layer], w_rg_x[layer], b_rg_x[layer],
                   lru_lambda[layer], b_forget[layer], b_merge[layer], w_proj_a[layer],
                   w_proj_b[layer], w_out[layer], ffn_norm_pre[layer], ffn_norm_post[layer],
                   w_up[layer], conv_f_w[layer], conv_f_b[layer], w_down[layer])
    return x
```

```python
import functools
import math

import jax
import jax.numpy as jnp
from jax import lax
from jax.experimental import pallas as pl
from jax.experimental.pallas import tpu as pltpu

D_MODEL = 1024
D_RNN = 1280
H_RNN = 16
RNN_BLOCK = D_RNN // H_RNN
CONV_A = 4
LRU_C = 8.0
H_ATT = 16
HEAD_DIM = 64
D_ATT = H_ATT * HEAD_DIM
N_PAIRS = H_ATT // 2
D_FF = 3 * D_MODEL
CONV_F = 3
RMS_EPS = 1e-6

LANES = 128
SUBLANES = 8
MXU_COLS = 256
VMEM_LIMIT = 56 * 1024 * 1024

MIX_IN_ROWS = 256
ATTN_Q_ROWS = 256
ATTN_PAIRS = 4
FFN_ROWS = 256

LOG2E = math.log2(math.e)

F32 = jnp.float32
BF16 = jnp.bfloat16

_OFF_XA = 0
_OFF_GA = _OFF_XA + D_RNN
_OFF_Q = _OFF_GA + D_RNN
_OFF_K = _OFF_Q + D_ATT
_OFF_V = _OFF_K + D_ATT
_OFF_F = _OFF_V + D_ATT
_OFF_G = _OFF_F + H_ATT

GATE_TILE = MXU_COLS
GATE_WIN = 2 * MXU_COLS
N_GATE_TILES = D_RNN // GATE_TILE


def _gate_window_start(j):
    lo = (j * GATE_TILE // RNN_BLOCK) * RNN_BLOCK
    start = min((lo // LANES) * LANES, D_RNN - GATE_WIN)
    hi = -(-((j + 1) * GATE_TILE) // RNN_BLOCK) * RNN_BLOCK
    assert start <= lo and hi <= start + GATE_WIN
    return start


GATE_STARTS = tuple(_gate_window_start(j) for j in range(N_GATE_TILES))


def _rms(x, gain):
    y = x * lax.rsqrt(jnp.mean(x * x, axis=-1, keepdims=True) + RMS_EPS)
    return y * gain


def _padded(n):
    stride = n * SUBLANES // (2 * LANES)
    return n + LANES if stride % SUBLANES == 0 else n


def _bf16_weight(w):
    k, n = w.shape
    if _padded(n) == n:
        return w.astype(BF16)
    return jnp.concatenate([w.astype(BF16), jnp.zeros((k, _padded(n) - n), BF16)], axis=1)


def _const_spec(shape):
    nd = len(shape)
    return pl.BlockSpec(shape, lambda *_: (0,) * nd, pipeline_mode=pl.Buffered(1))


def _scan_pitch(tc):
    seg = tc // SUBLANES
    return seg + SUBLANES if (seg // SUBLANES) % 2 == 0 else seg


def _mix_in_kernel(x_ref, g_ref, w_ref, wgate_ref, wf_ref, cw_ref, cb_ref, wg_ref, ba_ref, bx_ref,
                   lam_ref, wp_ref, q_ref, k_ref, v_ref, gab_ref, f_ref, pa_ref,
                   xa_s, ga_in_s, xpad, ga_s, xc_s, xcb_s, a_s, b_s, h_s, p_s, y_s, hprev,
                   *, tc, tiles_per_seq):
    seg = tc // SUBLANES
    pitch = _scan_pitch(tc)
    n_lane = D_RNN // LANES
    step = pl.program_id(0)

    @pl.when(step == 0)
    def _():
        xa_s[...] = jnp.zeros_like(xa_s)
        ga_in_s[...] = jnp.zeros_like(ga_in_s)
        xpad[0:SUBLANES, :] = jnp.zeros((SUBLANES, D_RNN), F32)
        hprev[...] = jnp.zeros_like(hprev)

    @pl.when((step + tiles_per_seq - 1) % tiles_per_seq == 0)
    def _():
        xpad[0:SUBLANES, :] = jnp.zeros((SUBLANES, D_RNN), F32)
        hprev[...] = jnp.zeros_like(hprev)

    xpad[SUBLANES:SUBLANES + tc, :] = xa_s[...]
    ga_s[...] = ga_in_s[...]

    h = _rms(x_ref[...], g_ref[...]).astype(BF16)

    def mm(lo, hi):
        return jnp.dot(h, w_ref[:, lo:hi], preferred_element_type=F32)

    def proj_chunk(kind, c):
        lo, hi = c * MXU_COLS, (c + 1) * MXU_COLS

        def put_pairs(ref, val):
            for p in range(MXU_COLS // LANES):
                ref[c * (MXU_COLS // LANES) + p] = val[:, p * LANES:(p + 1) * LANES]

        if kind == "q":
            put_pairs(q_ref, (mm(_OFF_Q + lo, _OFF_Q + hi)
                              * (LOG2E / math.sqrt(HEAD_DIM))).astype(BF16))
        elif kind == "k":
            put_pairs(k_ref, mm(_OFF_K + lo, _OFF_K + hi).astype(BF16))
        elif kind == "v":
            put_pairs(v_ref, mm(_OFF_V + lo, _OFF_V + hi).astype(BF16))
        elif kind == "g":
            gab_ref[:, lo:hi] = jnp.dot(h, wgate_ref[:, lo:hi], preferred_element_type=F32)
        elif kind == "xa":
            xa_s[:, lo:hi] = mm(_OFF_XA + lo, _OFF_XA + hi)
        elif kind == "ga":
            ga_in_s[:, lo:hi] = mm(_OFF_GA + lo, _OFF_GA + hi)
        else:
            f_ref[...] = jnp.dot(h, wf_ref[...], preferred_element_type=F32)

    pending = ([("q", c) for c in range(D_ATT // MXU_COLS)]
               + [("k", c) for c in range(D_ATT // MXU_COLS)]
               + [("v", c) for c in range(D_ATT // MXU_COLS)]
               + [("g", c) for c in range(2 * D_MODEL // MXU_COLS)] + [("f", 0)]
               + [("xa", c) for c in range(D_RNN // MXU_COLS)]
               + [("ga", c) for c in range(D_RNN // MXU_COLS)])

    def emit(n):
        for _ in range(min(n, len(pending))):
            proj_chunk(*pending.pop(0))

    for j in range(N_GATE_TILES):
        cols = slice(j * GATE_TILE, (j + 1) * GATE_TILE)
        xc = cb_ref[:, cols] + cw_ref[CONV_A - 1:CONV_A, cols] * xpad[SUBLANES:SUBLANES + tc, cols]
        for kk in range(CONV_A - 1):
            sh = CONV_A - 1 - kk
            xc = xc + cw_ref[kk:kk + 1, cols] * xpad[SUBLANES - sh:SUBLANES - sh + tc, cols]
        xc_s[:, cols] = xc
        xcb_s[:, cols] = xc.astype(BF16)
        emit(1)
    xpad[0:SUBLANES, :] = xpad[tc:tc + SUBLANES, :]

    sp = jax.nn.softplus(-lam_ref[...])
    lane_chunks = GATE_TILE // LANES
    for j in range(N_GATE_TILES):
        ks = GATE_STARTS[j]
        cols = slice(j * GATE_TILE, (j + 1) * GATE_TILE)
        g = jnp.dot(xcb_s[:, ks:ks + GATE_WIN], wg_ref[j, :, :2 * GATE_TILE],
                    preferred_element_type=F32)
        r = jax.nn.sigmoid(g[:, :GATE_TILE] + ba_ref[:, cols])
        i = jax.nn.sigmoid(g[:, GATE_TILE:] + bx_ref[:, cols])
        log_a = -LRU_C * r * sp[:, cols]
        a = jnp.exp(log_a)
        b = jnp.sqrt(-jnp.tanh(log_a) * (a * a + 1.0)) * (i * xc_s[:, cols])
        for c in range(lane_chunks):
            slab = j * lane_chunks + c
            lanes = slice(c * LANES, (c + 1) * LANES)
            for s in range(SUBLANES):
                dst = slice(s * pitch, s * pitch + seg)
                a_s[slab, dst, :] = a[s * seg:(s + 1) * seg, lanes]
                b_s[slab, dst, :] = b[s * seg:(s + 1) * seg, lanes]
        emit(3)

    h8 = [jnp.zeros((SUBLANES, LANES), F32) for _ in range(n_lane)]
    p8 = [jnp.ones((SUBLANES, LANES), F32) for _ in range(n_lane)]
    for j in range(seg):
        rows = pl.ds(j, SUBLANES, stride=pitch)
        for c in range(n_lane):
            a_j = a_s[c, rows, :]
            h8[c] = a_j * h8[c] + b_s[c, rows, :]
            p8[c] = p8[c] * a_j
            h_s[c, rows, :] = h8[c]
            p_s[c, rows, :] = p8[c]
        if j % 4 == 3:
            emit(1)

    pa = jnp.zeros((tc, D_MODEL), F32)
    for c in range(n_lane):
        cols = slice(c * LANES, (c + 1) * LANES)
        start = hprev[:, cols]
        for s in range(SUBLANES):
            rows = slice(s * seg, (s + 1) * seg)
            src = slice(s * pitch, s * pitch + seg)
            hfin = h_s[c, src, :] + p_s[c, src, :] * start
            y_s[rows, cols] = (jax.nn.gelu(ga_s[rows, cols]) * hfin).astype(BF16)
            start = h8[c][s:s + 1, :] + p8[c][s:s + 1, :] * start
        hprev[:, cols] = start
        emit(1)
        if c % lane_chunks == lane_chunks - 1:
            slab = slice((c - lane_chunks + 1) * LANES, (c + 1) * LANES)
            pa = pa + jnp.dot(y_s[:, slab], wp_ref[slab, :D_MODEL], preferred_element_type=F32)
    emit(len(pending))
    pa_ref[...] = pa


def _mix_in(x2, gain, w_main, w_gate, w_f, cw, cb, wg, ba, bx, lam, wp, tc, seq):
    t = x2.shape[0]
    n_tiles = t // tc
    row = lambda n: pl.BlockSpec((tc, n), lambda i: (jnp.minimum(i, n_tiles - 1), 0))
    pairs = pl.BlockSpec((N_PAIRS, tc, LANES), lambda i: (0, jnp.minimum(i, n_tiles - 1), 0))
    scan_shape = (D_RNN // LANES, SUBLANES * _scan_pitch(tc), LANES)
    consts = (gain, w_main, w_gate, w_f, cw, cb, wg, ba, bx, lam, wp)
    return pl.pallas_call(
        functools.partial(_mix_in_kernel, tc=tc, tiles_per_seq=seq // tc),
        grid=(n_tiles + 1,),
        in_specs=[row(D_MODEL)] + [_const_spec(c.shape) for c in consts],
        out_specs=[pairs, pairs, pairs, row(2 * D_MODEL), row(LANES),
                   pl.BlockSpec((tc, D_MODEL), lambda i: (jnp.maximum(i - 1, 0), 0))],
        out_shape=[jax.ShapeDtypeStruct((N_PAIRS, t, LANES), BF16),
                   jax.ShapeDtypeStruct((N_PAIRS, t, LANES), BF16),
                   jax.ShapeDtypeStruct((N_PAIRS, t, LANES), BF16),
                   jax.ShapeDtypeStruct((t, 2 * D_MODEL), F32),
                   jax.ShapeDtypeStruct((t, LANES), F32),
                   jax.ShapeDtypeStruct((t, D_MODEL), F32)],
        scratch_shapes=[pltpu.VMEM((tc, D_RNN), F32), pltpu.VMEM((tc, D_RNN), F32),
                        pltpu.VMEM((tc + SUBLANES, D_RNN), F32), pltpu.VMEM((tc, D_RNN), F32),
                        pltpu.VMEM((tc, D_RNN), F32), pltpu.VMEM((tc, D_RNN), BF16),
                        pltpu.VMEM(scan_shape, F32), pltpu.VMEM(scan_shape, F32),
                        pltpu.VMEM(scan_shape, F32), pltpu.VMEM(scan_shape, F32),
                        pltpu.VMEM((tc, D_RNN), BF16), pltpu.VMEM((1, D_RNN), F32)],
        compiler_params=pltpu.CompilerParams(dimension_semantics=("arbitrary",),
                                             vmem_limit_bytes=VMEM_LIMIT),
        name="mix_in",
    )(x2, *consts)


def _fox_attn_kernel(q_ref, k_ref, v_ref, f_ref, bf_ref, o_ref, cum, cum_t, km_s, vt_s, ck_s,
                     cq_s, sc_s, p_s, m_s, off_s, o0_s, *, s, tq):
    pair = pl.program_id(1)

    @pl.when(pair == 0)
    def _():
        c = jax.nn.log_sigmoid(f_ref[0] + bf_ref[...])
        row = lax.broadcasted_iota(jnp.int32, (s, LANES), 0)
        d = 1
        while d < s:
            c = c + jnp.where(row >= d, pltpu.roll(c, d, axis=0), 0.0)
            d *= 2
        c = c * LOG2E
        cum[...] = c
        cum_t[...] = c.T

    lane = lax.broadcasted_iota(jnp.int32, (1, LANES), 1)
    n_q = s // tq

    n_pairs = q_ref.shape[0]
    v_row = lax.broadcasted_iota(jnp.int32, (LANES, 1), 0)
    for pp in range(n_pairs):
        k = k_ref[pp, 0]
        v_t = v_ref[pp, 0].astype(F32).T
        for hh in range(2):
            hs = 2 * pp + hh
            head = 2 * (n_pairs * pair + pp) + hh
            in_head = (lane >= hh * HEAD_DIM) & (lane < (hh + 1) * HEAD_DIM)
            km_s[hs] = jnp.where(in_head, k, jnp.zeros_like(k))
            own_rows = (v_row >= hh * HEAD_DIM) & (v_row < (hh + 1) * HEAD_DIM)
            vt_s[hs] = jnp.where(own_rows, v_t, 1.0).astype(BF16)
            ck = jnp.sum(jnp.where(lane == head, cum[...], 0.0), axis=-1, keepdims=True)
            ck_s[hs] = jnp.broadcast_to(ck, (s, LANES))
            cq_s[hs] = cum_t[pl.ds(head, 1), :]

    tasks = [(hs, i) for hs in range(2 * n_pairs) for i in range(n_q)]

    def pieces(i, j):
        if j == i:
            return [(0, tq // 2, 0, tq), (tq // 2, tq, tq // 2, tq)]
        return [(0, tq, 0, tq)]

    def score_tile(n, j):
        hs, i = tasks[n]
        slot = n % 2
        ks = slice(j * tq, (j + 1) * tq)
        sc = lax.dot_general(km_s[hs, ks, :], q_ref[hs // 2, 0, i * tq:(i + 1) * tq, :],
                             (((1,), (1,)), ((), ())), preferred_element_type=F32)
        for n_piece, (r0, r1, l0, l1) in enumerate(pieces(i, j)):
            rows = slice(j * tq + r0, j * tq + r1)
            blk = jnp.concatenate([sc[r0:r1, c * LANES:(c + 1) * LANES] - ck_s[hs, rows, :]
                                   for c in range(l0 // LANES, l1 // LANES)], axis=1)
            if j == i:
                key = lax.broadcasted_iota(jnp.int32, blk.shape, 0) + r0
                qry = lax.broadcasted_iota(jnp.int32, blk.shape, 1) + l0
                blk = jnp.where(key <= qry, blk, -jnp.inf)
            sc_s[slot, rows, l0:l1] = blk
            part = blk[0:SUBLANES, :]
            for r in range(1, (r1 - r0) // SUBLANES):
                part = jnp.maximum(part, blk[r * SUBLANES:(r + 1) * SUBLANES, :])
            if j == 0 and n_piece == 0:
                m_s[slot] = part
            else:
                m_s[slot, :, l0:l1] = jnp.maximum(m_s[slot, :, l0:l1], part)

    def finish_scores(n):
        hs, i = tasks[n]
        slot = n % 2
        m = jnp.max(m_s[slot], axis=0, keepdims=True)
        cq_i = cq_s[hs, :, i * tq:(i + 1) * tq]
        off_s[slot] = (m + cq_i) - cq_i

    def exp_tile(n, j):
        slot = n % 2
        i = tasks[n][1]
        for r0, r1, l0, l1 in pieces(i, j):
            rows = slice(j * tq + r0, j * tq + r1)
            p_s[slot, rows, l0:l1] = jnp.exp2(sc_s[slot, rows, l0:l1]
                                              - off_s[slot, :, l0:l1]).astype(BF16)
        if j == i:
            half = tq // 2
            p_s[slot, j * tq + half:(j + 1) * tq, 0:half] = jnp.zeros((half, half), BF16)

    def weighted_values(n):
        hs, i = tasks[n]
        slot = n % 2
        qs, qe = i * tq, (i + 1) * tq
        o = jnp.dot(vt_s[hs, :, 0:qe], p_s[slot, 0:qe, :], preferred_element_type=F32)
        sum_row = (1 - hs % 2) * HEAD_DIM
        o = o / o[sum_row:sum_row + 1, :]
        if hs % 2 == 0:
            o0_s[i] = o
        else:
            row = lax.broadcasted_iota(jnp.int32, (LANES, 1), 0)
            o_ref[hs // 2, 0, qs:qe, :] = jnp.where(row < HEAD_DIM, o0_s[i], o).T.astype(BF16)

    for j in range(tasks[0][1] + 1):
        score_tile(0, j)
    finish_scores(0)
    for n in range(len(tasks)):
        n_exp = tasks[n][1] + 1
        n_score = tasks[n + 1][1] + 1 if n + 1 < len(tasks) else 0
        for j in range(max(n_exp, n_score)):
            if j < n_score:
                score_tile(n + 1, j)
            if j < n_exp:
                exp_tile(n, j)
        if n_score:
            finish_scores(n + 1)
        weighted_values(n)


def _fox_attn(q3, k3, v3, f3, bf, tq):
    _, b, s, _ = q3.shape
    blk = pl.BlockSpec((ATTN_PAIRS, 1, s, LANES), lambda bi, pi: (pi, bi, 0, 0))
    nh = 2 * ATTN_PAIRS
    return pl.pallas_call(
        functools.partial(_fox_attn_kernel, s=s, tq=tq),
        grid=(b, N_PAIRS // ATTN_PAIRS),
        in_specs=[blk, blk, blk, pl.BlockSpec((1, s, LANES), lambda bi, pi: (bi, 0, 0)),
                  _const_spec((1, LANES))],
        out_specs=blk,
        out_shape=jax.ShapeDtypeStruct((N_PAIRS, b, s, LANES), BF16),
        scratch_shapes=[pltpu.VMEM((s, LANES), F32), pltpu.VMEM((LANES, s), F32),
                        pltpu.VMEM((nh, s, LANES), BF16), pltpu.VMEM((nh, LANES, s), BF16),
                        pltpu.VMEM((nh, s, LANES), F32), pltpu.VMEM((nh, 1, s), F32),
                        pltpu.VMEM((2, s, tq), F32), pltpu.VMEM((2, s, tq), BF16),
                        pltpu.VMEM((2, SUBLANES, tq), F32), pltpu.VMEM((2, 1, tq), F32),
                        pltpu.VMEM((s // tq, LANES, tq), F32)],
        compiler_params=pltpu.CompilerParams(dimension_semantics=("arbitrary", "arbitrary"),
                                             vmem_limit_bytes=VMEM_LIMIT),
        name="fox_attn",
    )(q3, k3, v3, f3, bf)


FF_CHUNK = 512
N_MERGE_STAGES = 6


def _merge_ffn_kernel(x_ref, pa_ref, yb_ref, gab_ref, bm_ref, wpb_ref, wout_ref, gpost_ref,
                      gpre2_ref, gpost2_ref, perm_ref, wup_ref, cw_ref, cb_ref, wdn_ref, o_ref,
                      tail, unperm, x1_s, hp_s, *, tm, tiles_per_seq):
    i = pl.program_id(0)
    seg = tm // SUBLANES
    pitch = _scan_pitch(tm)

    @pl.when(i == 0)
    def _():
        x1_s[...] = jnp.zeros_like(x1_s)
        hp_s[...] = jnp.zeros_like(hp_s)
        tail[...] = jnp.zeros_like(tail)

    @pl.when((i + tiles_per_seq - 1) % tiles_per_seq == 0)
    def _():
        tail[...] = jnp.zeros_like(tail)

    x1_prev = x1_s[...]
    hp = hp_s[...]

    def merge_stage(k, st):
        if k == 0:
            st["gates"] = jax.nn.sigmoid(gab_ref[...] + bm_ref[...])
        elif k == 1:
            yb = jnp.concatenate([yb_ref[p] for p in range(N_PAIRS)], axis=1)
            st["pb"] = jnp.dot(yb, wpb_ref[:, :D_MODEL], preferred_element_type=F32)
        elif k == 2:
            g = st.pop("gates")
            merged = g[:, :D_MODEL] * pa_ref[...] + g[:, D_MODEL:] * st.pop("pb")
            st["merged"] = merged.astype(BF16)
        elif k == 3:
            st["mix"] = jnp.dot(st.pop("merged"), wout_ref[:, :D_MODEL], preferred_element_type=F32)
        elif k == 4:
            x1_s[...] = x_ref[...] + _rms(st.pop("mix"), gpost_ref[...])
        elif k == 5:
            h = _rms(x1_s[...], gpre2_ref[...]).astype(BF16)
            hp_s[...] = jnp.dot(perm_ref[...], h, preferred_element_type=F32).astype(BF16)

    first = lax.broadcasted_iota(jnp.int32, (SUBLANES, FF_CHUNK), 0) == 0

    def up_proj(c):
        return [jnp.dot(hp, wup_ref[:, off + c * FF_CHUNK:off + (c + 1) * FF_CHUNK],
                        preferred_element_type=F32) for off in (0, D_FF)]

    def conv(u, cols):
        prev = []
        for g in range(CONV_F - 1):
            old = tail[g * SUBLANES:(g + 1) * SUBLANES, cols]
            cur = u[tm - (2 - g) * SUBLANES:tm - (1 - g) * SUBLANES, :]
            prev.append(jnp.where(first, pltpu.roll(old, 1, axis=0), pltpu.roll(cur, 1, axis=0)))
        tail[:, cols] = u[tm - (CONV_F - 1) * SUBLANES:, :]
        u1 = jnp.concatenate([prev[1], u[:tm - SUBLANES, :]], axis=0)
        u2 = jnp.concatenate([prev[0], prev[1], u[:tm - 2 * SUBLANES, :]], axis=0)
        return (cb_ref[:, cols] + cw_ref[2:3, cols] * u + cw_ref[1:2, cols] * u1
                + cw_ref[0:1, cols] * u2)

    n_chunks = D_FF // FF_CHUNK
    stages = {}
    acc = jnp.zeros((tm, D_MODEL), F32)
    u_next = up_proj(0)
    for c in range(n_chunks):
        u_gate, u_val = u_next
        if c + 1 < n_chunks:
            u_next = up_proj(c + 1)
        for k in range(c * N_MERGE_STAGES // n_chunks, (c + 1) * N_MERGE_STAGES // n_chunks):
            merge_stage(k, stages)
        gate = conv(u_gate, slice(c * FF_CHUNK, (c + 1) * FF_CHUNK))
        val = conv(u_val, slice(D_FF + c * FF_CHUNK, D_FF + (c + 1) * FF_CHUNK))
        act = (jax.nn.gelu(gate) * val).astype(BF16)
        acc = acc + jnp.dot(act, wdn_ref[c * FF_CHUNK:(c + 1) * FF_CHUNK, :D_MODEL],
                            preferred_element_type=F32)

    for j in range(seg):
        for c in range(D_MODEL // LANES):
            unperm[c, pl.ds(j, SUBLANES, stride=pitch), :] = (
                acc[j * SUBLANES:(j + 1) * SUBLANES, c * LANES:(c + 1) * LANES])
    ffn = jnp.concatenate(
        [jnp.concatenate([unperm[c, s * pitch:s * pitch + seg, :] for s in range(SUBLANES)], axis=0)
         for c in range(D_MODEL // LANES)], axis=1)
    o_ref[...] = x1_prev + _rms(ffn, gpost2_ref[...])


def _merge_ffn(x2, pa2, yb2, gab2, bm, wpb, wout, gpost, gpre2, gpost2, wup, cw, cb, wdn,
               tm, seq):
    t = x2.shape[0]
    n_tiles = t // tm
    row = lambda n: pl.BlockSpec((tm, n), lambda i: (jnp.minimum(i, n_tiles - 1), 0))
    r = jnp.arange(tm)
    perm = (r[None, :] == ((r % SUBLANES) * (tm // SUBLANES) + r // SUBLANES)[:, None]).astype(BF16)
    return pl.pallas_call(
        functools.partial(_merge_ffn_kernel, tm=tm, tiles_per_seq=seq // tm),
        grid=(n_tiles + 1,),
        in_specs=[row(D_MODEL), row(D_MODEL),
                  pl.BlockSpec((N_PAIRS, tm, LANES), lambda i: (0, jnp.minimum(i, n_tiles - 1), 0)),
                  row(2 * D_MODEL),
                  _const_spec((1, 2 * D_MODEL)), _const_spec((D_ATT, _padded(D_MODEL))),
                  _const_spec((D_MODEL, _padded(D_MODEL))), _const_spec((1, D_MODEL)),
                  _const_spec((1, D_MODEL)), _const_spec((1, D_MODEL)), _const_spec((tm, tm)),
                  _const_spec((D_MODEL, _padded(2 * D_FF))), _const_spec((CONV_F, 2 * D_FF)),
                  _const_spec((1, 2 * D_FF)), _const_spec((D_FF, _padded(D_MODEL)))],
        out_specs=pl.BlockSpec((tm, D_MODEL), lambda i: (jnp.maximum(i - 1, 0), 0)),
        out_shape=jax.ShapeDtypeStruct((t, D_MODEL), F32),
        scratch_shapes=[pltpu.VMEM(((CONV_F - 1) * SUBLANES, 2 * D_FF), F32),
                        pltpu.VMEM((D_MODEL // LANES, SUBLANES * _scan_pitch(tm), LANES), F32),
                        pltpu.VMEM((tm, D_MODEL), F32), pltpu.VMEM((tm, D_MODEL), BF16)],
        compiler_params=pltpu.CompilerParams(dimension_semantics=("arbitrary",),
                                             vmem_limit_bytes=VMEM_LIMIT),
        name="merge_ffn",
    )(x2, pa2, yb2, gab2, bm, wpb, wout, gpost, gpre2, gpost2, perm, wup, cw, cb, wdn)


def _block_diag(w):
    h, n, _ = w.shape
    col = jnp.arange(h * n)
    tile = (col[None, :] % n == jnp.arange(n)[:, None]).astype(w.dtype)
    rep = jnp.dot(w.reshape(h * n, n), tile, precision=lax.Precision.HIGHEST)
    return jnp.where(col[:, None] // n == col[None, :] // n, rep, 0.0)


def _gate_windows(w_a, w_x):
    da, dx = _block_diag(w_a), _block_diag(w_x)
    tiles = []
    for j, ks in enumerate(GATE_STARTS):
        cols = slice(j * GATE_TILE, (j + 1) * GATE_TILE)
        tiles.append(jnp.concatenate([da[ks:ks + GATE_WIN, cols], dx[ks:ks + GATE_WIN, cols]],
                                     axis=1))
    return jnp.stack([_bf16_weight(t) for t in tiles])


def _layer(x, mix_norm_pre, mix_norm_post, w_in, conv_a_w, conv_a_b, w_rg_a, b_rg_a, w_rg_x,
           b_rg_x, lru_lambda, b_forget, b_merge, w_proj_a, w_proj_b, w_out, ffn_norm_pre,
           ffn_norm_post, w_up, conv_f_w, conv_f_b, w_down):
    bsz, seq, _ = x.shape
    t = bsz * seq
    row = lambda a: a.reshape(1, -1)

    w_main = _bf16_weight(w_in[:, :_OFF_F])
    w_gate = _bf16_weight(w_in[:, _OFF_G:])
    w_f = jnp.pad(w_in[:, _OFF_F:_OFF_G].astype(BF16), ((0, 0), (0, LANES - H_ATT)))
    bf = jnp.concatenate([b_forget, jnp.zeros((LANES - H_ATT,), b_forget.dtype)]).reshape(1, LANES)
    wg = _gate_windows(w_rg_a, w_rg_x)

    x2 = x.reshape(t, D_MODEL)
    q, k, v, gab, f, pa = _mix_in(x2, row(mix_norm_pre), w_main, w_gate, w_f, conv_a_w,
                                  row(conv_a_b), wg, row(b_rg_a), row(b_rg_x), row(lru_lambda),
                                  _bf16_weight(w_proj_a), tc=MIX_IN_ROWS, seq=seq)

    p4 = lambda a: a.reshape(N_PAIRS, bsz, seq, LANES)
    yb = _fox_attn(p4(q), p4(k), p4(v), f.reshape(bsz, seq, LANES), bf, tq=ATTN_Q_ROWS)

    out = _merge_ffn(x2, pa, yb.reshape(N_PAIRS, t, LANES), gab, row(b_merge),
                     _bf16_weight(w_proj_b), _bf16_weight(w_out), row(mix_norm_post),
                     row(ffn_norm_pre), row(ffn_norm_post), _bf16_weight(w_up), conv_f_w,
                     row(conv_f_b), _bf16_weight(w_down), tm=FFN_ROWS, seq=seq)
    return out.reshape(bsz, seq, D_MODEL)


def kernel(x, mix_norm_pre, mix_norm_post, w_in, conv_a_w, conv_a_b, w_rg_a, b_rg_a, w_rg_x, b_rg_x, lru_lambda, b_forget, b_merge, w_proj_a, w_proj_b, w_out, ffn_norm_pre, ffn_norm_post, w_up, conv_f_w, conv_f_b, w_down):
    depth = w_in.shape[0]
    for layer in range(depth):
        x = _layer(x, mix_norm_pre[layer], mix_norm_post[layer], w_in[layer], conv_a_w[layer],
                   conv_a_b[layer], w_rg_a[layer], b_rg_a[layer], w_rg_x[layer], b_rg_x[layer],
                   lru_lambda[layer], b_forget[layer], b_merge[layer], w_proj_a[layer],
                   w_proj_b[layer], w_out[layer], ffn_norm_pre[layer], ffn_norm_post[layer],
                   w_up[layer], conv_f_w[layer], conv_f_b[layer], w_down[layer])
    return x
```

```python
import functools
import math

import jax
import jax.numpy as jnp
from jax import lax
from jax.experimental import pallas as pl
from jax.experimental.pallas import tpu as pltpu

D_MODEL = 1024
D_RNN = 1280
H_RNN = 16
RNN_BLOCK = D_RNN // H_RNN
CONV_A = 4
LRU_C = 8.0
H_ATT = 16
HEAD_DIM = 64
D_ATT = H_ATT * HEAD_DIM
N_PAIRS = H_ATT // 2
D_FF = 3 * D_MODEL
CONV_F = 3
RMS_EPS = 1e-6

LANES = 128
SUBLANES = 8
MXU_COLS = 256
VMEM_LIMIT = 56 * 1024 * 1024

MIX_IN_ROWS = 256
ATTN_Q_ROWS = 256
ATTN_PAIRS = 4
FFN_ROWS = 256

LOG2E = math.log2(math.e)

F32 = jnp.float32
BF16 = jnp.bfloat16

_OFF_XA = 0
_OFF_GA = _OFF_XA + D_RNN
_OFF_Q = _OFF_GA + D_RNN
_OFF_K = _OFF_Q + D_ATT
_OFF_V = _OFF_K + D_ATT
_OFF_F = _OFF_V + D_ATT
_OFF_G = _OFF_F + H_ATT

GATE_TILE = MXU_COLS
GATE_WIN = 2 * MXU_COLS
N_GATE_TILES = D_RNN // GATE_TILE


def _gate_window_start(j):
    lo = (j * GATE_TILE // RNN_BLOCK) * RNN_BLOCK
    start = min((lo // LANES) * LANES, D_RNN - GATE_WIN)
    hi = -(-((j + 1) * GATE_TILE) // RNN_BLOCK) * RNN_BLOCK
    assert start <= lo and hi <= start + GATE_WIN
    return start


GATE_STARTS = tuple(_gate_window_start(j) for j in range(N_GATE_TILES))


def _rms(x, gain):
    y = x * lax.rsqrt(jnp.mean(x * x, axis=-1, keepdims=True) + RMS_EPS)
    return y * gain


def _padded(n):
    stride = n * SUBLANES // (2 * LANES)
    return n + LANES if stride % SUBLANES == 0 else n


def _bf16_weight(w):
    n = w.shape[1]
    return jnp.pad(w, ((0, 0), (0, _padded(n) - n))).astype(BF16)


def _const_spec(shape):
    nd = len(shape)
    return pl.BlockSpec(shape, lambda *_: (0,) * nd, pipeline_mode=pl.Buffered(1))


def _scan_pitch(tc):
    seg = tc // SUBLANES
    return seg + SUBLANES if (seg // SUBLANES) % 2 == 0 else seg


def _mix_in_kernel(x_ref, g_ref, w_ref, wgate_ref, wf_ref, cw_ref, cb_ref, wg_ref, ba_ref, bx_ref,
                   lam_ref, wp_ref, q_ref, k_ref, v_ref, gab_ref, f_ref, pa_ref,
                   xpad, ga_s, xc_s, xcb_s, a_s, b_s, h_s, p_s, y_s, hprev,
                   *, tc, tiles_per_seq):
    seg = tc // SUBLANES
    pitch = _scan_pitch(tc)
    n_lane = D_RNN // LANES
    step = pl.program_id(0)

    @pl.when(step == 0)
    def _():
        xpad[...] = jnp.zeros_like(xpad)
        ga_s[...] = jnp.zeros_like(ga_s)
        hprev[...] = jnp.zeros_like(hprev)

    @pl.when((step + tiles_per_seq - 1) % tiles_per_seq == 0)
    def _():
        xpad[0:SUBLANES, :] = jnp.zeros((SUBLANES, D_RNN), F32)
        hprev[...] = jnp.zeros_like(hprev)

    h = _rms(x_ref[...], g_ref[...]).astype(BF16)

    def mm(lo, hi):
        return jnp.dot(h, w_ref[:, lo:hi], preferred_element_type=F32)

    def proj_chunk(kind, c):
        lo, hi = c * MXU_COLS, (c + 1) * MXU_COLS

        def put_pairs(ref, val):
            for p in range(MXU_COLS // LANES):
                ref[c * (MXU_COLS // LANES) + p] = val[:, p * LANES:(p + 1) * LANES]

        if kind == "q":
            put_pairs(q_ref, (mm(_OFF_Q + lo, _OFF_Q + hi)
                              * (LOG2E / math.sqrt(HEAD_DIM))).astype(BF16))
        elif kind == "k":
            put_pairs(k_ref, mm(_OFF_K + lo, _OFF_K + hi).astype(BF16))
        elif kind == "v":
            put_pairs(v_ref, mm(_OFF_V + lo, _OFF_V + hi).astype(BF16))
        elif kind == "g":
            gab_ref[:, lo:hi] = jnp.dot(h, wgate_ref[:, lo:hi], preferred_element_type=F32)
        elif kind == "xa":
            xpad[SUBLANES:SUBLANES + tc, lo:hi] = mm(_OFF_XA + lo, _OFF_XA + hi)
        elif kind == "ga":
            ga_s[:, lo:hi] = mm(_OFF_GA + lo, _OFF_GA + hi)
        else:
            f_ref[...] = jnp.dot(h, wf_ref[...], preferred_element_type=F32)

    pending = ([("q", c) for c in range(D_ATT // MXU_COLS)]
               + [("k", c) for c in range(D_ATT // MXU_COLS)]
               + [("v", c) for c in range(D_ATT // MXU_COLS)]
               + [("g", c) for c in range(2 * D_MODEL // MXU_COLS)] + [("f", 0)]
               + [("xa", c) for c in range(D_RNN // MXU_COLS)])

    def emit(n):
        for _ in range(min(n, len(pending))):
            proj_chunk(*pending.pop(0))

    for j in range(N_GATE_TILES):
        cols = slice(j * GATE_TILE, (j + 1) * GATE_TILE)
        xc = cb_ref[:, cols] + cw_ref[CONV_A - 1:CONV_A, cols] * xpad[SUBLANES:SUBLANES + tc, cols]
        for kk in range(CONV_A - 1):
            sh = CONV_A - 1 - kk
            xc = xc + cw_ref[kk:kk + 1, cols] * xpad[SUBLANES - sh:SUBLANES - sh + tc, cols]
        xc_s[:, cols] = xc
        xcb_s[:, cols] = xc.astype(BF16)
        emit(1)
    xpad[0:SUBLANES, :] = xpad[tc:tc + SUBLANES, :]

    sp = jax.nn.softplus(-lam_ref[...])
    lane_chunks = GATE_TILE // LANES
    for j in range(N_GATE_TILES):
        ks = GATE_STARTS[j]
        cols = slice(j * GATE_TILE, (j + 1) * GATE_TILE)
        g = jnp.dot(xcb_s[:, ks:ks + GATE_WIN], wg_ref[j, :, :2 * GATE_TILE],
                    preferred_element_type=F32)
        r = jax.nn.sigmoid(g[:, :GATE_TILE] + ba_ref[:, cols])
        i = jax.nn.sigmoid(g[:, GATE_TILE:] + bx_ref[:, cols])
        log_a = -LRU_C * r * sp[:, cols]
        a = jnp.exp(log_a)
        b = jnp.sqrt(-jnp.tanh(log_a) * (a * a + 1.0)) * (i * xc_s[:, cols])
        for c in range(lane_chunks):
            slab = j * lane_chunks + c
            lanes = slice(c * LANES, (c + 1) * LANES)
            for s in range(SUBLANES):
                dst = slice(s * pitch, s * pitch + seg)
                a_s[slab, dst, :] = a[s * seg:(s + 1) * seg, lanes]
                b_s[slab, dst, :] = b[s * seg:(s + 1) * seg, lanes]
        emit(3)

    h8 = [jnp.zeros((SUBLANES, LANES), F32) for _ in range(n_lane)]
    p8 = [jnp.ones((SUBLANES, LANES), F32) for _ in range(n_lane)]
    for j in range(seg):
        rows = pl.ds(j, SUBLANES, stride=pitch)
        for c in range(n_lane):
            a_j = a_s[c, rows, :]
            h8[c] = a_j * h8[c] + b_s[c, rows, :]
            p8[c] = p8[c] * a_j
            h_s[c, rows, :] = h8[c]
            p_s[c, rows, :] = p8[c]
        if j % 4 == 3:
            emit(1)

    pa = jnp.zeros((tc, D_MODEL), F32)
    for c in range(n_lane):
        cols = slice(c * LANES, (c + 1) * LANES)
        start = hprev[:, cols]
        for s in range(SUBLANES):
            rows = slice(s * seg, (s + 1) * seg)
            src = slice(s * pitch, s * pitch + seg)
            hfin = h_s[c, src, :] + p_s[c, src, :] * start
            y_s[rows, cols] = (jax.nn.gelu(ga_s[rows, cols]) * hfin).astype(BF16)
            start = h8[c][s:s + 1, :] + p8[c][s:s + 1, :] * start
        hprev[:, cols] = start
        emit(1)
        if c % lane_chunks == lane_chunks - 1:
            proj_chunk("ga", c // lane_chunks)
            slab = slice((c - lane_chunks + 1) * LANES, (c + 1) * LANES)
            pa = pa + jnp.dot(y_s[:, slab], wp_ref[slab, :D_MODEL], preferred_element_type=F32)
    emit(len(pending))
    pa_ref[...] = pa


def _mix_in(x2, gain, w_main, w_gate, w_f, cw, cb, wg, ba, bx, lam, wp, tc, seq):
    t = x2.shape[0]
    n_tiles = t // tc
    row = lambda n: pl.BlockSpec((tc, n), lambda i: (jnp.minimum(i, n_tiles - 1), 0))
    pairs = pl.BlockSpec((N_PAIRS, tc, LANES), lambda i: (0, jnp.minimum(i, n_tiles - 1), 0))
    scan_shape = (D_RNN // LANES, SUBLANES * _scan_pitch(tc), LANES)
    consts = (gain, w_main, w_gate, w_f, cw, cb, wg, ba, bx, lam, wp)
    return pl.pallas_call(
        functools.partial(_mix_in_kernel, tc=tc, tiles_per_seq=seq // tc),
        grid=(n_tiles + 1,),
        in_specs=[row(D_MODEL)] + [_const_spec(c.shape) for c in consts],
        out_specs=[pairs, pairs, pairs, row(2 * D_MODEL), row(LANES),
                   pl.BlockSpec((tc, D_MODEL), lambda i: (jnp.maximum(i - 1, 0), 0))],
        out_shape=[jax.ShapeDtypeStruct((N_PAIRS, t, LANES), BF16),
                   jax.ShapeDtypeStruct((N_PAIRS, t, LANES), BF16),
                   jax.ShapeDtypeStruct((N_PAIRS, t, LANES), BF16),
                   jax.ShapeDtypeStruct((t, 2 * D_MODEL), F32),
                   jax.ShapeDtypeStruct((t, LANES), F32),
                   jax.ShapeDtypeStruct((t, D_MODEL), F32)],
        scratch_shapes=[pltpu.VMEM((tc + SUBLANES, D_RNN), F32), pltpu.VMEM((tc, D_RNN), F32),
                        pltpu.VMEM((tc, D_RNN), F32), pltpu.VMEM((tc, D_RNN), BF16),
                        pltpu.VMEM(scan_shape, F32), pltpu.VMEM(scan_shape, F32),
                        pltpu.VMEM(scan_shape, F32), pltpu.VMEM(scan_shape, F32),
                        pltpu.VMEM((tc, D_RNN), BF16), pltpu.VMEM((1, D_RNN), F32)],
        compiler_params=pltpu.CompilerParams(dimension_semantics=("arbitrary",),
                                             vmem_limit_bytes=VMEM_LIMIT),
        name="mix_in",
    )(x2, *consts)


def _fox_attn_kernel(q_ref, k_ref, v_ref, f_ref, bf_ref, o_ref, cum, cum_t, km_s, vt_s, ck_s,
                     cq_s, sc_s, p_s, m_s, off_s, o0_s, *, s, tq):
    pair = pl.program_id(1)

    @pl.when(pair == 0)
    def _():
        c = jax.nn.log_sigmoid(f_ref[0] + bf_ref[...])
        row = lax.broadcasted_iota(jnp.int32, (s, LANES), 0)
        d = 1
        while d < s:
            c = c + jnp.where(row >= d, pltpu.roll(c, d, axis=0), 0.0)
            d *= 2
        c = c * LOG2E
        cum[...] = c
        cum_t[...] = c.T

    lane = lax.broadcasted_iota(jnp.int32, (1, LANES), 1)
    n_q = s // tq

    n_pairs = q_ref.shape[0]
    v_row = lax.broadcasted_iota(jnp.int32, (LANES, 1), 0)
    for pp in range(n_pairs):
        k = k_ref[pp, 0]
        v_t = v_ref[pp, 0].astype(F32).T
        for hh in range(2):
            hs = 2 * pp + hh
            head = 2 * (n_pairs * pair + pp) + hh
            in_head = (lane >= hh * HEAD_DIM) & (lane < (hh + 1) * HEAD_DIM)
            km_s[hs] = jnp.where(in_head, k, jnp.zeros_like(k))
            own_rows = (v_row >= hh * HEAD_DIM) & (v_row < (hh + 1) * HEAD_DIM)
            vt_s[hs] = jnp.where(own_rows, v_t, 1.0).astype(BF16)
            ck = jnp.sum(jnp.where(lane == head, cum[...], 0.0), axis=-1, keepdims=True)
            ck_s[hs] = jnp.broadcast_to(ck, (s, LANES))
            cq_s[hs] = cum_t[pl.ds(head, 1), :]

    tasks = [(hs, i) for hs in range(2 * n_pairs) for i in range(n_q)]

    def pieces(i, j):
        if j == i:
            return [(0, tq // 2, 0, tq), (tq // 2, tq, tq // 2, tq)]
        return [(0, tq, 0, tq)]

    def score_tile(n, j):
        hs, i = tasks[n]
        slot = n % 2
        ks = slice(j * tq, (j + 1) * tq)
        sc = lax.dot_general(km_s[hs, ks, :], q_ref[hs // 2, 0, i * tq:(i + 1) * tq, :],
                             (((1,), (1,)), ((), ())), preferred_element_type=F32)
        for n_piece, (r0, r1, l0, l1) in enumerate(pieces(i, j)):
            rows = slice(j * tq + r0, j * tq + r1)
            blk = jnp.concatenate([sc[r0:r1, c * LANES:(c + 1) * LANES] - ck_s[hs, rows, :]
                                   for c in range(l0 // LANES, l1 // LANES)], axis=1)
            if j == i:
                key = lax.broadcasted_iota(jnp.int32, blk.shape, 0) + r0
                qry = lax.broadcasted_iota(jnp.int32, blk.shape, 1) + l0
                blk = jnp.where(key <= qry, blk, -jnp.inf)
            sc_s[slot, rows, l0:l1] = blk
            part = blk[0:SUBLANES, :]
            for r in range(1, (r1 - r0) // SUBLANES):
                part = jnp.maximum(part, blk[r * SUBLANES:(r + 1) * SUBLANES, :])
            if j == 0 and n_piece == 0:
                m_s[slot] = part
            else:
                m_s[slot, :, l0:l1] = jnp.maximum(m_s[slot, :, l0:l1], part)

    def finish_scores(n):
        hs, i = tasks[n]
        slot = n % 2
        m = jnp.max(m_s[slot], axis=0, keepdims=True)
        cq_i = cq_s[hs, :, i * tq:(i + 1) * tq]
        off_s[slot] = (m + cq_i) - cq_i

    def exp_tile(n, j):
        slot = n % 2
        i = tasks[n][1]
        for r0, r1, l0, l1 in pieces(i, j):
            rows = slice(j * tq + r0, j * tq + r1)
            p_s[slot, rows, l0:l1] = jnp.exp2(sc_s[slot, rows, l0:l1]
                                              - off_s[slot, :, l0:l1]).astype(BF16)
        if j == i:
            half = tq // 2
            p_s[slot, j * tq + half:(j + 1) * tq, 0:half] = jnp.zeros((half, half), BF16)

    def weighted_values(n):
        hs, i = tasks[n]
        slot = n % 2
        qs, qe = i * tq, (i + 1) * tq
        o = jnp.dot(vt_s[hs, :, 0:qe], p_s[slot, 0:qe, :], preferred_element_type=F32)
        sum_row = (1 - hs % 2) * HEAD_DIM
        o = o / o[sum_row:sum_row + 1, :]
        if hs % 2 == 0:
            o0_s[i] = o
        else:
            row = lax.broadcasted_iota(jnp.int32, (LANES, 1), 0)
            o_ref[hs // 2, 0, qs:qe, :] = jnp.where(row < HEAD_DIM, o0_s[i], o).T.astype(BF16)

    for j in range(tasks[0][1] + 1):
        score_tile(0, j)
    finish_scores(0)
    for n in range(len(tasks)):
        n_exp = tasks[n][1] + 1
        n_score = tasks[n + 1][1] + 1 if n + 1 < len(tasks) else 0
        for j in range(max(n_exp, n_score)):
            if j < n_score:
                score_tile(n + 1, j)
            if j < n_exp:
                exp_tile(n, j)
        if n_score:
            finish_scores(n + 1)
        weighted_values(n)


def _fox_attn(q3, k3, v3, f3, bf, tq):
    _, b, s, _ = q3.shape
    blk = pl.BlockSpec((ATTN_PAIRS, 1, s, LANES), lambda bi, pi: (pi, bi, 0, 0))
    nh = 2 * ATTN_PAIRS
    return pl.pallas_call(
        functools.partial(_fox_attn_kernel, s=s, tq=tq),
        grid=(b, N_PAIRS // ATTN_PAIRS),
        in_specs=[blk, blk, blk, pl.BlockSpec((1, s, LANES), lambda bi, pi: (bi, 0, 0)),
                  _const_spec((1, LANES))],
        out_specs=blk,
        out_shape=jax.ShapeDtypeStruct((N_PAIRS, b, s, LANES), BF16),
        scratch_shapes=[pltpu.VMEM((s, LANES), F32), pltpu.VMEM((LANES, s), F32),
                        pltpu.VMEM((nh, s, LANES), BF16), pltpu.VMEM((nh, LANES, s), BF16),
                        pltpu.VMEM((nh, s, LANES), F32), pltpu.VMEM((nh, 1, s), F32),
                        pltpu.VMEM((2, s, tq), F32), pltpu.VMEM((2, s, tq), BF16),
                        pltpu.VMEM((2, SUBLANES, tq), F32), pltpu.VMEM((2, 1, tq), F32),
                        pltpu.VMEM((s // tq, LANES, tq), F32)],
        compiler_params=pltpu.CompilerParams(dimension_semantics=("arbitrary", "arbitrary"),
                                             vmem_limit_bytes=VMEM_LIMIT),
        name="fox_attn",
    )(q3, k3, v3, f3, bf)


FF_CHUNK = 512
N_MERGE_STAGES = 6


def _merge_ffn_kernel(x_ref, pa_ref, yb_ref, gab_ref, bm_ref, wpb_ref, wout_ref, gpost_ref,
                      gpre2_ref, gpost2_ref, perm_ref, wup_ref, cw_ref, cb_ref, wdn_ref, o_ref,
                      tail, unperm, x1_s, hp_s, *, tm, tiles_per_seq):
    i = pl.program_id(0)
    seg = tm // SUBLANES
    pitch = _scan_pitch(tm)

    @pl.when(i == 0)
    def _():
        x1_s[...] = jnp.zeros_like(x1_s)
        hp_s[...] = jnp.zeros_like(hp_s)
        tail[...] = jnp.zeros_like(tail)

    @pl.when((i + tiles_per_seq - 1) % tiles_per_seq == 0)
    def _():
        tail[...] = jnp.zeros_like(tail)

    x1_prev = x1_s[...]
    hp = hp_s[...]

    def merge_stage(k, st):
        if k == 0:
            st["gates"] = jax.nn.sigmoid(gab_ref[...] + bm_ref[...])
        elif k == 1:
            yb = jnp.concatenate([yb_ref[p] for p in range(N_PAIRS)], axis=1)
            st["pb"] = jnp.dot(yb, wpb_ref[:, :D_MODEL], preferred_element_type=F32)
        elif k == 2:
            g = st.pop("gates")
            merged = g[:, :D_MODEL] * pa_ref[...] + g[:, D_MODEL:] * st.pop("pb")
            st["merged"] = merged.astype(BF16)
        elif k == 3:
            st["mix"] = jnp.dot(st.pop("merged"), wout_ref[:, :D_MODEL], preferred_element_type=F32)
        elif k == 4:
            x1_s[...] = x_ref[...] + _rms(st.pop("mix"), gpost_ref[...])
        elif k == 5:
            h = _rms(x1_s[...], gpre2_ref[...]).astype(BF16)
            hp_s[...] = jnp.dot(perm_ref[...], h, preferred_element_type=F32).astype(BF16)

    first = lax.broadcasted_iota(jnp.int32, (SUBLANES, FF_CHUNK), 0) == 0

    def up_proj(c):
        return [jnp.dot(hp, wup_ref[:, off + c * FF_CHUNK:off + (c + 1) * FF_CHUNK],
                        preferred_element_type=F32) for off in (0, D_FF)]

    def conv(u, cols):
        prev = []
        for g in range(CONV_F - 1):
            old = tail[g * SUBLANES:(g + 1) * SUBLANES, cols]
            cur = u[tm - (2 - g) * SUBLANES:tm - (1 - g) * SUBLANES, :]
            prev.append(jnp.where(first, pltpu.roll(old, 1, axis=0), pltpu.roll(cur, 1, axis=0)))
        tail[:, cols] = u[tm - (CONV_F - 1) * SUBLANES:, :]
        u1 = jnp.concatenate([prev[1], u[:tm - SUBLANES, :]], axis=0)
        u2 = jnp.concatenate([prev[0], prev[1], u[:tm - 2 * SUBLANES, :]], axis=0)
        return (cb_ref[:, cols] + cw_ref[2:3, cols] * u + cw_ref[1:2, cols] * u1
                + cw_ref[0:1, cols] * u2)

    n_chunks = D_FF // FF_CHUNK
    stages = {}
    acc = jnp.zeros((tm, D_MODEL), F32)
    u_next = up_proj(0)
    for c in range(n_chunks):
        u_gate, u_val = u_next
        if c + 1 < n_chunks:
            u_next = up_proj(c + 1)
        for k in range(c * N_MERGE_STAGES // n_chunks, (c + 1) * N_MERGE_STAGES // n_chunks):
            merge_stage(k, stages)
        gate = conv(u_gate, slice(c * FF_CHUNK, (c + 1) * FF_CHUNK))
        val = conv(u_val, slice(D_FF + c * FF_CHUNK, D_FF + (c + 1) * FF_CHUNK))
        act = (jax.nn.gelu(gate) * val).astype(BF16)
        acc = acc + jnp.dot(act, wdn_ref[c * FF_CHUNK:(c + 1) * FF_CHUNK, :D_MODEL],
                            preferred_element_type=F32)

    for j in range(seg):
        for c in range(D_MODEL // LANES):
            unperm[c, pl.ds(j, SUBLANES, stride=pitch), :] = (
                acc[j * SUBLANES:(j + 1) * SUBLANES, c * LANES:(c + 1) * LANES])
    ffn = jnp.concatenate(
        [jnp.concatenate([unperm[c, s * pitch:s * pitch + seg, :] for s in range(SUBLANES)], axis=0)
         for c in range(D_MODEL // LANES)], axis=1)
    o_ref[...] = x1_prev + _rms(ffn, gpost2_ref[...])


def _merge_ffn(x2, pa2, yb2, gab2, bm, wpb, wout, gpost, gpre2, gpost2, wup, cw, cb, wdn,
               tm, seq):
    t = x2.shape[0]
    n_tiles = t // tm
    row = lambda n: pl.BlockSpec((tm, n), lambda i: (jnp.minimum(i, n_tiles - 1), 0))
    r = jnp.arange(tm)
    perm = (r[None, :] == ((r % SUBLANES) * (tm // SUBLANES) + r // SUBLANES)[:, None]).astype(BF16)
    return pl.pallas_call(
        functools.partial(_merge_ffn_kernel, tm=tm, tiles_per_seq=seq // tm),
        grid=(n_tiles + 1,),
        in_specs=[row(D_MODEL), row(D_MODEL),
                  pl.BlockSpec((N_PAIRS, tm, LANES), lambda i: (0, jnp.minimum(i, n_tiles - 1), 0)),
                  row(2 * D_MODEL),
                  _const_spec((1, 2 * D_MODEL)), _const_spec((D_ATT, _padded(D_MODEL))),
                  _const_spec((D_MODEL, _padded(D_MODEL))), _const_spec((1, D_MODEL)),
                  _const_spec((1, D_MODEL)), _const_spec((1, D_MODEL)), _const_spec((tm, tm)),
                  _const_spec((D_MODEL, _padded(2 * D_FF))), _const_spec((CONV_F, 2 * D_FF)),
                  _const_spec((1, 2 * D_FF)), _const_spec((D_FF, _padded(D_MODEL)))],
        out_specs=pl.BlockSpec((tm, D_MODEL), lambda i: (jnp.maximum(i - 1, 0), 0)),
        out_shape=jax.ShapeDtypeStruct((t, D_MODEL), F32),
        scratch_shapes=[pltpu.VMEM(((CONV_F - 1) * SUBLANES, 2 * D_FF), F32),
                        pltpu.VMEM((D_MODEL // LANES, SUBLANES * _scan_pitch(tm), LANES), F32),
                        pltpu.VMEM((tm, D_MODEL), F32), pltpu.VMEM((tm, D_MODEL), BF16)],
        compiler_params=pltpu.CompilerParams(dimension_semantics=("arbitrary",),
                                             vmem_limit_bytes=VMEM_LIMIT),
        name="merge_ffn",
    )(x2, pa2, yb2, gab2, bm, wpb, wout, gpost, gpre2, gpost2, perm, wup, cw, cb, wdn)


def _block_diag(w):
    h, n, _ = w.shape
    col = jnp.arange(h * n)
    tile = (col[None, :] % n == jnp.arange(n)[:, None]).astype(w.dtype)
    rep = jnp.dot(w.reshape(h * n, n), tile, precision=lax.Precision.HIGHEST)
    return jnp.where(col[:, None] // n == col[None, :] // n, rep, 0.0)


def _gate_windows(w_a, w_x):
    da, dx = _block_diag(w_a), _block_diag(w_x)
    tiles = []
    for j, ks in enumerate(GATE_STARTS):
        cols = slice(j * GATE_TILE, (j + 1) * GATE_TILE)
        tiles.append(jnp.concatenate([da[ks:ks + GATE_WIN, cols], dx[ks:ks + GATE_WIN, cols]],
                                     axis=1))
    return jnp.stack([_bf16_weight(t) for t in tiles])


def _layer(x, mix_norm_pre, mix_norm_post, w_in, conv_a_w, conv_a_b, w_rg_a, b_rg_a, w_rg_x,
           b_rg_x, lru_lambda, b_forget, b_merge, w_proj_a, w_proj_b, w_out, ffn_norm_pre,
           ffn_norm_post, w_up, conv_f_w, conv_f_b, w_down):
    bsz, seq, _ = x.shape
    t = bsz * seq
    row = lambda a: a.reshape(1, -1)

    w_main = _bf16_weight(w_in[:, :_OFF_F])
    w_gate = _bf16_weight(w_in[:, _OFF_G:])
    w_f = jnp.pad(w_in[:, _OFF_F:_OFF_G].astype(BF16), ((0, 0), (0, LANES - H_ATT)))
    bf = jnp.concatenate([b_forget, jnp.zeros((LANES - H_ATT,), b_forget.dtype)]).reshape(1, LANES)
    wg = _gate_windows(w_rg_a, w_rg_x)

    x2 = x.reshape(t, D_MODEL)
    q, k, v, gab, f, pa = _mix_in(x2, row(mix_norm_pre), w_main, w_gate, w_f, conv_a_w,
                                  row(conv_a_b), wg, row(b_rg_a), row(b_rg_x), row(lru_lambda),
                                  _bf16_weight(w_proj_a), tc=MIX_IN_ROWS, seq=seq)

    p4 = lambda a: a.reshape(N_PAIRS, bsz, seq, LANES)
    yb = _fox_attn(p4(q), p4(k), p4(v), f.reshape(bsz, seq, LANES), bf, tq=ATTN_Q_ROWS)

    out = _merge_ffn(x2, pa, yb.reshape(N_PAIRS, t, LANES), gab, row(b_merge),
                     _bf16_weight(w_proj_b), _bf16_weight(w_out), row(mix_norm_post),
                     row(ffn_norm_pre), row(ffn_norm_post), _bf16_weight(w_up), conv_f_w,
                     row(conv_f_b), _bf16_weight(w_down), tm=FFN_ROWS, seq=seq)
    return out.reshape(bsz, seq, D_MODEL)


def kernel(x, mix_norm_pre, mix_norm_post, w_in, conv_a_w, conv_a_b, w_rg_a, b_rg_a, w_rg_x, b_rg_x, lru_lambda, b_forget, b_merge, w_proj_a, w_proj_b, w_out, ffn_norm_pre, ffn_norm_post, w_up, conv_f_w, conv_f_b, w_down):
    depth = w_in.shape[0]
    for layer in range(depth):
        x = _layer(x, mix_norm_pre[layer], mix_norm_post[layer], w_in[layer], conv_a_w[layer],
                   conv_a_b[layer], w_rg_a[layer], b_rg_a[layer], w_rg_x[layer], b_rg_x[layer],
                   lru_lambda[layer], b_forget[layer], b_merge[layer], w_proj_a[layer],
                   w_proj_b[layer], w_out[layer], ffn_norm_pre[layer], ffn_norm_post[layer],
                   w_up[layer], conv_f_w[layer], conv_f_b[layer], w_down[layer])
    return x
```
